```python
import math
import jax, jax.numpy as jnp
from jax import lax
import numpy as np

D_MODEL = 1024
BATCH = 8
SEQ = 2048
DEPTH = 1
DEC_BATCH = 32
DEC_SEQ = 64
PAST_LEN = 2048

CHUNK = 64
EPS = 1e-6
MIX_W = D_MODEL
SSD_W = MIX_W // 2
S5_W = MIX_W - SSD_W
SSD_HEAD_DIM = 64
SSD_HEADS = SSD_W // SSD_HEAD_DIM
SSD_GROUPS = 2
SSD_STATE = 128
CONV_K = 4
CONV_CH = SSD_W + 2 * SSD_GROUPS * SSD_STATE
S5_GROUP_CH = 16
S5_GROUPS = S5_W // S5_GROUP_CH
S5_STATE = 64
N_PROJ = SSD_W + CONV_CH + SSD_HEADS + S5_W
MEM_LEN = 256
X_HEADS = 4
X_HEAD_DIM = D_MODEL // X_HEADS
N_EXPERT_GROUPS = 4
EXPERTS_PER_GROUP = 8
N_EXPERTS = N_EXPERT_GROUPS * EXPERTS_PER_GROUP
TOP_K_INNER = 2
EXPERT_FF = 256

kernel_name = "hymba_ssd_s5_memxattn_hmoe_stream_step"


def rmsnorm(x, g):
    xf = x.astype(jnp.float32)
    y = xf * lax.rsqrt(jnp.mean(xf * xf, axis=-1, keepdims=True) + EPS)
    return (y * g.astype(jnp.float32)).astype(x.dtype)


def segsum(a):
    t = a.shape[-1]
    cs = jnp.cumsum(a, axis=-1)
    diff = cs[..., :, None] - cs[..., None, :]
    mask = jnp.tril(jnp.ones((t, t), dtype=bool))
    return jnp.where(mask, diff, -jnp.inf)


def ssd_scan(xdt, a, bm, cm, h0, q):
    bsz, L, H, P = xdt.shape
    nc = L // q
    xc = xdt.reshape(bsz, nc, q, H, P)
    bc = bm.reshape(bsz, nc, q, H, -1)
    cc = cm.reshape(bsz, nc, q, H, -1)
    ac = a.reshape(bsz, nc, q, H).transpose(0, 3, 1, 2)
    a_cs = jnp.cumsum(ac, axis=-1)
    scores = jnp.einsum("bclhn,bcshn->bhcls", cc, bc) * jnp.exp(segsum(ac))
    y_diag = jnp.einsum("bhcls,bcshp->bclhp", scores, xc)
    decay_to_end = jnp.exp(a_cs[..., -1:] - a_cs)
    chunk_states = jnp.einsum("bclhn,bhcl,bclhp->bchpn", bc, decay_to_end, xc)
    chunk_states = jnp.concatenate([h0[:, None], chunk_states], axis=1)
    chunk_decay = jnp.exp(segsum(jnp.pad(a_cs[..., -1], ((0, 0), (0, 0), (1, 0)))))
    states = jnp.einsum("bhzc,bchpn->bzhpn", chunk_decay, chunk_states)
    y_off = jnp.einsum("bclhn,bchpn,bhcl->bclhp", cc, states[:, :-1], jnp.exp(a_cs))
    return (y_diag + y_off).reshape(bsz, L, H, P), states[:, -1]


def causal_conv(xbc, conv_st, w, b):
    L = xbc.shape[1]
    xpad = jnp.concatenate([conv_st.astype(xbc.dtype), xbc], axis=1)
    out = b + w[0] * xpad[:, 0:L]
    for k in range(1, CONV_K):
        out = out + w[k] * xpad[:, k:k + L]
    return jax.nn.silu(out), xpad[:, L:]


def ssd_mixer(z, xbc, dt_raw, conv_st, ssd_st, p):
    bsz, L, _ = z.shape
    f32 = jnp.float32
    xbc, new_conv = causal_conv(xbc, conv_st, p["conv_w"], p["conv_b"])
    xbc = xbc.astype(f32)
    xs, bm, cm = jnp.split(xbc, [SSD_W, SSD_W + SSD_GROUPS * SSD_STATE], axis=-1)
    xs = xs.reshape(bsz, L, SSD_HEADS, SSD_HEAD_DIM)
    rep = SSD_HEADS // SSD_GROUPS
    bm = jnp.repeat(bm.reshape(bsz, L, SSD_GROUPS, SSD_STATE), rep, axis=2)
    cm = jnp.repeat(cm.reshape(bsz, L, SSD_GROUPS, SSD_STATE), rep, axis=2)
    dt = jax.nn.softplus(dt_raw.astype(f32) + p["dt_bias"].astype(f32))
    a = -jnp.exp(p["a_log"].astype(f32))
    q = math.gcd(min(CHUNK, L), L)
    y, new_ssd = ssd_scan(xs * dt[..., None], dt * a, bm, cm, ssd_st.astype(f32), q)
    y = y + p["d_skip"].astype(f32)[:, None] * xs
    y = y.reshape(bsz, L, SSD_W) * jax.nn.silu(z.astype(f32))
    y = rmsnorm(y, p["ssd_norm"])
    return y, new_conv, new_ssd


def _complex_affine_combine(e1, e2):
    a1r, a1i, b1r, b1i = e1
    a2r, a2i, b2r, b2i = e2
    return (a1r * a2r - a1i * a2i, a1r * a2i + a1i * a2r,
            a2r * b1r - a2i * b1i + b2r, a2r * b1i + a2i * b1r + b2i)


def s5_mixer(u, h_re, h_im, p):
    bsz, L, _ = u.shape
    f32 = jnp.float32
    uf = u.astype(f32).reshape(bsz, L, S5_GROUPS, S5_GROUP_CH)
    a_re, a_im = p["s5_a_re"].astype(f32), p["s5_a_im"].astype(f32)
    dt = jnp.exp(p["s5_log_dt"].astype(f32))[:, None]
    mag = jnp.exp(a_re * dt)
    lam_re, lam_im = mag * jnp.cos(a_im * dt), mag * jnp.sin(a_im * dt)
    den = a_re * a_re + a_im * a_im
    f_re = ((lam_re - 1.0) * a_re + lam_im * a_im) / den
    f_im = (lam_im * a_re - (lam_re - 1.0) * a_im) / den
    b_re, b_im = p["s5_b_re"].astype(f32), p["s5_b_im"].astype(f32)
    bb_re = f_re[..., None] * b_re - f_im[..., None] * b_im
    bb_im = f_re[..., None] * b_im + f_im[..., None] * b_re
    bu_re = jnp.einsum("blgc,gnc->blgn", uf, bb_re)
    bu_im = jnp.einsum("blgc,gnc->blgn", uf, bb_im)
    h_re, h_im = h_re.astype(f32), h_im.astype(f32)
    bu_re = bu_re.at[:, 0].add(lam_re * h_re - lam_im * h_im)
    bu_im = bu_im.at[:, 0].add(lam_re * h_im + lam_im * h_re)
    lam_re_t = jnp.broadcast_to(lam_re, bu_re.shape)
    lam_im_t = jnp.broadcast_to(lam_im, bu_im.shape)
    _, _, s_re, s_im = lax.associative_scan(
        _complex_affine_combine, (lam_re_t, lam_im_t, bu_re, bu_im), axis=1)
    y = (jnp.einsum("blgn,gcn->blgc", s_re, p["s5_c_re"].astype(f32))
         - jnp.einsum("blgn,gcn->blgc", s_im, p["s5_c_im"].astype(f32))
         + p["s5_d"].astype(f32) * uf).reshape(bsz, L, S5_W)
    y = jax.nn.gelu(y)
    y = y * jax.nn.sigmoid(y @ p["s5_w_glu"].astype(f32))
    return y, s_re[:, -1], s_im[:, -1]


def memory_kv(mem, g, w_k, w_v):
    bsz, m, _ = mem.shape
    mn = rmsnorm(mem, g)
    k = (mn @ w_k).reshape(bsz, m, X_HEADS, X_HEAD_DIM)
    v = (mn @ w_v).reshape(bsz, m, X_HEADS, X_HEAD_DIM)
    return k, v


def cross_attn(x, mem_k, mem_v, w_q, w_o):
    bsz, L, _ = x.shape
    f32 = jnp.float32
    q = (x @ w_q).reshape(bsz, L, X_HEADS, X_HEAD_DIM)
    s = jnp.einsum("blhd,bmhd->bhlm", q.astype(f32), mem_k.astype(f32)) * (X_HEAD_DIM ** -0.5)
    pr = jax.nn.softmax(s, axis=-1)
    o = jnp.einsum("bhlm,bmhd->blhd", pr, mem_v.astype(f32)).reshape(bsz, L, D_MODEL)
    return o.astype(x.dtype) @ w_o


def hmoe(x, p):
    bsz, L, _ = x.shape
    f32 = jnp.float32
    t = x.reshape(bsz * L, D_MODEL)
    g_logits = (t @ p["w_router_group"]).astype(f32) + p["b_router_group"].astype(f32)
    g_idx = jnp.argmax(g_logits, axis=-1)
    g_onehot = jax.nn.one_hot(g_idx, N_EXPERT_GROUPS, dtype=f32)
    g_prob = jnp.sum(jax.nn.softmax(g_logits, axis=-1) * g_onehot, axis=-1)
    e_logits = (jnp.einsum("td,dge->tge", t, p["w_router_expert"]).astype(f32)
                + p["b_router_expert"].astype(f32))
    e_logits = jnp.einsum("tge,tg->te", e_logits, g_onehot)
    top_v, top_i = lax.top_k(e_logits, TOP_K_INNER)
    gates = jax.nn.softmax(top_v, axis=-1) * g_prob[:, None]
    expert_id = g_idx[:, None] * EXPERTS_PER_GROUP + top_i
    combine = jnp.einsum("tk,tke->te", gates,
                         jax.nn.one_hot(expert_id, N_EXPERTS, dtype=f32)).astype(x.dtype)
    out = jnp.zeros_like(t)
    for e in range(N_EXPERTS):
        hid = jax.nn.silu(t @ p["w_gate"][e]) * (t @ p["w_up"][e])
        out = out + (hid @ p["w_down"][e]) * combine[:, e:e + 1]
    return out.reshape(bsz, L, D_MODEL)


def block(h, conv_st, ssd_st, s5r_st, s5i_st, mem_k, mem_v, p):
    proj = rmsnorm(h, p["norm_mix"]) @ p["w_in"]
    z, xbc, dt_raw, u = jnp.split(proj, [SSD_W, SSD_W + CONV_CH, SSD_W + CONV_CH + SSD_HEADS], axis=-1)
    y_ssd, new_conv, new_ssd = ssd_mixer(z, xbc, dt_raw, conv_st, ssd_st, p)
    y_s5, new_re, new_im = s5_mixer(u, s5r_st, s5i_st, p)
    mix = jnp.concatenate([y_ssd.astype(h.dtype), y_s5.astype(h.dtype)], axis=-1)
    h = h + mix @ p["w_out"]
    h = h + cross_attn(rmsnorm(h, p["norm_x"]), mem_k, mem_v, p["w_q"], p["w_o"])
    h = h + hmoe(rmsnorm(h, p["norm_ffn"]), p)
    return h, (new_conv, new_ssd.astype(h.dtype), new_re.astype(h.dtype), new_im.astype(h.dtype))


def setup_inputs(seed: int = 0) -> dict:
    key = jax.random.key(seed)
    ks = iter(jax.random.split(key, 64))
    f32 = jnp.float32

    def nrm(shape, scale):
        return scale * jax.random.normal(next(ks), shape, f32)

    def gain(shape):
        return 1.0 + 0.02 * jax.random.normal(next(ks), shape, f32)

    def unif(shape, lo, hi):
        return jax.random.uniform(next(ks), shape, f32, lo, hi)

    dt0 = jnp.exp(unif((DEPTH, SSD_HEADS), math.log(1e-3), math.log(1e-1)))
    return {
        "x_prompt": nrm((BATCH, SEQ, D_MODEL), 1.0),
        "x_sample": nrm((DEC_BATCH, DEC_SEQ, D_MODEL), 1.0),
        "cache_conv": nrm((DEPTH, DEC_BATCH, CONV_K - 1, CONV_CH), 1.0),
        "state_ssd": nrm((DEPTH, DEC_BATCH, SSD_HEADS, SSD_HEAD_DIM, SSD_STATE), 0.1),
        "state_s5_re": nrm((DEPTH, DEC_BATCH, S5_GROUPS, S5_STATE), 0.1),
        "state_s5_im": nrm((DEPTH, DEC_BATCH, S5_GROUPS, S5_STATE), 0.1),
        "cache_mem_k": nrm((DEPTH, DEC_BATCH, MEM_LEN, X_HEADS, X_HEAD_DIM), 1.0),
        "cache_mem_v": nrm((DEPTH, DEC_BATCH, MEM_LEN, X_HEADS, X_HEAD_DIM), 1.0),
        "mem_prompt": nrm((BATCH, MEM_LEN, D_MODEL), 1.0),
        "norm_mix": gain((DEPTH, D_MODEL)),
        "w_in": nrm((DEPTH, D_MODEL, N_PROJ), D_MODEL ** -0.5),
        "conv_w": nrm((DEPTH, CONV_K, CONV_CH), CONV_K ** -0.5),
        "conv_b": nrm((DEPTH, CONV_CH), 0.02),
        "dt_bias": dt0 + jnp.log(-jnp.expm1(-dt0)),
        "a_log": jnp.log(unif((DEPTH, SSD_HEADS), 1.0, 16.0)),
        "d_skip": gain((DEPTH, SSD_HEADS)),
        "ssd_norm": gain((DEPTH, SSD_W)),
        "s5_a_re": -0.5 + nrm((DEPTH, S5_GROUPS, S5_STATE), 0.01),
        "s5_a_im": math.pi * jnp.arange(S5_STATE, dtype=f32) + nrm((DEPTH, S5_GROUPS, S5_STATE), 0.01),
        "s5_log_dt": unif((DEPTH, S5_GROUPS), math.log(1e-3), math.log(1e-1)),
        "s5_b_re": nrm((DEPTH, S5_GROUPS, S5_STATE, S5_GROUP_CH), (2 * S5_GROUP_CH) ** -0.5),
        "s5_b_im": nrm((DEPTH, S5_GROUPS, S5_STATE, S5_GROUP_CH), (2 * S5_GROUP_CH) ** -0.5),
        "s5_c_re": nrm((DEPTH, S5_GROUPS, S5_GROUP_CH, S5_STATE), S5_STATE ** -0.5),
        "s5_c_im": nrm((DEPTH, S5_GROUPS, S5_GROUP_CH, S5_STATE), S5_STATE ** -0.5),
        "s5_d": gain((DEPTH, S5_GROUPS, S5_GROUP_CH)),
        "s5_w_glu": nrm((DEPTH, S5_W, S5_W), S5_W ** -0.5),
        "w_out": nrm((DEPTH, MIX_W, D_MODEL), MIX_W ** -0.5),
        "norm_x": gain((DEPTH, D_MODEL)),
        "norm_mem": gain((DEPTH, D_MODEL)),
        "w_q": nrm((DEPTH, D_MODEL, D_MODEL), D_MODEL ** -0.5),
        "w_k": nrm((DEPTH, D_MODEL, D_MODEL), D_MODEL ** -0.5),
        "w_v": nrm((DEPTH, D_MODEL, D_MODEL), D_MODEL ** -0.5),
        "w_o": nrm((DEPTH, D_MODEL, D_MODEL), D_MODEL ** -0.5),
        "norm_ffn": gain((DEPTH, D_MODEL)),
        "w_router_group": nrm((DEPTH, D_MODEL, N_EXPERT_GROUPS), D_MODEL ** -0.5),
        "b_router_group": nrm((DEPTH, N_EXPERT_GROUPS), 0.01),
        "w_router_expert": nrm((DEPTH, D_MODEL, N_EXPERT_GROUPS, EXPERTS_PER_GROUP), D_MODEL ** -0.5),
        "b_router_expert": nrm((DEPTH, N_EXPERT_GROUPS, EXPERTS_PER_GROUP), 0.01),
        "w_gate": nrm((DEPTH, N_EXPERTS, D_MODEL, EXPERT_FF), D_MODEL ** -0.5),
        "w_up": nrm((DEPTH, N_EXPERTS, D_MODEL, EXPERT_FF), D_MODEL ** -0.5),
        "w_down": nrm((DEPTH, N_EXPERTS, EXPERT_FF, D_MODEL), EXPERT_FF ** -0.5),
        "norm_final": gain((D_MODEL,)),
    }


def reference(x_prompt, x_sample, cache_conv, state_ssd, state_s5_re, state_s5_im,
              cache_mem_k, cache_mem_v, mem_prompt,
              norm_mix, w_in, conv_w, conv_b, dt_bias, a_log, d_skip, ssd_norm,
              s5_a_re, s5_a_im, s5_log_dt, s5_b_re, s5_b_im, s5_c_re, s5_c_im, s5_d, s5_w_glu,
              w_out, norm_x, norm_mem, w_q, w_k, w_v, w_o, norm_ffn,
              w_router_group, b_router_group, w_router_expert, b_router_expert,
              w_gate, w_up, w_down, norm_final):
    bp = x_prompt.shape[0]
    hp, hs = x_prompt, x_sample
    new_p, new_s = [], []
    for l in range(DEPTH):
        p = {
            "norm_mix": norm_mix[l], "w_in": w_in[l], "conv_w": conv_w[l], "conv_b": conv_b[l],
            "dt_bias": dt_bias[l], "a_log": a_log[l], "d_skip": d_skip[l], "ssd_norm": ssd_norm[l],
            "s5_a_re": s5_a_re[l], "s5_a_im": s5_a_im[l], "s5_log_dt": s5_log_dt[l],
            "s5_b_re": s5_b_re[l], "s5_b_im": s5_b_im[l], "s5_c_re": s5_c_re[l], "s5_c_im": s5_c_im[l],
            "s5_d": s5_d[l], "s5_w_glu": s5_w_glu[l], "w_out": w_out[l],
            "norm_x": norm_x[l], "w_q": w_q[l], "w_o": w_o[l], "norm_ffn": norm_ffn[l],
            "w_router_group": w_router_group[l], "b_router_group": b_router_group[l],
            "w_router_expert": w_router_expert[l], "b_router_expert": b_router_expert[l],
            "w_gate": w_gate[l], "w_up": w_up[l], "w_down": w_down[l],
        }
        mk_p, mv_p = memory_kv(mem_prompt, norm_mem[l], w_k[l], w_v[l])
        zero_conv = jnp.zeros((bp, CONV_K - 1, CONV_CH), hp.dtype)
        zero_ssd = jnp.zeros((bp, SSD_HEADS, SSD_HEAD_DIM, SSD_STATE), jnp.float32)
        zero_s5 = jnp.zeros((bp, S5_GROUPS, S5_STATE), jnp.float32)
        hp, st_p = block(hp, zero_conv, zero_ssd, zero_s5, zero_s5, mk_p, mv_p, p)
        hs, st_s = block(hs, cache_conv[l], state_ssd[l], state_s5_re[l], state_s5_im[l],
                         cache_mem_k[l], cache_mem_v[l], p)
        new_p.append(st_p + (mk_p, mv_p))
        new_s.append(st_s)
    y_prompt = rmsnorm(hp, norm_final)
    y_sample = rmsnorm(hs, norm_final)
    conv_p, ssd_p, s5_re_p, s5_im_p, mem_k_p, mem_v_p = [jnp.stack(a) for a in zip(*new_p)]
    conv_s, ssd_s, s5_re_s, s5_im_s = [jnp.stack(a) for a in zip(*new_s)]
    return (y_prompt, y_sample, conv_p, ssd_p, s5_re_p, s5_im_p, mem_k_p, mem_v_p,
            conv_s, ssd_s, s5_re_s, s5_im_s)
```

```python
import functools
import math

import jax
import jax.numpy as jnp
from jax import lax
from jax.experimental import pallas as pl
from jax.experimental.pallas import tpu as pltpu

F32 = jnp.float32
BF16 = jnp.bfloat16
EPS = 1e-6

D_MODEL = 1024
CHUNK = 64
SEQ_TILE = 8
SSD_W = 512
SSD_HEAD_DIM = 64
SSD_HEADS = 8
SSD_GROUPS = 2
SSD_STATE = 128
CONV_K = 4
CONV_CH = 1024
S5_W = 512
S5_GROUPS = 32
S5_GROUP_CH = 16
S5_STATE = 64
S5_LANES = S5_GROUPS * S5_STATE
MEM_LEN = 256
X_HEADS = 4
X_HEAD_DIM = 256
N_EXPERT_GROUPS = 4
EXPERTS_PER_GROUP = 8
N_EXPERTS = 32
EXPERT_FF = 256
LANE = 128
PAD_ROWS = 8
VMEM_LIMIT = 56 * 1024 * 1024


def _rms(x, g):
    return x * lax.rsqrt(jnp.mean(x * x, axis=-1, keepdims=True) + EPS) * g


def _dot(a, b):
    return jnp.dot(a, b, preferred_element_type=F32)


def _dot_nt(a, b):
    return lax.dot_general(a, b, (((1,), (1,)), ((), ())), preferred_element_type=F32)


def _dot_tn(a, b):
    return lax.dot_general(a, b, (((0,), (0,)), ((), ())), preferred_element_type=F32)


def _const_spec(shape):
    nd = len(shape)
    return pl.BlockSpec(shape, lambda *_: (0,) * nd)


def _memkv_kernel(m_ref, g_ref, wk_ref, wv_ref, k_ref, v_ref):
    mn = _rms(m_ref[...], g_ref[...]).astype(BF16)
    k_ref[...] = _dot(mn, wk_ref[...])
    v_ref[...] = _dot(mn, wv_ref[...])


def _memkv(mem2d, g, wk, wv):
    rows = mem2d.shape[0]
    tile = 512
    return pl.pallas_call(
        _memkv_kernel,
        grid=(rows // tile,),
        in_specs=[pl.BlockSpec((tile, D_MODEL), lambda i: (i, 0)),
                  _const_spec((1, D_MODEL)), _const_spec((D_MODEL, D_MODEL)), _const_spec((D_MODEL, D_MODEL))],
        out_specs=[pl.BlockSpec((tile, D_MODEL), lambda i: (i, 0))] * 2,
        out_shape=[jax.ShapeDtypeStruct((rows, D_MODEL), F32)] * 2,
        compiler_params=pltpu.CompilerParams(dimension_semantics=("arbitrary",), vmem_limit_bytes=VMEM_LIMIT),
        name="memkv",
    )(mem2d, g, wk, wv)


def _softplus(x):
    return jnp.maximum(x, 0.0) + jnp.log1p(jnp.exp(-jnp.abs(x)))


def _mixer_kernel(x_ref, conv0_ref, ssd0_ref, s5re0_ref, s5im0_ref,
                  gmix_ref, wz_ref, wxbc_ref, wdt_ref, wu_ref, convw_ref, convb_ref,
                  dtb_ref, aneg_ref, dskip_ref, ssdn_ref,
                  lamre_ref, lamim_ref, wb_ref, wc_ref, s5d_ref, wglu_ref, wout_ref,
                  h_ref, conv_ref, ssd_ref, s5re_ref, s5im_ref,
                  xn_ref, xpad_ref, xc_ref, dt_ref, cs_ref, cst_ref, y_ref, u_ref,
                  bure_ref, buim_ref, mix_ref):
    c = pl.program_id(1)
    rows = SEQ_TILE * CHUNK

    @pl.when(c == 0)
    def _():
        xpad_ref[:, PAD_ROWS - (CONV_K - 1):PAD_ROWS, :] = conv0_ref[...]
        ssd_ref[...] = ssd0_ref[...]
        s5re_ref[...] = s5re0_ref[...]
        s5im_ref[...] = s5im0_ref[...]

    x = x_ref[...].reshape(rows, D_MODEL)
    xn_ref[...] = _rms(x, gmix_ref[...]).astype(BF16)

    xpad_ref[:, PAD_ROWS:, :] = _dot(xn_ref[...], wxbc_ref[...]).reshape(SEQ_TILE, CHUNK, CONV_CH)
    acc = convb_ref[...].reshape(1, 1, CONV_CH)
    for k in range(CONV_K):
        lo = PAD_ROWS - (CONV_K - 1) + k
        acc = acc + convw_ref[k:k + 1, :].reshape(1, 1, CONV_CH) * xpad_ref[:, lo:lo + CHUNK, :]
    xc_ref[...] = (acc * jax.nn.sigmoid(acc)).reshape(rows, CONV_CH)
    hist = xpad_ref[:, PAD_ROWS + CHUNK - (CONV_K - 1):, :]
    conv_ref[...] = hist
    xpad_ref[:, PAD_ROWS - (CONV_K - 1):PAD_ROWS, :] = hist

    dt = _softplus(_dot(xn_ref[...], wdt_ref[...]) + dtb_ref[...])
    dt_ref[...] = dt
    a = dt * aneg_ref[...]
    tpos = lax.broadcasted_iota(jnp.int32, (rows, LANE), 0) % CHUNK
    sh = 1
    while sh < CHUNK:
        a = a + jnp.where(tpos >= sh, pltpu.roll(a, sh, axis=0), 0.0)
        sh *= 2
    cs_ref[...] = a
    for b in range(SEQ_TILE):
        cst_ref[b] = a[b * CHUNK:(b + 1) * CHUNK, :].T

    tri = (lax.broadcasted_iota(jnp.int32, (CHUNK, CHUNK), 0)
           >= lax.broadcasted_iota(jnp.int32, (CHUNK, CHUNK), 1))
    heads_per_group = SSD_HEADS // SSD_GROUPS

    def seq_body(b, carry):
        r0 = pl.multiple_of(b * CHUNK, CHUNK)
        rs = pl.ds(r0, CHUNK)
        cs_blk = cs_ref[rs, :]
        cs_last = cs_ref[pl.ds(r0 + CHUNK - 1, 1), :]
        dt_blk = dt_ref[rs, :]
        for g in range(SSD_GROUPS):
            b_blk = xc_ref[rs, pl.ds(SSD_W + g * SSD_STATE, SSD_STATE)]
            c_blk = xc_ref[rs, pl.ds(SSD_W + SSD_GROUPS * SSD_STATE + g * SSD_STATE, SSD_STATE)]
            b_bf = b_blk.astype(BF16)
            c_bf = c_blk.astype(BF16)
            gram = _dot_nt(c_bf, b_bf)
            for hh in range(heads_per_group):
                h = g * heads_per_group + hh
                cs_col = cs_blk[:, h:h + 1]
                cs_row = cst_ref[b, pl.ds(h, 1), :]
                decay = jnp.exp(jnp.where(tri, cs_col - cs_row, -jnp.inf))
                xs_h = xc_ref[rs, pl.ds(h * SSD_HEAD_DIM, SSD_HEAD_DIM)]
                xdt = xs_h * dt_blk[:, h:h + 1]
                y = _dot((gram * decay).astype(BF16), xdt.astype(BF16))
                st = ssd_ref[b, h]
                y = y + _dot_nt(c_bf, st.astype(BF16)) * jnp.exp(cs_col)
                y = y + dskip_ref[:, h:h + 1] * xs_h
                y_ref[rs, pl.ds(h * SSD_HEAD_DIM, SSD_HEAD_DIM)] = y
                to_end = jnp.exp(cs_last[:, h:h + 1] - cs_col)
                upd = _dot_tn((xdt * to_end).astype(BF16), b_bf)
                ssd_ref[b, h] = st * jnp.exp(cs_last[:, h:h + 1]) + upd
        return carry

    lax.fori_loop(0, SEQ_TILE, seq_body, 0)

    z = _dot(xn_ref[...], wz_ref[...])
    y = y_ref[...] * (z * jax.nn.sigmoid(z))
    mix_ref[:, 0:SSD_W] = _rms(y, ssdn_ref[...]).astype(BF16)

    u = _dot(xn_ref[...], wu_ref[...])
    u_ref[...] = u
    half_ch = S5_W // 2
    half_st = S5_LANES // 2
    half_tiles = half_st // LANE
    for hf in range(2):
        bu = _dot(u[:, hf * half_ch:(hf + 1) * half_ch].astype(BF16), wb_ref[hf])
        for k in range(half_tiles):
            bure_ref[hf * half_tiles + k] = bu[:, k * LANE:(k + 1) * LANE]
            buim_ref[hf * half_tiles + k] = bu[:, half_st + k * LANE:half_st + (k + 1) * LANE]
    scan_tiles = 4
    for j in range(S5_LANES // LANE // scan_tiles):
        tiles = [j * scan_tiles + k for k in range(scan_tiles)]
        lr = [lamre_ref[:, pl.ds(k * LANE, LANE)] for k in tiles]
        li = [lamim_ref[:, pl.ds(k * LANE, LANE)] for k in tiles]
        sr = [s5re_ref[:, pl.ds(k * LANE, LANE)] for k in tiles]
        si = [s5im_ref[:, pl.ds(k * LANE, LANE)] for k in tiles]
        for t in range(CHUNK):
            ts = pl.ds(t, SEQ_TILE, stride=CHUNK)
            for q, k in enumerate(tiles):
                nr = lr[q] * sr[q] - li[q] * si[q] + bure_ref[k, ts, :]
                ni = lr[q] * si[q] + li[q] * sr[q] + buim_ref[k, ts, :]
                sr[q], si[q] = nr, ni
                bure_ref[k, ts, :] = nr
                buim_ref[k, ts, :] = ni
        for q, k in enumerate(tiles):
            s5re_ref[:, pl.ds(k * LANE, LANE)] = sr[q]
            s5im_ref[:, pl.ds(k * LANE, LANE)] = si[q]
    ys = []
    for j in range(4):
        s_re = jnp.concatenate([bure_ref[4 * j + k] for k in range(4)], axis=1)
        s_im = jnp.concatenate([buim_ref[4 * j + k] for k in range(4)], axis=1)
        ys.append(_dot(s_re.astype(BF16), wc_ref[j, 0]) + _dot(s_im.astype(BF16), wc_ref[j, 1]))
    y5 = jnp.concatenate(ys, axis=1) + s5d_ref[...] * u_ref[...]
    y5 = jax.nn.gelu(y5)
    y5 = y5 * jax.nn.sigmoid(_dot(y5.astype(BF16), wglu_ref[...]))
    mix_ref[:, SSD_W:] = y5.astype(BF16)

    h = x_ref[...].reshape(rows, D_MODEL) + _dot(mix_ref[...], wout_ref[...])
    h_ref[...] = h.reshape(SEQ_TILE, CHUNK, D_MODEL)


def _mixer(x, conv0, ssd0, s5re0, s5im0, w):
    nb, seq, _ = x.shape
    grid = (nb // SEQ_TILE, seq // CHUNK)
    rows = SEQ_TILE * CHUNK
    weights = [w["gmix"], w["wz"], w["wxbc"], w["wdt"], w["wu"], w["convw"], w["convb"],
               w["dtb"], w["aneg"], w["dskip"], w["ssdn"],
               w["lamre"], w["lamim"], w["wb"], w["wc"], w["s5d"], w["wglu"], w["wout"]]
    state_specs = [pl.BlockSpec((SEQ_TILE, CONV_K - 1, CONV_CH), lambda i, c: (i, 0, 0)),
                   pl.BlockSpec((SEQ_TILE, SSD_HEADS, SSD_HEAD_DIM, SSD_STATE), lambda i, c: (i, 0, 0, 0)),
                   pl.BlockSpec((SEQ_TILE, S5_LANES), lambda i, c: (i, 0)),
                   pl.BlockSpec((SEQ_TILE, S5_LANES), lambda i, c: (i, 0))]
    x_spec = pl.BlockSpec((SEQ_TILE, CHUNK, D_MODEL), lambda i, c: (i, c, 0))
    return pl.pallas_call(
        _mixer_kernel,
        grid=grid,
        in_specs=[x_spec] + state_specs + [_const_spec(a.shape) for a in weights],
        out_specs=[x_spec] + state_specs,
        out_shape=[jax.ShapeDtypeStruct(x.shape, F32),
                   jax.ShapeDtypeStruct(conv0.shape, F32), jax.ShapeDtypeStruct(ssd0.shape, F32),
                   jax.ShapeDtypeStruct(s5re0.shape, F32), jax.ShapeDtypeStruct(s5im0.shape, F32)],
        scratch_shapes=[
            pltpu.VMEM((rows, D_MODEL), BF16),
            pltpu.VMEM((SEQ_TILE, PAD_ROWS + CHUNK, CONV_CH), F32),
            pltpu.VMEM((rows, CONV_CH), F32),
            pltpu.VMEM((rows, LANE), F32),
            pltpu.VMEM((rows, LANE), F32),
            pltpu.VMEM((SEQ_TILE, LANE, CHUNK), F32),
            pltpu.VMEM((rows, SSD_W), F32),
            pltpu.VMEM((rows, S5_W), F32),
            pltpu.VMEM((S5_LANES // LANE, rows, LANE), F32),
            pltpu.VMEM((S5_LANES // LANE, rows, LANE), F32),
            pltpu.VMEM((rows, D_MODEL), BF16),
        ],
        compiler_params=pltpu.CompilerParams(dimension_semantics=("arbitrary", "arbitrary"),
                                             vmem_limit_bytes=VMEM_LIMIT),
        name="mixer",
    )(x, conv0, ssd0, s5re0, s5im0, *weights)


def _attn_kernel(h_ref, k_ref, v_ref, gx_ref, wq_ref, wo_ref, gffn_ref, wr_ref, br_ref,
                 h2_ref, tn_ref, comb_ref, o_ref):
    h1 = h_ref[0]
    xn = _rms(h1, gx_ref[...]).astype(BF16)
    q = _dot(xn, wq_ref[...])
    scale = X_HEAD_DIM ** -0.5
    for hd in range(X_HEADS):
        ls = pl.ds(hd * X_HEAD_DIM, X_HEAD_DIM)
        kh = k_ref[0, :, ls].astype(BF16)
        vh = v_ref[0, :, ls].astype(BF16)
        s = _dot_nt(q[:, hd * X_HEAD_DIM:(hd + 1) * X_HEAD_DIM].astype(BF16), kh) * scale
        s = s - jnp.max(s, axis=-1, keepdims=True)
        p = jnp.exp(s)
        p = p / jnp.sum(p, axis=-1, keepdims=True)
        o_ref[:, ls] = _dot(p.astype(BF16), vh).astype(BF16)
    h2 = h1 + _dot(o_ref[...], wo_ref[...])
    h2_ref[0] = h2

    tn = _rms(h2, gffn_ref[...]).astype(BF16)
    tn_ref[0] = tn
    logits = _dot(tn, wr_ref[...]) + br_ref[...]
    lane = lax.broadcasted_iota(jnp.int32, logits.shape, 1)
    big = jnp.int32(2 ** 30)
    neg = -jnp.inf
    is_g = (lane >= N_EXPERTS) & (lane < N_EXPERTS + N_EXPERT_GROUPS)
    gl = jnp.where(is_g, logits, neg)
    gmax = jnp.max(gl, axis=-1, keepdims=True)
    g_idx = jnp.min(jnp.where(gl == gmax, lane - N_EXPERTS, big), axis=-1, keepdims=True)
    g_prob = 1.0 / jnp.sum(jnp.exp(gl - gmax), axis=-1, keepdims=True)
    in_grp = (lane < N_EXPERTS) & ((lane // EXPERTS_PER_GROUP) == g_idx)
    el = jnp.where(in_grp, logits, neg)
    m1 = jnp.max(el, axis=-1, keepdims=True)
    i1 = jnp.min(jnp.where(el == m1, lane, big), axis=-1, keepdims=True)
    el2 = jnp.where(lane == i1, neg, el)
    m2 = jnp.max(el2, axis=-1, keepdims=True)
    i2 = jnp.min(jnp.where(el2 == m2, lane, big), axis=-1, keepdims=True)
    e2 = jnp.exp(m2 - m1)
    den = 1.0 + e2
    comb_ref[0] = (jnp.where(lane == i1, (1.0 / den) * g_prob, 0.0)
                   + jnp.where(lane == i2, (e2 / den) * g_prob, 0.0))


def _attn(h1, mem_k, mem_v, w, tile):
    nb, seq, _ = h1.shape
    row_spec = pl.BlockSpec((1, tile, D_MODEL), lambda b, i: (b, i, 0))
    kv_spec = pl.BlockSpec((1, MEM_LEN, D_MODEL), lambda b, i: (b, 0, 0))
    weights = [w["gx"], w["wq"], w["wo"], w["gffn"], w["wr"], w["br"]]
    return pl.pallas_call(
        _attn_kernel,
        grid=(nb, seq // tile),
        in_specs=[row_spec, kv_spec, kv_spec] + [_const_spec(a.shape) for a in weights],
        out_specs=[row_spec, row_spec, pl.BlockSpec((1, tile, LANE), lambda b, i: (b, i, 0))],
        out_shape=[jax.ShapeDtypeStruct(h1.shape, F32), jax.ShapeDtypeStruct(h1.shape, BF16),
                   jax.ShapeDtypeStruct((nb, seq, LANE), F32)],
        scratch_shapes=[pltpu.VMEM((tile, D_MODEL), BF16)],
        compiler_params=pltpu.CompilerParams(dimension_semantics=("arbitrary", "arbitrary"),
                                             vmem_limit_bytes=VMEM_LIMIT),
        name="attn_router",
    )(h1, mem_k, mem_v, *weights)


def _moe_kernel(tn_ref, comb_ref, h2_ref, wg_ref, wu_ref, wd_ref, gfin_ref, out_ref, acc_ref):
    e = pl.program_id(1)

    @pl.when(e == 0)
    def _():
        acc_ref[...] = jnp.zeros_like(acc_ref)

    t = tn_ref[...]
    gate = _dot(t, wg_ref[0])
    hid = (gate * jax.nn.sigmoid(gate)) * _dot(t, wu_ref[0])
    comb = comb_ref[...]
    lane = lax.broadcasted_iota(jnp.int32, comb.shape, 1)
    ce = jnp.sum(jnp.where(lane == e, comb, 0.0), axis=-1, keepdims=True)
    acc_ref[...] += _dot(hid.astype(BF16), wd_ref[0]) * ce

    @pl.when(e == N_EXPERTS - 1)
    def _():
        out_ref[...] = _rms(h2_ref[...] + acc_ref[...], gfin_ref[...])


def _moe(tn, comb, h2, w, tile):
    rows = tn.shape[0]
    row_spec = pl.BlockSpec((tile, D_MODEL), lambda i, e: (i, 0))
    return pl.pallas_call(
        _moe_kernel,
        grid=(rows // tile, N_EXPERTS),
        in_specs=[row_spec, pl.BlockSpec((tile, LANE), lambda i, e: (i, 0)), row_spec,
                  pl.BlockSpec((1, D_MODEL, EXPERT_FF), lambda i, e: (e, 0, 0)),
                  pl.BlockSpec((1, D_MODEL, EXPERT_FF), lambda i, e: (e, 0, 0)),
                  pl.BlockSpec((1, EXPERT_FF, D_MODEL), lambda i, e: (e, 0, 0)),
                  _const_spec((1, D_MODEL))],
        out_specs=row_spec,
        out_shape=jax.ShapeDtypeStruct((rows, D_MODEL), F32),
        scratch_shapes=[pltpu.VMEM((tile, D_MODEL), F32)],
        compiler_params=pltpu.CompilerParams(dimension_semantics=("arbitrary", "arbitrary"),
                                             vmem_limit_bytes=VMEM_LIMIT),
        name="moe",
    )(tn, comb, h2, w["wgate"], w["wup"], w["wdown"], w["gfin"])


def _row(v, width=None):
    v = v.astype(F32).reshape(1, -1)
    if width is not None and v.shape[1] < width:
        v = jnp.pad(v, ((0, 0), (0, width - v.shape[1])))
    return v


def _s5_params(a_re, a_im, log_dt, b_re, b_im, c_re, c_im):
    dt = jnp.exp(log_dt)[:, None]
    mag = jnp.exp(a_re * dt)
    lam_re, lam_im = mag * jnp.cos(a_im * dt), mag * jnp.sin(a_im * dt)
    den = a_re * a_re + a_im * a_im
    f_re = ((lam_re - 1.0) * a_re + lam_im * a_im) / den
    f_im = (lam_im * a_re - (lam_re - 1.0) * a_im) / den
    bb_re = f_re[..., None] * b_re - f_im[..., None] * b_im
    bb_im = f_re[..., None] * b_im + f_im[..., None] * b_re
    half = S5_GROUPS // 2

    def in_proj(bb):
        bb = bb.reshape(2, half, S5_STATE, S5_GROUP_CH)
        eye = jnp.eye(half, dtype=F32)
        m = jnp.einsum("hgnc,gk->hgckn", bb, eye)
        return m.reshape(2, half * S5_GROUP_CH, half * S5_STATE)

    wb = jnp.concatenate([in_proj(bb_re), in_proj(bb_im)], axis=2).astype(BF16)

    def out_proj(cc):
        q = S5_GROUPS // 4
        cc = cc.reshape(4, q, S5_GROUP_CH, S5_STATE)
        eye = jnp.eye(q, dtype=F32)
        m = jnp.einsum("qgcn,gk->qgnkc", cc, eye)
        return m.reshape(4, q * S5_STATE, q * S5_GROUP_CH)

    wc = jnp.stack([out_proj(c_re), -out_proj(c_im)], axis=1).astype(BF16)
    return lam_re.reshape(1, S5_LANES), lam_im.reshape(1, S5_LANES), wb, wc


def kernel(x_prompt, x_sample, cache_conv, state_ssd, state_s5_re, state_s5_im, cache_mem_k, cache_mem_v,
           mem_prompt, norm_mix, w_in, conv_w, conv_b, dt_bias, a_log, d_skip, ssd_norm, s5_a_re, s5_a_im,
           s5_log_dt, s5_b_re, s5_b_im, s5_c_re, s5_c_im, s5_d, s5_w_glu, w_out, norm_x, norm_mem, w_q,
           w_k, w_v, w_o, norm_ffn, w_router_group, b_router_group, w_router_expert, b_router_expert,
           w_gate, w_up, w_down, norm_final):
    depth = norm_mix.shape[0]
    assert depth == 1
    l = 0
    bp, seq_p, _ = x_prompt.shape
    bs, seq_s, _ = x_sample.shape

    o1 = SSD_W
    o2 = o1 + CONV_CH
    o3 = o2 + SSD_HEADS
    lam_re, lam_im, wb, wc = _s5_params(s5_a_re[l], s5_a_im[l], s5_log_dt[l], s5_b_re[l], s5_b_im[l],
                                        s5_c_re[l], s5_c_im[l])
    wm = {
        "gmix": _row(norm_mix[l]),
        "wz": w_in[l][:, :o1].astype(BF16),
        "wxbc": w_in[l][:, o1:o2].astype(BF16),
        "wdt": jnp.pad(w_in[l][:, o2:o3], ((0, 0), (0, LANE - SSD_HEADS))).astype(BF16),
        "wu": w_in[l][:, o3:].astype(BF16),
        "convw": conv_w[l].astype(F32), "convb": _row(conv_b[l]),
        "dtb": _row(dt_bias[l], LANE), "aneg": _row(-jnp.exp(a_log[l].astype(F32)), LANE),
        "dskip": _row(d_skip[l], LANE), "ssdn": _row(ssd_norm[l]),
        "lamre": lam_re, "lamim": lam_im, "wb": wb, "wc": wc,
        "s5d": _row(s5_d[l]), "wglu": s5_w_glu[l].astype(BF16), "wout": w_out[l].astype(BF16),
    }
    wr = jnp.concatenate([w_router_expert[l].reshape(D_MODEL, N_EXPERTS), w_router_group[l]], axis=1)
    br = jnp.concatenate([b_router_expert[l].reshape(N_EXPERTS), b_router_group[l]])
    wa = {
        "gx": _row(norm_x[l]), "wq": w_q[l].astype(BF16), "wo": w_o[l].astype(BF16),
        "gffn": _row(norm_ffn[l]),
        "wr": jnp.pad(wr, ((0, 0), (0, LANE - wr.shape[1]))).astype(BF16), "br": _row(br, LANE),
    }
    we = {"wgate": w_gate[l].astype(BF16), "wup": w_up[l].astype(BF16), "wdown": w_down[l].astype(BF16),
          "gfin": _row(norm_final)}

    mk_p, mv_p = _memkv(mem_prompt.reshape(bp * MEM_LEN, D_MODEL), _row(norm_mem[l]),
                        w_k[l].astype(BF16), w_v[l].astype(BF16))
    mk_p = mk_p.reshape(bp, MEM_LEN, D_MODEL)
    mv_p = mv_p.reshape(bp, MEM_LEN, D_MODEL)

    def group(x, conv0, ssd0, s5re0, s5im0, mem_k, mem_v, attn_tile, moe_tile):
        nb, seq, _ = x.shape
        h1, conv, ssd, s5re, s5im = _mixer(x, conv0, ssd0, s5re0.reshape(nb, S5_LANES),
                                           s5im0.reshape(nb, S5_LANES), wm)
        h2, tn, comb = _attn(h1, mem_k, mem_v, wa, attn_tile)
        y = _moe(tn.reshape(nb * seq, D_MODEL), comb.reshape(nb * seq, LANE),
                 h2.reshape(nb * seq, D_MODEL), we, moe_tile)
        return (y.reshape(nb, seq, D_MODEL), conv[None], ssd[None],
                s5re.reshape(1, nb, S5_GROUPS, S5_STATE), s5im.reshape(1, nb, S5_GROUPS, S5_STATE))

    zeros = lambda *s: jnp.zeros(s, F32)
    y_p, conv_p, ssd_p, s5re_p, s5im_p = group(
        x_prompt, zeros(bp, CONV_K - 1, CONV_CH), zeros(bp, SSD_HEADS, SSD_HEAD_DIM, SSD_STATE),
        zeros(bp, S5_GROUPS, S5_STATE), zeros(bp, S5_GROUPS, S5_STATE), mk_p, mv_p,
        min(512, seq_p), min(1024, bp * seq_p))
    y_s, conv_s, ssd_s, s5re_s, s5im_s = group(
        x_sample, cache_conv[l], state_ssd[l], state_s5_re[l], state_s5_im[l],
        cache_mem_k[l].reshape(bs, MEM_LEN, D_MODEL), cache_mem_v[l].reshape(bs, MEM_LEN, D_MODEL), seq_s, min(1024, bs * seq_s))

    shape_kv = (1, bp, MEM_LEN, X_HEADS, X_HEAD_DIM)
    return (y_p, y_s, conv_p, ssd_p, s5re_p, s5im_p, mk_p.reshape(shape_kv), mv_p.reshape(shape_kv),
            conv_s, ssd_s, s5re_s, s5im_s)
```

```python
import functools
import math

import jax
import jax.numpy as jnp
from jax import lax
from jax.experimental import pallas as pl
from jax.experimental.pallas import tpu as pltpu

F32 = jnp.float32
BF16 = jnp.bfloat16
EPS = 1e-6

D_MODEL = 1024
CHUNK = 64
SEQ_TILE = 8
SSD_W = 512
SSD_HEAD_DIM = 64
SSD_HEADS = 8
SSD_GROUPS = 2
SSD_STATE = 128
CONV_K = 4
CONV_CH = 1024
S5_W = 512
S5_GROUPS = 32
S5_GROUP_CH = 16
S5_STATE = 64
S5_LANES = S5_GROUPS * S5_STATE
MEM_LEN = 256
X_HEADS = 4
X_HEAD_DIM = 256
N_EXPERT_GROUPS = 4
EXPERTS_PER_GROUP = 8
N_EXPERTS = 32
EXPERT_FF = 256
LANE = 128
PAD_ROWS = 8
S5_PITCH = CHUNK + PAD_ROWS
MOE_TILE = 256
COMBINE_TILE = 256
VMEM_LIMIT = 56 * 1024 * 1024


def _rms(x, g):
    return x * lax.rsqrt(jnp.mean(x * x, axis=-1, keepdims=True) + EPS) * g


def _dot(a, b):
    return jnp.dot(a, b, preferred_element_type=F32)


def _dot_nt(a, b):
    return lax.dot_general(a, b, (((1,), (1,)), ((), ())), preferred_element_type=F32)


def _dot_tn(a, b):
    return lax.dot_general(a, b, (((0,), (0,)), ((), ())), preferred_element_type=F32)


def _const_spec(shape):
    nd = len(shape)
    return pl.BlockSpec(shape, lambda *_: (0,) * nd)


def _memkv_kernel(m_ref, g_ref, wk_ref, wv_ref, k_ref, v_ref):
    mn = _rms(m_ref[...], g_ref[...]).astype(BF16)
    k_ref[...] = _dot(mn, wk_ref[...])
    v_ref[...] = _dot(mn, wv_ref[...])


def _memkv(mem2d, g, wk, wv):
    rows = mem2d.shape[0]
    tile = 512
    return pl.pallas_call(
        _memkv_kernel,
        grid=(rows // tile,),
        in_specs=[pl.BlockSpec((tile, D_MODEL), lambda i: (i, 0)),
                  _const_spec((1, D_MODEL)), _const_spec((D_MODEL, D_MODEL)), _const_spec((D_MODEL, D_MODEL))],
        out_specs=[pl.BlockSpec((tile, D_MODEL), lambda i: (i, 0))] * 2,
        out_shape=[jax.ShapeDtypeStruct((rows, D_MODEL), F32)] * 2,
        compiler_params=pltpu.CompilerParams(dimension_semantics=("arbitrary",), vmem_limit_bytes=VMEM_LIMIT),
        name="memkv",
    )(mem2d, g, wk, wv)


def _softplus(x):
    return jnp.maximum(x, 0.0) + jnp.log1p(jnp.exp(-jnp.abs(x)))


def _mixer_kernel(x_ref, conv0_ref, ssd0_ref, s5re0_ref, s5im0_ref,
                  gmix_ref, wz_ref, wxbc_ref, wdt_ref, wu_ref, convw_ref, convb_ref,
                  dtb_ref, aneg_ref, dskip_ref, ssdn_ref,
                  lamre_ref, lamim_ref, wb_ref, wc_ref, s5d_ref, wglu_ref, wout_ref,
                  h_ref, conv_ref, ssd_ref, s5re_ref, s5im_ref,
                  xn_ref, xpad_ref, xc_ref, dt_ref, cs_ref, cst_ref, y_ref, u_ref,
                  bure_ref, buim_ref, mix_ref):
    c = pl.program_id(1)
    rows = SEQ_TILE * CHUNK

    @pl.when(c == 0)
    def _():
        xpad_ref[:, PAD_ROWS - (CONV_K - 1):PAD_ROWS, :] = conv0_ref[...]
        ssd_ref[...] = ssd0_ref[...]
        s5re_ref[...] = s5re0_ref[...]
        s5im_ref[...] = s5im0_ref[...]

    x = x_ref[...].reshape(rows, D_MODEL)
    xn_ref[...] = _rms(x, gmix_ref[...]).astype(BF16)

    xpad_ref[:, PAD_ROWS:, :] = _dot(xn_ref[...], wxbc_ref[...]).reshape(SEQ_TILE, CHUNK, CONV_CH)
    acc = convb_ref[...].reshape(1, 1, CONV_CH)
    for k in range(CONV_K):
        lo = PAD_ROWS - (CONV_K - 1) + k
        acc = acc + convw_ref[k:k + 1, :].reshape(1, 1, CONV_CH) * xpad_ref[:, lo:lo + CHUNK, :]
    xc_ref[...] = (acc * jax.nn.sigmoid(acc)).reshape(rows, CONV_CH)
    hist = xpad_ref[:, PAD_ROWS + CHUNK - (CONV_K - 1):, :]
    conv_ref[...] = hist
    xpad_ref[:, PAD_ROWS - (CONV_K - 1):PAD_ROWS, :] = hist

    dt = _softplus(_dot(xn_ref[...], wdt_ref[...]) + dtb_ref[...])
    dt_ref[...] = dt
    a = dt * aneg_ref[...]
    tpos = lax.broadcasted_iota(jnp.int32, (rows, LANE), 0) % CHUNK
    sh = 1
    while sh < CHUNK:
        a = a + jnp.where(tpos >= sh, pltpu.roll(a, sh, axis=0), 0.0)
        sh *= 2
    cs_ref[...] = a
    for b in range(SEQ_TILE):
        cst_ref[b] = a[b * CHUNK:(b + 1) * CHUNK, :].T

    tri = (lax.broadcasted_iota(jnp.int32, (CHUNK, CHUNK), 0)
           >= lax.broadcasted_iota(jnp.int32, (CHUNK, CHUNK), 1))
    heads_per_group = SSD_HEADS // SSD_GROUPS

    def seq_body(b, carry):
        r0 = pl.multiple_of(b * CHUNK, CHUNK)
        rs = pl.ds(r0, CHUNK)
        cs_blk = cs_ref[rs, :]
        cs_last = cs_ref[pl.ds(r0 + CHUNK - 1, 1), :]
        dt_blk = dt_ref[rs, :]
        for g in range(SSD_GROUPS):
            b_blk = xc_ref[rs, pl.ds(SSD_W + g * SSD_STATE, SSD_STATE)]
            c_blk = xc_ref[rs, pl.ds(SSD_W + SSD_GROUPS * SSD_STATE + g * SSD_STATE, SSD_STATE)]
            b_bf = b_blk.astype(BF16)
            c_bf = c_blk.astype(BF16)
            gram = _dot_nt(c_bf, b_bf)
            for hh in range(heads_per_group):
                h = g * heads_per_group + hh
                cs_col = cs_blk[:, h:h + 1]
                cs_row = cst_ref[b, pl.ds(h, 1), :]
                decay = jnp.exp(jnp.where(tri, cs_col - cs_row, -jnp.inf))
                xs_h = xc_ref[rs, pl.ds(h * SSD_HEAD_DIM, SSD_HEAD_DIM)]
                xdt = xs_h * dt_blk[:, h:h + 1]
                y = _dot((gram * decay).astype(BF16), xdt.astype(BF16))
                st = ssd_ref[b, h]
                y = y + _dot_nt(c_bf, st.astype(BF16)) * jnp.exp(cs_col)
                y = y + dskip_ref[:, h:h + 1] * xs_h
                y_ref[rs, pl.ds(h * SSD_HEAD_DIM, SSD_HEAD_DIM)] = y
                to_end = jnp.exp(cs_last[:, h:h + 1] - cs_col)
                upd = _dot_tn((xdt * to_end).astype(BF16), b_bf)
                ssd_ref[b, h] = st * jnp.exp(cs_last[:, h:h + 1]) + upd
        return carry

    lax.fori_loop(0, SEQ_TILE, seq_body, 0)

    z = _dot(xn_ref[...], wz_ref[...])
    y = y_ref[...] * (z * jax.nn.sigmoid(z))
    mix_ref[:, 0:SSD_W] = _rms(y, ssdn_ref[...]).astype(BF16)

    u = _dot(xn_ref[...], wu_ref[...])
    u_ref[...] = u
    half_ch = S5_W // 2
    half_st = S5_LANES // 2
    half_tiles = half_st // LANE
    for hf in range(2):
        bu = _dot(u[:, hf * half_ch:(hf + 1) * half_ch].astype(BF16), wb_ref[hf])
        for k in range(half_tiles):
            for b in range(SEQ_TILE):
                dst = pl.ds(b * S5_PITCH, CHUNK)
                src = slice(b * CHUNK, (b + 1) * CHUNK)
                bure_ref[hf * half_tiles + k, dst, :] = bu[src, k * LANE:(k + 1) * LANE]
                buim_ref[hf * half_tiles + k, dst, :] = bu[src, half_st + k * LANE:half_st + (k + 1) * LANE]
    scan_tiles = 4
    for j in range(S5_LANES // LANE // scan_tiles):
        tiles = [j * scan_tiles + k for k in range(scan_tiles)]
        lr = [lamre_ref[:, pl.ds(k * LANE, LANE)] for k in tiles]
        li = [lamim_ref[:, pl.ds(k * LANE, LANE)] for k in tiles]
        sr = [s5re_ref[:, pl.ds(k * LANE, LANE)] for k in tiles]
        si = [s5im_ref[:, pl.ds(k * LANE, LANE)] for k in tiles]
        for t in range(CHUNK):
            ts = pl.ds(t, SEQ_TILE, stride=S5_PITCH)
            for q, k in enumerate(tiles):
                nr = lr[q] * sr[q] - li[q] * si[q] + bure_ref[k, ts, :]
                ni = lr[q] * si[q] + li[q] * sr[q] + buim_ref[k, ts, :]
                sr[q], si[q] = nr, ni
                bure_ref[k, ts, :] = nr
                buim_ref[k, ts, :] = ni
        for q, k in enumerate(tiles):
            s5re_ref[:, pl.ds(k * LANE, LANE)] = sr[q]
            s5im_ref[:, pl.ds(k * LANE, LANE)] = si[q]
    def seq_rows(ref, k):
        return jnp.concatenate([ref[k, pl.ds(b * S5_PITCH, CHUNK), :] for b in range(SEQ_TILE)], axis=0)

    ys = []
    for j in range(4):
        s_re = jnp.concatenate([seq_rows(bure_ref, 4 * j + k) for k in range(4)], axis=1)
        s_im = jnp.concatenate([seq_rows(buim_ref, 4 * j + k) for k in range(4)], axis=1)
        ys.append(_dot(s_re.astype(BF16), wc_ref[j, 0]) + _dot(s_im.astype(BF16), wc_ref[j, 1]))
    y5 = jnp.concatenate(ys, axis=1) + s5d_ref[...] * u_ref[...]
    y5 = jax.nn.gelu(y5)
    y5 = y5 * jax.nn.sigmoid(_dot(y5.astype(BF16), wglu_ref[...]))
    mix_ref[:, SSD_W:] = y5.astype(BF16)

    h = x_ref[...].reshape(rows, D_MODEL) + _dot(mix_ref[...], wout_ref[...])
    h_ref[...] = h.reshape(SEQ_TILE, CHUNK, D_MODEL)


def _mixer(x, conv0, ssd0, s5re0, s5im0, w):
    nb, seq, _ = x.shape
    grid = (nb // SEQ_TILE, seq // CHUNK)
    rows = SEQ_TILE * CHUNK
    weights = [w["gmix"], w["wz"], w["wxbc"], w["wdt"], w["wu"], w["convw"], w["convb"],
               w["dtb"], w["aneg"], w["dskip"], w["ssdn"],
               w["lamre"], w["lamim"], w["wb"], w["wc"], w["s5d"], w["wglu"], w["wout"]]
    state_specs = [pl.BlockSpec((SEQ_TILE, CONV_K - 1, CONV_CH), lambda i, c: (i, 0, 0)),
                   pl.BlockSpec((SEQ_TILE, SSD_HEADS, SSD_HEAD_DIM, SSD_STATE), lambda i, c: (i, 0, 0, 0)),
                   pl.BlockSpec((SEQ_TILE, S5_LANES), lambda i, c: (i, 0)),
                   pl.BlockSpec((SEQ_TILE, S5_LANES), lambda i, c: (i, 0))]
    x_spec = pl.BlockSpec((SEQ_TILE, CHUNK, D_MODEL), lambda i, c: (i, c, 0))
    return pl.pallas_call(
        _mixer_kernel,
        grid=grid,
        in_specs=[x_spec] + state_specs + [_const_spec(a.shape) for a in weights],
        out_specs=[x_spec] + state_specs,
        out_shape=[jax.ShapeDtypeStruct(x.shape, F32),
                   jax.ShapeDtypeStruct(conv0.shape, F32), jax.ShapeDtypeStruct(ssd0.shape, F32),
                   jax.ShapeDtypeStruct(s5re0.shape, F32), jax.ShapeDtypeStruct(s5im0.shape, F32)],
        scratch_shapes=[
            pltpu.VMEM((rows, D_MODEL), BF16),
            pltpu.VMEM((SEQ_TILE, PAD_ROWS + CHUNK, CONV_CH), F32),
            pltpu.VMEM((rows, CONV_CH), F32),
            pltpu.VMEM((rows, LANE), F32),
            pltpu.VMEM((rows, LANE), F32),
            pltpu.VMEM((SEQ_TILE, LANE, CHUNK), F32),
            pltpu.VMEM((rows, SSD_W), F32),
            pltpu.VMEM((rows, S5_W), F32),
            pltpu.VMEM((S5_LANES // LANE, SEQ_TILE * S5_PITCH, LANE), F32),
            pltpu.VMEM((S5_LANES // LANE, SEQ_TILE * S5_PITCH, LANE), F32),
            pltpu.VMEM((rows, D_MODEL), BF16),
        ],
        compiler_params=pltpu.CompilerParams(dimension_semantics=("arbitrary", "arbitrary"),
                                             vmem_limit_bytes=VMEM_LIMIT),
        name="mixer",
    )(x, conv0, ssd0, s5re0, s5im0, *weights)


def _attn_kernel(h_ref, k_ref, v_ref, gx_ref, wq_ref, wo_ref, gffn_ref, wr_ref, br_ref,
                 h2_ref, tn_ref, eid_ref, gate_ref, o_ref):
    h1 = h_ref[0]
    xn = _rms(h1, gx_ref[...]).astype(BF16)
    q = _dot(xn, wq_ref[...])
    scale = X_HEAD_DIM ** -0.5
    for hd in range(X_HEADS):
        ls = pl.ds(hd * X_HEAD_DIM, X_HEAD_DIM)
        kh = k_ref[0, :, ls].astype(BF16)
        vh = v_ref[0, :, ls].astype(BF16)
        s = _dot_nt(q[:, hd * X_HEAD_DIM:(hd + 1) * X_HEAD_DIM].astype(BF16), kh) * scale
        s = s - jnp.max(s, axis=-1, keepdims=True)
        p = jnp.exp(s)
        p = p / jnp.sum(p, axis=-1, keepdims=True)
        o_ref[:, ls] = _dot(p.astype(BF16), vh).astype(BF16)
    h2 = h1 + _dot(o_ref[...], wo_ref[...])
    h2_ref[0] = h2

    tn = _rms(h2, gffn_ref[...]).astype(BF16)
    tn_ref[0] = tn.astype(F32)
    logits = _dot(tn, wr_ref[...]) + br_ref[...]
    lane = lax.broadcasted_iota(jnp.int32, logits.shape, 1)
    big = jnp.int32(2 ** 30)
    neg = -jnp.inf
    is_g = (lane >= N_EXPERTS) & (lane < N_EXPERTS + N_EXPERT_GROUPS)
    gl = jnp.where(is_g, logits, neg)
    gmax = jnp.max(gl, axis=-1, keepdims=True)
    g_idx = jnp.min(jnp.where(gl == gmax, lane - N_EXPERTS, big), axis=-1, keepdims=True)
    g_prob = 1.0 / jnp.sum(jnp.exp(gl - gmax), axis=-1, keepdims=True)
    in_grp = (lane < N_EXPERTS) & ((lane // EXPERTS_PER_GROUP) == g_idx)
    el = jnp.where(in_grp, logits, neg)
    m1 = jnp.max(el, axis=-1, keepdims=True)
    i1 = jnp.min(jnp.where(el == m1, lane, big), axis=-1, keepdims=True)
    el2 = jnp.where(lane == i1, neg, el)
    m2 = jnp.max(el2, axis=-1, keepdims=True)
    i2 = jnp.min(jnp.where(el2 == m2, lane, big), axis=-1, keepdims=True)
    e2 = jnp.exp(m2 - m1)
    den = 1.0 + e2
    eid_ref[0] = jnp.where(lane == 0, i1, jnp.where(lane == 1, i2, 0))
    gate_ref[0] = jnp.where(lane == 0, (1.0 / den) * g_prob, jnp.where(lane == 1, (e2 / den) * g_prob, 0.0))


def _attn(h1, mem_k, mem_v, w, tile):
    nb, seq, _ = h1.shape
    row_spec = pl.BlockSpec((1, tile, D_MODEL), lambda b, i: (b, i, 0))
    kv_spec = pl.BlockSpec((1, MEM_LEN, D_MODEL), lambda b, i: (b, 0, 0))
    lane_spec = pl.BlockSpec((1, tile, LANE), lambda b, i: (b, i, 0))
    weights = [w["gx"], w["wq"], w["wo"], w["gffn"], w["wr"], w["br"]]
    return pl.pallas_call(
        _attn_kernel,
        grid=(nb, seq // tile),
        in_specs=[row_spec, kv_spec, kv_spec] + [_const_spec(a.shape) for a in weights],
        out_specs=[row_spec, row_spec, lane_spec, lane_spec],
        out_shape=[jax.ShapeDtypeStruct(h1.shape, F32), jax.ShapeDtypeStruct(h1.shape, F32),
                   jax.ShapeDtypeStruct((nb, seq, LANE), jnp.int32),
                   jax.ShapeDtypeStruct((nb, seq, LANE), F32)],
        scratch_shapes=[pltpu.VMEM((tile, D_MODEL), BF16)],
        compiler_params=pltpu.CompilerParams(dimension_semantics=("arbitrary", "arbitrary"),
                                             vmem_limit_bytes=VMEM_LIMIT),
        name="attn_router",
    )(h1, mem_k, mem_v, *weights)


def _route_plan(eid):
    n_tok = eid.shape[0]
    n_asg = 2 * n_tok
    n_tiles = n_asg // MOE_TILE + N_EXPERTS
    e_flat = eid.reshape(n_asg)
    onehot = (e_flat[:, None] == jnp.arange(N_EXPERTS, dtype=jnp.int32)[None, :]).astype(jnp.int32)
    csum = jnp.cumsum(onehot, axis=0)
    rank = jnp.sum(csum * onehot, axis=1) - 1
    counts = csum[-1]
    tiles_e = (counts + MOE_TILE - 1) // MOE_TILE
    tile_end = jnp.cumsum(tiles_e)
    tile_start = tile_end - tiles_e
    pos = jnp.sum(onehot * tile_start[None, :], axis=1) * MOE_TILE + rank
    n_used = tile_end[-1]
    tile_ids = jnp.arange(n_tiles, dtype=jnp.int32)
    tile_expert = jnp.sum((jnp.minimum(tile_ids, n_used - 1)[:, None] >= tile_end[None, :]).astype(jnp.int32),
                          axis=1)
    row_token = jnp.zeros((n_tiles * MOE_TILE,), jnp.int32).at[pos].set(
        jnp.arange(n_asg, dtype=jnp.int32) // 2, unique_indices=True)
    return (pos.reshape(n_tok, 2), row_token.reshape(n_tiles, 1, MOE_TILE),
            tile_expert.astype(jnp.int32), n_used.reshape(1).astype(jnp.int32))


def _row_copy(src_hbm, row, dst, r, sem):
    return pltpu.make_async_copy(src_hbm.at[pl.ds(row, 1), :], dst.at[pl.ds(r, 1), :], sem)


def _experts_kernel(te_ref, nu_ref, rt_cur_ref, rt_nxt_ref, tn_hbm, wgu_ref, wd_ref, y_ref, xbuf, sem):
    i = pl.program_id(0)
    n_used = nu_ref[0]
    slot = i % 2

    def start_gather(rt_ref, s):
        def body(r, carry):
            _row_copy(tn_hbm, rt_ref[0, 0, r], xbuf.at[s], r, sem.at[s]).start()
            return carry
        lax.fori_loop(0, MOE_TILE, body, 0, unroll=8)

    @pl.when(i == 0)
    def _():
        start_gather(rt_cur_ref, 0)

    @pl.when(i + 1 < n_used)
    def _():
        start_gather(rt_nxt_ref, 1 - slot)

    @pl.when(i < n_used)
    def _():
        def wait_body(r, carry):
            _row_copy(tn_hbm, 0, xbuf.at[slot], r, sem.at[slot]).wait()
            return carry
        lax.fori_loop(0, MOE_TILE, wait_body, 0, unroll=8)
        x = xbuf[slot].astype(BF16)
        gu = _dot(x, wgu_ref[0])
        gate = gu[:, :EXPERT_FF]
        hid = (gate * jax.nn.sigmoid(gate)) * gu[:, EXPERT_FF:]
        y_ref[...] = _dot(hid.astype(BF16), wd_ref[0])

    @pl.when(i >= n_used)
    def _():
        y_ref[...] = jnp.zeros_like(y_ref)


def _experts(tn, row_token, tile_expert, n_used, w):
    n_tiles = row_token.shape[0]

    def smem_tile(imap):
        return pl.BlockSpec((1, 1, MOE_TILE), imap, memory_space=pltpu.SMEM)

    grid_spec = pltpu.PrefetchScalarGridSpec(
        num_scalar_prefetch=2,
        grid=(n_tiles,),
        in_specs=[smem_tile(lambda i, te, nu: (i, 0, 0)),
                  smem_tile(lambda i, te, nu: (jnp.minimum(i + 1, n_tiles - 1), 0, 0)),
                  pl.BlockSpec(memory_space=pl.ANY),
                  pl.BlockSpec((1, D_MODEL, 2 * EXPERT_FF), lambda i, te, nu: (te[i], 0, 0)),
                  pl.BlockSpec((1, EXPERT_FF, D_MODEL), lambda i, te, nu: (te[i], 0, 0))],
        out_specs=pl.BlockSpec((MOE_TILE, D_MODEL), lambda i, te, nu: (i, 0)),
        scratch_shapes=[pltpu.VMEM((2, MOE_TILE, D_MODEL), F32), pltpu.SemaphoreType.DMA((2,))],
    )
    return pl.pallas_call(
        _experts_kernel,
        grid_spec=grid_spec,
        out_shape=jax.ShapeDtypeStruct((n_tiles * MOE_TILE, D_MODEL), F32),
        compiler_params=pltpu.CompilerParams(dimension_semantics=("arbitrary",), vmem_limit_bytes=VMEM_LIMIT),
        name="experts",
    )(tile_expert, n_used, row_token, row_token, tn, w["wgu"], w["wdown"])


def _combine_kernel(pos_cur_ref, pos_nxt_ref, ys_hbm, gate_ref, h2_ref, gfin_ref, out_ref, ybuf, sem):
    i = pl.program_id(0)
    n = pl.num_programs(0)
    slot = i % 2

    def start_gather(pos_ref, s):
        def body(r, carry):
            for k in range(2):
                _row_copy(ys_hbm, pos_ref[0, 0, k * COMBINE_TILE + r], ybuf.at[s, k], r, sem.at[s]).start()
            return carry
        lax.fori_loop(0, COMBINE_TILE, body, 0, unroll=8)

    @pl.when(i == 0)
    def _():
        start_gather(pos_cur_ref, 0)

    @pl.when(i + 1 < n)
    def _():
        start_gather(pos_nxt_ref, 1 - slot)

    def wait_body(r, carry):
        for k in range(2):
            _row_copy(ys_hbm, 0, ybuf.at[slot, k], r, sem.at[slot]).wait()
        return carry
    lax.fori_loop(0, COMBINE_TILE, wait_body, 0, unroll=8)

    gates = gate_ref[...]
    moe = ybuf[slot, 0] * gates[:, 0:1] + ybuf[slot, 1] * gates[:, 1:2]
    out_ref[...] = _rms(h2_ref[...] + moe, gfin_ref[...])


def _combine(ys, pos, gates, h2, gfin):
    n_tok = h2.shape[0]
    n_tiles = n_tok // COMBINE_TILE
    pos_tiles = pos.reshape(n_tiles, COMBINE_TILE, 2).transpose(0, 2, 1).reshape(n_tiles, 1, 2 * COMBINE_TILE)

    def smem_tile(imap):
        return pl.BlockSpec((1, 1, 2 * COMBINE_TILE), imap, memory_space=pltpu.SMEM)

    row_spec = pl.BlockSpec((COMBINE_TILE, D_MODEL), lambda i: (i, 0))
    return pl.pallas_call(
        _combine_kernel,
        grid=(n_tiles,),
        in_specs=[smem_tile(lambda i: (i, 0, 0)),
                  smem_tile(lambda i: (jnp.minimum(i + 1, n_tiles - 1), 0, 0)),
                  pl.BlockSpec(memory_space=pl.ANY),
                  pl.BlockSpec((COMBINE_TILE, LANE), lambda i: (i, 0)),
                  row_spec, _const_spec((1, D_MODEL))],
        out_specs=row_spec,
        out_shape=jax.ShapeDtypeStruct((n_tok, D_MODEL), F32),
        scratch_shapes=[pltpu.VMEM((2, 2, COMBINE_TILE, D_MODEL), F32), pltpu.SemaphoreType.DMA((2,))],
        compiler_params=pltpu.CompilerParams(dimension_semantics=("arbitrary",), vmem_limit_bytes=VMEM_LIMIT),
        name="combine",
    )(pos_tiles, pos_tiles, ys, gates, h2, gfin)


def _row(v, width=None):
    v = v.astype(F32).reshape(1, -1)
    if width is not None and v.shape[1] < width:
        v = jnp.pad(v, ((0, 0), (0, width - v.shape[1])))
    return v


def _s5_params(a_re, a_im, log_dt, b_re, b_im, c_re, c_im):
    dt = jnp.exp(log_dt)[:, None]
    mag = jnp.exp(a_re * dt)
    lam_re, lam_im = mag * jnp.cos(a_im * dt), mag * jnp.sin(a_im * dt)
    den = a_re * a_re + a_im * a_im
    f_re = ((lam_re - 1.0) * a_re + lam_im * a_im) / den
    f_im = (lam_im * a_re - (lam_re - 1.0) * a_im) / den
    bb_re = f_re[..., None] * b_re - f_im[..., None] * b_im
    bb_im = f_re[..., None] * b_im + f_im[..., None] * b_re
    half = S5_GROUPS // 2

    def in_proj(bb):
        bb = bb.reshape(2, half, S5_STATE, S5_GROUP_CH)
        eye = jnp.eye(half, dtype=F32)
        m = jnp.einsum("hgnc,gk->hgckn", bb, eye)
        return m.reshape(2, half * S5_GROUP_CH, half * S5_STATE)

    wb = jnp.concatenate([in_proj(bb_re), in_proj(bb_im)], axis=2).astype(BF16)

    def out_proj(cc):
        q = S5_GROUPS // 4
        cc = cc.reshape(4, q, S5_GROUP_CH, S5_STATE)
        eye = jnp.eye(q, dtype=F32)
        m = jnp.einsum("qgcn,gk->qgnkc", cc, eye)
        return m.reshape(4, q * S5_STATE, q * S5_GROUP_CH)

    wc = jnp.stack([out_proj(c_re), -out_proj(c_im)], axis=1).astype(BF16)
    return lam_re.reshape(1, S5_LANES), lam_im.reshape(1, S5_LANES), wb, wc


def kernel(x_prompt, x_sample, cache_conv, state_ssd, state_s5_re, state_s5_im, cache_mem_k, cache_mem_v,
           mem_prompt, norm_mix, w_in, conv_w, conv_b, dt_bias, a_log, d_skip, ssd_norm, s5_a_re, s5_a_im,
           s5_log_dt, s5_b_re, s5_b_im, s5_c_re, s5_c_im, s5_d, s5_w_glu, w_out, norm_x, norm_mem, w_q,
           w_k, w_v, w_o, norm_ffn, w_router_group, b_router_group, w_router_expert, b_router_expert,
           w_gate, w_up, w_down, norm_final):
    depth = norm_mix.shape[0]
    assert depth == 1
    l = 0
    bp, seq_p, _ = x_prompt.shape
    bs, seq_s, _ = x_sample.shape

    o1 = SSD_W
    o2 = o1 + CONV_CH
    o3 = o2 + SSD_HEADS
    lam_re, lam_im, wb, wc = _s5_params(s5_a_re[l], s5_a_im[l], s5_log_dt[l], s5_b_re[l], s5_b_im[l],
                                        s5_c_re[l], s5_c_im[l])
    wm = {
        "gmix": _row(norm_mix[l]),
        "wz": w_in[l][:, :o1].astype(BF16),
        "wxbc": w_in[l][:, o1:o2].astype(BF16),
        "wdt": jnp.pad(w_in[l][:, o2:o3], ((0, 0), (0, LANE - SSD_HEADS))).astype(BF16),
        "wu": w_in[l][:, o3:].astype(BF16),
        "convw": conv_w[l].astype(F32), "convb": _row(conv_b[l]),
        "dtb": _row(dt_bias[l], LANE), "aneg": _row(-jnp.exp(a_log[l].astype(F32)), LANE),
        "dskip": _row(d_skip[l], LANE), "ssdn": _row(ssd_norm[l]),
        "lamre": lam_re, "lamim": lam_im, "wb": wb, "wc": wc,
        "s5d": _row(s5_d[l]), "wglu": s5_w_glu[l].astype(BF16), "wout": w_out[l].astype(BF16),
    }
    wr = jnp.concatenate([w_router_expert[l].reshape(D_MODEL, N_EXPERTS), w_router_group[l]], axis=1)
    br = jnp.concatenate([b_router_expert[l].reshape(N_EXPERTS), b_router_group[l]])
    wa = {
        "gx": _row(norm_x[l]), "wq": w_q[l].astype(BF16), "wo": w_o[l].astype(BF16),
        "gffn": _row(norm_ffn[l]),
        "wr": jnp.pad(wr, ((0, 0), (0, LANE - wr.shape[1]))).astype(BF16), "br": _row(br, LANE),
    }
    we = {"wgu": jnp.concatenate([w_gate[l], w_up[l]], axis=2).astype(BF16), "wdown": w_down[l].astype(BF16)}
    gfin = _row(norm_final)

    mk_p, mv_p = _memkv(mem_prompt.reshape(bp * MEM_LEN, D_MODEL), _row(norm_mem[l]),
                        w_k[l].astype(BF16), w_v[l].astype(BF16))
    mk_p = mk_p.reshape(bp, MEM_LEN, D_MODEL)
    mv_p = mv_p.reshape(bp, MEM_LEN, D_MODEL)

    def group(x, conv0, ssd0, s5re0, s5im0, mem_k, mem_v, attn_tile):
        nb, seq, _ = x.shape
        n_tok = nb * seq
        h1, conv, ssd, s5re, s5im = _mixer(x, conv0, ssd0, s5re0.reshape(nb, S5_LANES),
                                           s5im0.reshape(nb, S5_LANES), wm)
        h2, tn, eid, gates = _attn(h1, mem_k, mem_v, wa, attn_tile)
        pos, row_token, tile_expert, n_used = _route_plan(eid.reshape(n_tok, LANE)[:, :2])
        ys = _experts(tn.reshape(n_tok, D_MODEL), row_token, tile_expert, n_used, we)
        y = _combine(ys, pos, gates.reshape(n_tok, LANE), h2.reshape(n_tok, D_MODEL), gfin)
        return (y.reshape(nb, seq, D_MODEL), conv[None], ssd[None],
                s5re.reshape(1, nb, S5_GROUPS, S5_STATE), s5im.reshape(1, nb, S5_GROUPS, S5_STATE))

    zeros = lambda *s: jnp.zeros(s, F32)
    y_p, conv_p, ssd_p, s5re_p, s5im_p = group(
        x_prompt, zeros(bp, CONV_K - 1, CONV_CH), zeros(bp, SSD_HEADS, SSD_HEAD_DIM, SSD_STATE),
        zeros(bp, S5_GROUPS, S5_STATE), zeros(bp, S5_GROUPS, S5_STATE), mk_p, mv_p,
        min(512, seq_p))
    y_s, conv_s, ssd_s, s5re_s, s5im_s = group(
        x_sample, cache_conv[l], state_ssd[l], state_s5_re[l], state_s5_im[l],
        cache_mem_k[l].reshape(bs, MEM_LEN, D_MODEL), cache_mem_v[l].reshape(bs, MEM_LEN, D_MODEL), seq_s)

    shape_kv = (1, bp, MEM_LEN, X_HEADS, X_HEAD_DIM)
    return (y_p, y_s, conv_p, ssd_p, s5re_p, s5im_p, mk_p.reshape(shape_kv), mv_p.reshape(shape_kv),
            conv_s, ssd_s, s5re_s, s5im_s)
```

```python
import functools
import math

import jax
import jax.numpy as jnp
from jax import lax
from jax.experimental import pallas as pl
from jax.experimental.pallas import tpu as pltpu

F32 = jnp.float32
BF16 = jnp.bfloat16
EPS = 1e-6

D_MODEL = 1024
CHUNK = 64
SEQ_TILE = 8
SSD_W = 512
SSD_HEAD_DIM = 64
SSD_HEADS = 8
SSD_GROUPS = 2
SSD_STATE = 128
CONV_K = 4
CONV_CH = 1024
S5_W = 512
S5_GROUPS = 32
S5_GROUP_CH = 16
S5_STATE = 64
S5_LANES = S5_GROUPS * S5_STATE
MEM_LEN = 256
X_HEADS = 4
X_HEAD_DIM = 256
N_EXPERT_GROUPS = 4
EXPERTS_PER_GROUP = 8
N_EXPERTS = 32
EXPERT_FF = 256
LANE = 128
PAD_ROWS = 8
S5_PITCH = CHUNK + PAD_ROWS
MOE_TILE = 256
COMBINE_TILE = 256
ROW_TILES = D_MODEL // LANE
KEY_SHIFT = 20
VMEM_LIMIT = 56 * 1024 * 1024


def _rms(x, g):
    return x * lax.rsqrt(jnp.mean(x * x, axis=-1, keepdims=True) + EPS) * g


def _dot(a, b):
    return jnp.dot(a, b, preferred_element_type=F32)


def _dot_nt(a, b):
    return lax.dot_general(a, b, (((1,), (1,)), ((), ())), preferred_element_type=F32)


def _dot_tn(a, b):
    return lax.dot_general(a, b, (((0,), (0,)), ((), ())), preferred_element_type=F32)


def _const_spec(shape):
    nd = len(shape)
    return pl.BlockSpec(shape, lambda *_: (0,) * nd)


def _memkv_kernel(m_ref, g_ref, wk_ref, wv_ref, k_ref, v_ref):
    mn = _rms(m_ref[...], g_ref[...]).astype(BF16)
    k_ref[...] = _dot(mn, wk_ref[...])
    v_ref[...] = _dot(mn, wv_ref[...])


def _memkv(mem2d, g, wk, wv):
    rows = mem2d.shape[0]
    tile = 512
    return pl.pallas_call(
        _memkv_kernel,
        grid=(rows // tile,),
        in_specs=[pl.BlockSpec((tile, D_MODEL), lambda i: (i, 0)),
                  _const_spec((1, D_MODEL)), _const_spec((D_MODEL, D_MODEL)), _const_spec((D_MODEL, D_MODEL))],
        out_specs=[pl.BlockSpec((tile, D_MODEL), lambda i: (i, 0))] * 2,
        out_shape=[jax.ShapeDtypeStruct((rows, D_MODEL), F32)] * 2,
        compiler_params=pltpu.CompilerParams(dimension_semantics=("arbitrary",), vmem_limit_bytes=VMEM_LIMIT),
        name="memkv",
    )(mem2d, g, wk, wv)


def _softplus(x):
    return jnp.maximum(x, 0.0) + jnp.log1p(jnp.exp(-jnp.abs(x)))


def _mixer_kernel(x_ref, conv0_ref, ssd0_ref, s5re0_ref, s5im0_ref,
                  gmix_ref, wz_ref, wxbc_ref, wdt_ref, wu_ref, convw_ref, convb_ref,
                  dtb_ref, aneg_ref, dskip_ref, ssdn_ref,
                  lamre_ref, lamim_ref, wb_ref, wc_ref, s5d_ref, wglu_ref, wout_ref,
                  h_ref, conv_ref, ssd_ref, s5re_ref, s5im_ref,
                  xn_ref, xpad_ref, xc_ref, dt_ref, cs_ref, cst_ref, y_ref, u_ref,
                  bure_ref, buim_ref, mix_ref):
    c = pl.program_id(1)
    rows = SEQ_TILE * CHUNK

    @pl.when(c == 0)
    def _():
        xpad_ref[:, PAD_ROWS - (CONV_K - 1):PAD_ROWS, :] = conv0_ref[...]
        ssd_ref[...] = ssd0_ref[...]
        s5re_ref[...] = s5re0_ref[...]
        s5im_ref[...] = s5im0_ref[...]

    x = x_ref[...].reshape(rows, D_MODEL)
    xn_ref[...] = _rms(x, gmix_ref[...]).astype(BF16)

    xpad_ref[:, PAD_ROWS:, :] = _dot(xn_ref[...], wxbc_ref[...]).reshape(SEQ_TILE, CHUNK, CONV_CH)
    acc = convb_ref[...].reshape(1, 1, CONV_CH)
    for k in range(CONV_K):
        lo = PAD_ROWS - (CONV_K - 1) + k
        acc = acc + convw_ref[k:k + 1, :].reshape(1, 1, CONV_CH) * xpad_ref[:, lo:lo + CHUNK, :]
    xc_ref[...] = (acc * jax.nn.sigmoid(acc)).reshape(rows, CONV_CH)
    hist = xpad_ref[:, PAD_ROWS + CHUNK - (CONV_K - 1):, :]
    conv_ref[...] = hist
    xpad_ref[:, PAD_ROWS - (CONV_K - 1):PAD_ROWS, :] = hist

    dt = _softplus(_dot(xn_ref[...], wdt_ref[...]) + dtb_ref[...])
    dt_ref[...] = dt
    a = dt * aneg_ref[...]
    tpos = lax.broadcasted_iota(jnp.int32, (rows, LANE), 0) % CHUNK
    sh = 1
    while sh < CHUNK:
        a = a + jnp.where(tpos >= sh, pltpu.roll(a, sh, axis=0), 0.0)
        sh *= 2
    cs_ref[...] = a
    for b in range(SEQ_TILE):
        cst_ref[b] = a[b * CHUNK:(b + 1) * CHUNK, :].T

    tri = (lax.broadcasted_iota(jnp.int32, (CHUNK, CHUNK), 0)
           >= lax.broadcasted_iota(jnp.int32, (CHUNK, CHUNK), 1))
    heads_per_group = SSD_HEADS // SSD_GROUPS

    def seq_body(b, carry):
        r0 = pl.multiple_of(b * CHUNK, CHUNK)
        rs = pl.ds(r0, CHUNK)
        cs_blk = cs_ref[rs, :]
        cs_last = cs_ref[pl.ds(r0 + CHUNK - 1, 1), :]
        dt_blk = dt_ref[rs, :]
        for g in range(SSD_GROUPS):
            b_blk = xc_ref[rs, pl.ds(SSD_W + g * SSD_STATE, SSD_STATE)]
            c_blk = xc_ref[rs, pl.ds(SSD_W + SSD_GROUPS * SSD_STATE + g * SSD_STATE, SSD_STATE)]
            b_bf = b_blk.astype(BF16)
            c_bf = c_blk.astype(BF16)
            gram = _dot_nt(c_bf, b_bf)
            for hh in range(heads_per_group):
                h = g * heads_per_group + hh
                cs_col = cs_blk[:, h:h + 1]
                cs_row = cst_ref[b, pl.ds(h, 1), :]
                decay = jnp.exp(jnp.where(tri, cs_col - cs_row, -jnp.inf))
                xs_h = xc_ref[rs, pl.ds(h * SSD_HEAD_DIM, SSD_HEAD_DIM)]
                xdt = xs_h * dt_blk[:, h:h + 1]
                y = _dot((gram * decay).astype(BF16), xdt.astype(BF16))
                st = ssd_ref[b, h]
                y = y + _dot_nt(c_bf, st.astype(BF16)) * jnp.exp(cs_col)
                y = y + dskip_ref[:, h:h + 1] * xs_h
                y_ref[rs, pl.ds(h * SSD_HEAD_DIM, SSD_HEAD_DIM)] = y
                to_end = jnp.exp(cs_last[:, h:h + 1] - cs_col)
                upd = _dot_tn((xdt * to_end).astype(BF16), b_bf)
                ssd_ref[b, h] = st * jnp.exp(cs_last[:, h:h + 1]) + upd
        return carry

    lax.fori_loop(0, SEQ_TILE, seq_body, 0)

    z = _dot(xn_ref[...], wz_ref[...])
    y = y_ref[...] * (z * jax.nn.sigmoid(z))
    mix_ref[:, 0:SSD_W] = _rms(y, ssdn_ref[...]).astype(BF16)

    u = _dot(xn_ref[...], wu_ref[...])
    u_ref[...] = u
    half_ch = S5_W // 2
    half_st = S5_LANES // 2
    half_tiles = half_st // LANE
    for hf in range(2):
        bu = _dot(u[:, hf * half_ch:(hf + 1) * half_ch].astype(BF16), wb_ref[hf])
        for k in range(half_tiles):
            for b in range(SEQ_TILE):
                dst = pl.ds(b * S5_PITCH, CHUNK)
                src = slice(b * CHUNK, (b + 1) * CHUNK)
                bure_ref[hf * half_tiles + k, dst, :] = bu[src, k * LANE:(k + 1) * LANE]
                buim_ref[hf * half_tiles + k, dst, :] = bu[src, half_st + k * LANE:half_st + (k + 1) * LANE]
    scan_tiles = 4
    for j in range(S5_LANES // LANE // scan_tiles):
        tiles = [j * scan_tiles + k for k in range(scan_tiles)]
        lr = [lamre_ref[:, pl.ds(k * LANE, LANE)] for k in tiles]
        li = [lamim_ref[:, pl.ds(k * LANE, LANE)] for k in tiles]
        sr = [s5re_ref[:, pl.ds(k * LANE, LANE)] for k in tiles]
        si = [s5im_ref[:, pl.ds(k * LANE, LANE)] for k in tiles]
        for t in range(CHUNK):
            ts = pl.ds(t, SEQ_TILE, stride=S5_PITCH)
            for q, k in enumerate(tiles):
                nr = lr[q] * sr[q] - li[q] * si[q] + bure_ref[k, ts, :]
                ni = lr[q] * si[q] + li[q] * sr[q] + buim_ref[k, ts, :]
                sr[q], si[q] = nr, ni
                bure_ref[k, ts, :] = nr
                buim_ref[k, ts, :] = ni
        for q, k in enumerate(tiles):
            s5re_ref[:, pl.ds(k * LANE, LANE)] = sr[q]
            s5im_ref[:, pl.ds(k * LANE, LANE)] = si[q]
    def seq_rows(ref, k):
        return jnp.concatenate([ref[k, pl.ds(b * S5_PITCH, CHUNK), :] for b in range(SEQ_TILE)], axis=0)

    ys = []
    for j in range(4):
        s_re = jnp.concatenate([seq_rows(bure_ref, 4 * j + k) for k in range(4)], axis=1)
        s_im = jnp.concatenate([seq_rows(buim_ref, 4 * j + k) for k in range(4)], axis=1)
        ys.append(_dot(s_re.astype(BF16), wc_ref[j, 0]) + _dot(s_im.astype(BF16), wc_ref[j, 1]))
    y5 = jnp.concatenate(ys, axis=1) + s5d_ref[...] * u_ref[...]
    y5 = jax.nn.gelu(y5)
    y5 = y5 * jax.nn.sigmoid(_dot(y5.astype(BF16), wglu_ref[...]))
    mix_ref[:, SSD_W:] = y5.astype(BF16)

    h = x_ref[...].reshape(rows, D_MODEL) + _dot(mix_ref[...], wout_ref[...])
    h_ref[...] = h.reshape(SEQ_TILE, CHUNK, D_MODEL)


def _mixer(x, conv0, ssd0, s5re0, s5im0, w):
    nb, seq, _ = x.shape
    grid = (nb // SEQ_TILE, seq // CHUNK)
    rows = SEQ_TILE * CHUNK
    weights = [w["gmix"], w["wz"], w["wxbc"], w["wdt"], w["wu"], w["convw"], w["convb"],
               w["dtb"], w["aneg"], w["dskip"], w["ssdn"],
               w["lamre"], w["lamim"], w["wb"], w["wc"], w["s5d"], w["wglu"], w["wout"]]
    state_specs = [pl.BlockSpec((SEQ_TILE, CONV_K - 1, CONV_CH), lambda i, c: (i, 0, 0)),
                   pl.BlockSpec((SEQ_TILE, SSD_HEADS, SSD_HEAD_DIM, SSD_STATE), lambda i, c: (i, 0, 0, 0)),
                   pl.BlockSpec((SEQ_TILE, S5_LANES), lambda i, c: (i, 0)),
                   pl.BlockSpec((SEQ_TILE, S5_LANES), lambda i, c: (i, 0))]
    x_spec = pl.BlockSpec((SEQ_TILE, CHUNK, D_MODEL), lambda i, c: (i, c, 0))
    return pl.pallas_call(
        _mixer_kernel,
        grid=grid,
        in_specs=[x_spec] + state_specs + [_const_spec(a.shape) for a in weights],
        out_specs=[x_spec] + state_specs,
        out_shape=[jax.ShapeDtypeStruct(x.shape, F32),
                   jax.ShapeDtypeStruct(conv0.shape, F32), jax.ShapeDtypeStruct(ssd0.shape, F32),
                   jax.ShapeDtypeStruct(s5re0.shape, F32), jax.ShapeDtypeStruct(s5im0.shape, F32)],
        scratch_shapes=[
            pltpu.VMEM((rows, D_MODEL), BF16),
            pltpu.VMEM((SEQ_TILE, PAD_ROWS + CHUNK, CONV_CH), F32),
            pltpu.VMEM((rows, CONV_CH), F32),
            pltpu.VMEM((rows, LANE), F32),
            pltpu.VMEM((rows, LANE), F32),
            pltpu.VMEM((SEQ_TILE, LANE, CHUNK), F32),
            pltpu.VMEM((rows, SSD_W), F32),
            pltpu.VMEM((rows, S5_W), F32),
            pltpu.VMEM((S5_LANES // LANE, SEQ_TILE * S5_PITCH, LANE), F32),
            pltpu.VMEM((S5_LANES // LANE, SEQ_TILE * S5_PITCH, LANE), F32),
            pltpu.VMEM((rows, D_MODEL), BF16),
        ],
        compiler_params=pltpu.CompilerParams(dimension_semantics=("arbitrary", "arbitrary"),
                                             vmem_limit_bytes=VMEM_LIMIT),
        name="mixer",
    )(x, conv0, ssd0, s5re0, s5im0, *weights)


def _attn_kernel(h_ref, k_ref, v_ref, gx_ref, wq_ref, wo_ref, gffn_ref, wr_ref, br_ref,
                 h2_ref, tn_ref, eid_ref, gate_ref, o_ref):
    h1 = h_ref[0]
    xn = _rms(h1, gx_ref[...]).astype(BF16)
    q = _dot(xn, wq_ref[...])
    scale = X_HEAD_DIM ** -0.5
    for hd in range(X_HEADS):
        ls = pl.ds(hd * X_HEAD_DIM, X_HEAD_DIM)
        kh = k_ref[0, :, ls].astype(BF16)
        vh = v_ref[0, :, ls].astype(BF16)
        s = _dot_nt(q[:, hd * X_HEAD_DIM:(hd + 1) * X_HEAD_DIM].astype(BF16), kh) * scale
        s = s - jnp.max(s, axis=-1, keepdims=True)
        p = jnp.exp(s)
        p = p / jnp.sum(p, axis=-1, keepdims=True)
        o_ref[:, ls] = _dot(p.astype(BF16), vh).astype(BF16)
    h2 = h1 + _dot(o_ref[...], wo_ref[...])
    h2_ref[0] = h2

    tn = _rms(h2, gffn_ref[...]).astype(BF16)
    tn32 = tn.astype(F32)
    rows = tn32.shape[0]
    for sb in range(ROW_TILES):
        tn_ref[0, pl.ds(sb, rows, stride=ROW_TILES), :] = tn32[:, sb * LANE:(sb + 1) * LANE]
    logits = _dot(tn, wr_ref[...]) + br_ref[...]
    lane = lax.broadcasted_iota(jnp.int32, logits.shape, 1)
    big = jnp.int32(2 ** 30)
    neg = -jnp.inf
    is_g = (lane >= N_EXPERTS) & (lane < N_EXPERTS + N_EXPERT_GROUPS)
    gl = jnp.where(is_g, logits, neg)
    gmax = jnp.max(gl, axis=-1, keepdims=True)
    g_idx = jnp.min(jnp.where(gl == gmax, lane - N_EXPERTS, big), axis=-1, keepdims=True)
    g_prob = 1.0 / jnp.sum(jnp.exp(gl - gmax), axis=-1, keepdims=True)
    in_grp = (lane < N_EXPERTS) & ((lane // EXPERTS_PER_GROUP) == g_idx)
    el = jnp.where(in_grp, logits, neg)
    m1 = jnp.max(el, axis=-1, keepdims=True)
    i1 = jnp.min(jnp.where(el == m1, lane, big), axis=-1, keepdims=True)
    el2 = jnp.where(lane == i1, neg, el)
    m2 = jnp.max(el2, axis=-1, keepdims=True)
    i2 = jnp.min(jnp.where(el2 == m2, lane, big), axis=-1, keepdims=True)
    e2 = jnp.exp(m2 - m1)
    den = 1.0 + e2
    eid_ref[0] = jnp.where(lane == 0, i1, jnp.where(lane == 1, i2, 0))
    gate_ref[0] = jnp.where(lane == 0, (1.0 / den) * g_prob, jnp.where(lane == 1, (e2 / den) * g_prob, 0.0))


def _attn(h1, mem_k, mem_v, w, tile):
    nb, seq, _ = h1.shape
    row_spec = pl.BlockSpec((1, tile, D_MODEL), lambda b, i: (b, i, 0))
    kv_spec = pl.BlockSpec((1, MEM_LEN, D_MODEL), lambda b, i: (b, 0, 0))
    lane_spec = pl.BlockSpec((1, tile, LANE), lambda b, i: (b, i, 0))
    weights = [w["gx"], w["wq"], w["wo"], w["gffn"], w["wr"], w["br"]]
    return pl.pallas_call(
        _attn_kernel,
        grid=(nb, seq // tile),
        in_specs=[row_spec, kv_spec, kv_spec] + [_const_spec(a.shape) for a in weights],
        out_specs=[row_spec, pl.BlockSpec((1, tile * ROW_TILES, LANE), lambda b, i: (b, i, 0)),
                   lane_spec, lane_spec],
        out_shape=[jax.ShapeDtypeStruct(h1.shape, F32),
                   jax.ShapeDtypeStruct((nb, seq * ROW_TILES, LANE), F32),
                   jax.ShapeDtypeStruct((nb, seq, LANE), jnp.int32),
                   jax.ShapeDtypeStruct((nb, seq, LANE), F32)],
        scratch_shapes=[pltpu.VMEM((tile, D_MODEL), BF16)],
        compiler_params=pltpu.CompilerParams(dimension_semantics=("arbitrary", "arbitrary"),
                                             vmem_limit_bytes=VMEM_LIMIT),
        name="attn_router",
    )(h1, mem_k, mem_v, *weights)


def _route_plan(eid):
    n_tok = eid.shape[0]
    n_asg = 2 * n_tok
    assert n_asg % MOE_TILE == 0 and n_asg < (1 << KEY_SHIFT)
    n_tiles = n_asg // MOE_TILE
    n_items = n_tiles + N_EXPERTS
    i32 = jnp.int32
    e_flat = eid.reshape(n_asg)
    a_idx = jnp.arange(n_asg, dtype=i32)
    order = lax.sort(e_flat * (1 << KEY_SHIFT) + a_idx) & ((1 << KEY_SHIFT) - 1)
    row_token = (order // 2).reshape(n_tiles, 1, MOE_TILE)
    onehot = (e_flat[:, None] == jnp.arange(N_EXPERTS, dtype=i32)[None, :]).astype(i32)
    csum = jnp.cumsum(onehot, axis=0)
    rank = jnp.sum(csum * onehot, axis=1) - 1
    counts = csum[-1]
    seg_end = jnp.cumsum(counts)
    seg_start = seg_end - counts
    pos = jnp.sum(onehot * seg_start[None, :], axis=1) + rank
    first_tile = seg_start // MOE_TILE
    last_tile = (seg_end - 1) // MOE_TILE
    items_e = jnp.where(counts > 0, last_tile - first_tile + 1, 0)
    it_end = jnp.cumsum(items_e)
    it_start = it_end - items_e
    w = jnp.arange(n_items, dtype=i32)
    wc = jnp.minimum(w, it_end[-1] - 1)
    it_expert = jnp.sum((wc[:, None] >= it_end[None, :]).astype(i32), axis=1)
    it_tile = first_tile[it_expert] + (wc - it_start[it_expert])
    it_lo = jnp.clip(seg_start[it_expert] - it_tile * MOE_TILE, 0, MOE_TILE)
    it_hi = jnp.clip(seg_end[it_expert] - it_tile * MOE_TILE, 0, MOE_TILE)
    it_valid = (w < it_end[-1]).astype(i32)
    items = tuple(a.astype(i32) for a in (it_tile, it_expert, it_lo, it_hi, it_valid))
    return pos.reshape(n_tok, 2), row_token, items


def _token_copy(src_hbm, src_row, dst, dst_row, sem):
    return pltpu.make_async_copy(src_hbm.at[pl.ds(pl.multiple_of(src_row * ROW_TILES, ROW_TILES), ROW_TILES), :],
                                 dst.at[pl.ds(pl.multiple_of(dst_row * ROW_TILES, ROW_TILES), ROW_TILES), :], sem)


def _experts_kernel(n_tiles, tile_ref, exp_ref, lo_ref, hi_ref, valid_ref, rt_cur_ref, rt_nxt_ref, tn_hbm,
                    wgu_ref, wd_ref, y_ref, xbuf, xb_ref, sem):
    w = pl.program_id(0)
    j = tile_ref[w]
    lo = lo_ref[w]
    hi = hi_ref[w]
    valid = valid_ref[w] == 1
    first = jnp.logical_and(valid, lo == 0)
    slot = j % 2

    def start_gather(rt_ref, s):
        def body(r, carry):
            _token_copy(tn_hbm, rt_ref[0, 0, r], xbuf.at[s], r, sem.at[s]).start()
            return carry
        lax.fori_loop(0, MOE_TILE, body, 0, unroll=8)

    @pl.when(w == 0)
    def _():
        start_gather(rt_cur_ref, 0)

    @pl.when(jnp.logical_and(first, j + 1 < n_tiles))
    def _():
        start_gather(rt_nxt_ref, 1 - slot)

    @pl.when(first)
    def _():
        def wait_body(r, carry):
            _token_copy(tn_hbm, 0, xbuf.at[slot], r, sem.at[slot]).wait()
            return carry
        lax.fori_loop(0, MOE_TILE, wait_body, 0, unroll=8)
        for sb in range(ROW_TILES):
            xb_ref[:, sb * LANE:(sb + 1) * LANE] = xbuf[slot, pl.ds(sb, MOE_TILE, stride=ROW_TILES), :].astype(BF16)

    @pl.when(valid)
    def _():
        gu = _dot(xb_ref[...], wgu_ref[0])
        gate = gu[:, :EXPERT_FF]
        hid = (gate * jax.nn.sigmoid(gate)) * gu[:, EXPERT_FF:]
        y = _dot(hid.astype(BF16), wd_ref[0])
        row = lax.broadcasted_iota(jnp.int32, (MOE_TILE, LANE), 0)
        keep = jnp.logical_and(row >= lo, row < hi)

        @pl.when(lo == 0)
        def _():
            for sb in range(ROW_TILES):
                y_ref[pl.ds(sb, MOE_TILE, stride=ROW_TILES), :] = y[:, sb * LANE:(sb + 1) * LANE]

        @pl.when(lo != 0)
        def _():
            for sb in range(ROW_TILES):
                rows = pl.ds(sb, MOE_TILE, stride=ROW_TILES)
                y_ref[rows, :] = jnp.where(keep, y[:, sb * LANE:(sb + 1) * LANE], y_ref[rows, :])


def _experts(tn, row_token, items, w):
    n_tiles = row_token.shape[0]
    n_items = items[0].shape[0]

    def smem_tile(imap):
        return pl.BlockSpec((1, 1, MOE_TILE), imap, memory_space=pltpu.SMEM)

    grid_spec = pltpu.PrefetchScalarGridSpec(
        num_scalar_prefetch=5,
        grid=(n_items,),
        in_specs=[smem_tile(lambda i, tl, ex, lo, hi, va: (tl[i], 0, 0)),
                  smem_tile(lambda i, tl, ex, lo, hi, va: (jnp.minimum(tl[i] + 1, n_tiles - 1), 0, 0)),
                  pl.BlockSpec(memory_space=pl.ANY),
                  pl.BlockSpec((1, D_MODEL, 2 * EXPERT_FF), lambda i, tl, ex, lo, hi, va: (ex[i], 0, 0)),
                  pl.BlockSpec((1, EXPERT_FF, D_MODEL), lambda i, tl, ex, lo, hi, va: (ex[i], 0, 0))],
        out_specs=pl.BlockSpec((MOE_TILE * ROW_TILES, LANE), lambda i, tl, ex, lo, hi, va: (tl[i], 0)),
        scratch_shapes=[pltpu.VMEM((2, MOE_TILE * ROW_TILES, LANE), F32),
                        pltpu.VMEM((MOE_TILE, D_MODEL), BF16),
                        pltpu.SemaphoreType.DMA((2,))],
    )
    return pl.pallas_call(
        functools.partial(_experts_kernel, n_tiles),
        grid_spec=grid_spec,
        out_shape=jax.ShapeDtypeStruct((n_tiles * MOE_TILE * ROW_TILES, LANE), F32),
        compiler_params=pltpu.CompilerParams(dimension_semantics=("arbitrary",), vmem_limit_bytes=VMEM_LIMIT),
        name="experts",
    )(*items, row_token, row_token, tn, w["wgu"], w["wdown"])


def _combine_kernel(pos_cur_ref, pos_nxt_ref, ys_hbm, gate_ref, h2_ref, gfin_ref, out_ref, ybuf, sem):
    i = pl.program_id(0)
    n = pl.num_programs(0)
    slot = i % 2

    def start_gather(pos_ref, s):
        def body(r, carry):
            for k in range(2):
                _token_copy(ys_hbm, pos_ref[0, 0, k * COMBINE_TILE + r], ybuf.at[s, k], r, sem.at[s]).start()
            return carry
        lax.fori_loop(0, COMBINE_TILE, body, 0, unroll=8)

    @pl.when(i == 0)
    def _():
        start_gather(pos_cur_ref, 0)

    @pl.when(i + 1 < n)
    def _():
        start_gather(pos_nxt_ref, 1 - slot)

    def wait_body(r, carry):
        for k in range(2):
            _token_copy(ys_hbm, 0, ybuf.at[slot, k], r, sem.at[slot]).wait()
        return carry
    lax.fori_loop(0, COMBINE_TILE, wait_body, 0, unroll=8)

    gates = gate_ref[...]
    g0 = gates[:, 0:1]
    g1 = gates[:, 1:2]
    ssq = jnp.zeros((COMBINE_TILE, 1), F32)
    for sb in range(ROW_TILES):
        rows = pl.ds(sb, COMBINE_TILE, stride=ROW_TILES)
        cols = pl.ds(sb * LANE, LANE)
        v = h2_ref[:, cols] + (ybuf[slot, 0, rows, :] * g0 + ybuf[slot, 1, rows, :] * g1)
        ssq = ssq + jnp.sum(v * v, axis=-1, keepdims=True)
        out_ref[:, cols] = v
    out_ref[...] = out_ref[...] * lax.rsqrt(ssq * (1.0 / D_MODEL) + EPS) * gfin_ref[...]


def _combine(ys, pos, gates, h2, gfin):
    n_tok = h2.shape[0]
    n_tiles = n_tok // COMBINE_TILE
    pos_tiles = pos.reshape(n_tiles, COMBINE_TILE, 2).transpose(0, 2, 1).reshape(n_tiles, 1, 2 * COMBINE_TILE)

    def smem_tile(imap):
        return pl.BlockSpec((1, 1, 2 * COMBINE_TILE), imap, memory_space=pltpu.SMEM)

    row_spec = pl.BlockSpec((COMBINE_TILE, D_MODEL), lambda i: (i, 0))
    return pl.pallas_call(
        _combine_kernel,
        grid=(n_tiles,),
        in_specs=[smem_tile(lambda i: (i, 0, 0)),
                  smem_tile(lambda i: (jnp.minimum(i + 1, n_tiles - 1), 0, 0)),
                  pl.BlockSpec(memory_space=pl.ANY),
                  pl.BlockSpec((COMBINE_TILE, LANE), lambda i: (i, 0)),
                  row_spec, _const_spec((1, D_MODEL))],
        out_specs=row_spec,
        out_shape=jax.ShapeDtypeStruct((n_tok, D_MODEL), F32),
        scratch_shapes=[pltpu.VMEM((2, 2, COMBINE_TILE * ROW_TILES, LANE), F32), pltpu.SemaphoreType.DMA((2,))],
        compiler_params=pltpu.CompilerParams(dimension_semantics=("arbitrary",), vmem_limit_bytes=VMEM_LIMIT),
        name="combine",
    )(pos_tiles, pos_tiles, ys, gates, h2, gfin)


def _row(v, width=None):
    v = v.astype(F32).reshape(1, -1)
    if width is not None and v.shape[1] < width:
        v = jnp.pad(v, ((0, 0), (0, width - v.shape[1])))
    return v


def _s5_params(a_re, a_im, log_dt, b_re, b_im, c_re, c_im):
    dt = jnp.exp(log_dt)[:, None]
    mag = jnp.exp(a_re * dt)
    lam_re, lam_im = mag * jnp.cos(a_im * dt), mag * jnp.sin(a_im * dt)
    den = a_re * a_re + a_im * a_im
    f_re = ((lam_re - 1.0) * a_re + lam_im * a_im) / den
    f_im = (lam_im * a_re - (lam_re - 1.0) * a_im) / den
    bb_re = f_re[..., None] * b_re - f_im[..., None] * b_im
    bb_im = f_re[..., None] * b_im + f_im[..., None] * b_re
    half = S5_GROUPS // 2

    def in_proj(bb):
        bb = bb.reshape(2, half, S5_STATE, S5_GROUP_CH)
        eye = jnp.eye(half, dtype=F32)
        m = jnp.einsum("hgnc,gk->hgckn", bb, eye)
        return m.reshape(2, half * S5_GROUP_CH, half * S5_STATE)

    wb = jnp.concatenate([in_proj(bb_re), in_proj(bb_im)], axis=2).astype(BF16)

    def out_proj(cc):
        q = S5_GROUPS // 4
        cc = cc.reshape(4, q, S5_GROUP_CH, S5_STATE)
        eye = jnp.eye(q, dtype=F32)
        m = jnp.einsum("qgcn,gk->qgnkc", cc, eye)
        return m.reshape(4, q * S5_STATE, q * S5_GROUP_CH)

    wc = jnp.stack([out_proj(c_re), -out_proj(c_im)], axis=1).astype(BF16)
    return lam_re.reshape(1, S5_LANES), lam_im.reshape(1, S5_LANES), wb, wc


def kernel(x_prompt, x_sample, cache_conv, state_ssd, state_s5_re, state_s5_im, cache_mem_k, cache_mem_v,
           mem_prompt, norm_mix, w_in, conv_w, conv_b, dt_bias, a_log, d_skip, ssd_norm, s5_a_re, s5_a_im,
           s5_log_dt, s5_b_re, s5_b_im, s5_c_re, s5_c_im, s5_d, s5_w_glu, w_out, norm_x, norm_mem, w_q,
           w_k, w_v, w_o, norm_ffn, w_router_group, b_router_group, w_router_expert, b_router_expert,
           w_gate, w_up, w_down, norm_final):
    depth = norm_mix.shape[0]
    assert depth == 1
    l = 0
    bp, seq_p, _ = x_prompt.shape
    bs, seq_s, _ = x_sample.shape

    o1 = SSD_W
    o2 = o1 + CONV_CH
    o3 = o2 + SSD_HEADS
    lam_re, lam_im, wb, wc = _s5_params(s5_a_re[l], s5_a_im[l], s5_log_dt[l], s5_b_re[l], s5_b_im[l],
                                        s5_c_re[l], s5_c_im[l])
    wm = {
        "gmix": _row(norm_mix[l]),
        "wz": w_in[l][:, :o1].astype(BF16),
        "wxbc": w_in[l][:, o1:o2].astype(BF16),
        "wdt": jnp.pad(w_in[l][:, o2:o3], ((0, 0), (0, LANE - SSD_HEADS))).astype(BF16),
        "wu": w_in[l][:, o3:].astype(BF16),
        "convw": conv_w[l].astype(F32), "convb": _row(conv_b[l]),
        "dtb": _row(dt_bias[l], LANE), "aneg": _row(-jnp.exp(a_log[l].astype(F32)), LANE),
        "dskip": _row(d_skip[l], LANE), "ssdn": _row(ssd_norm[l]),
        "lamre": lam_re, "lamim": lam_im, "wb": wb, "wc": wc,
        "s5d": _row(s5_d[l]), "wglu": s5_w_glu[l].astype(BF16), "wout": w_out[l].astype(BF16),
    }
    wr = jnp.concatenate([w_router_expert[l].reshape(D_MODEL, N_EXPERTS), w_router_group[l]], axis=1)
    br = jnp.concatenate([b_router_expert[l].reshape(N_EXPERTS), b_router_group[l]])
    wa = {
        "gx": _row(norm_x[l]), "wq": w_q[l].astype(BF16), "wo": w_o[l].astype(BF16),
        "gffn": _row(norm_ffn[l]),
        "wr": jnp.pad(wr, ((0, 0), (0, LANE - wr.shape[1]))).astype(BF16), "br": _row(br, LANE),
    }
    we = {"wgu": jnp.concatenate([w_gate[l], w_up[l]], axis=2).astype(BF16), "wdown": w_down[l].astype(BF16)}
    gfin = _row(norm_final)

    mk_p, mv_p = _memkv(mem_prompt.reshape(bp * MEM_LEN, D_MODEL), _row(norm_mem[l]),
                        w_k[l].astype(BF16), w_v[l].astype(BF16))
    mk_p = mk_p.reshape(bp, MEM_LEN, D_MODEL)
    mv_p = mv_p.reshape(bp, MEM_LEN, D_MODEL)

    def group(x, conv0, ssd0, s5re0, s5im0, mem_k, mem_v, attn_tile):
        nb, seq, _ = x.shape
        n_tok = nb * seq
        h1, conv, ssd, s5re, s5im = _mixer(x, conv0, ssd0, s5re0.reshape(nb, S5_LANES),
                                           s5im0.reshape(nb, S5_LANES), wm)
        h2, tn, eid, gates = _attn(h1, mem_k, mem_v, wa, attn_tile)
        pos, row_token, items = _route_plan(eid.reshape(n_tok, LANE)[:, :2])
        ys = _experts(tn.reshape(n_tok * ROW_TILES, LANE), row_token, items, we)
        y = _combine(ys, pos, gates.reshape(n_tok, LANE), h2.reshape(n_tok, D_MODEL), gfin)
        return (y.reshape(nb, seq, D_MODEL), conv[None], ssd[None],
                s5re.reshape(1, nb, S5_GROUPS, S5_STATE), s5im.reshape(1, nb, S5_GROUPS, S5_STATE))

    zeros = lambda *s: jnp.zeros(s, F32)
    y_p, conv_p, ssd_p, s5re_p, s5im_p = group(
        x_prompt, zeros(bp, CONV_K - 1, CONV_CH), zeros(bp, SSD_HEADS, SSD_HEAD_DIM, SSD_STATE),
        zeros(bp, S5_GROUPS, S5_STATE), zeros(bp, S5_GROUPS, S5_STATE), mk_p, mv_p,
        min(512, seq_p))
    y_s, conv_s, ssd_s, s5re_s, s5im_s = group(
        x_sample, cache_conv.reshape(cache_conv.shape[1:]), state_ssd.reshape(state_ssd.shape[1:]),
        state_s5_re.reshape(state_s5_re.shape[1:]), state_s5_im.reshape(state_s5_im.shape[1:]),
        cache_mem_k.reshape(bs, MEM_LEN, D_MODEL), cache_mem_v.reshape(bs, MEM_LEN, D_MODEL), seq_s)

    shape_kv = (1, bp, MEM_LEN, X_HEADS, X_HEAD_DIM)
    return (y_p, y_s, conv_p, ssd_p, s5re_p, s5im_p, mk_p.reshape(shape_kv), mv_p.reshape(shape_kv),
            conv_s, ssd_s, s5re_s, s5im_s)
```

```python
import functools
import math

import jax
import jax.numpy as jnp
from jax import lax
from jax.experimental import pallas as pl
from jax.experimental.pallas import tpu as pltpu

F32 = jnp.float32
BF16 = jnp.bfloat16
EPS = 1e-6

D_MODEL = 1024
CHUNK = 64
SEQ_TILE = 8
SSD_W = 512
SSD_HEAD_DIM = 64
SSD_HEADS = 8
SSD_GROUPS = 2
SSD_STATE = 128
CONV_K = 4
CONV_CH = 1024
S5_W = 512
S5_GROUPS = 32
S5_GROUP_CH = 16
S5_STATE = 64
S5_LANES = S5_GROUPS * S5_STATE
MEM_LEN = 256
X_HEADS = 4
X_HEAD_DIM = 256
N_EXPERT_GROUPS = 4
EXPERTS_PER_GROUP = 8
N_EXPERTS = 32
EXPERT_FF = 256
LANE = 128
PAD_ROWS = 8
S5_PITCH = CHUNK + PAD_ROWS
MOE_TILE = 256
COMBINE_TILE = 256
ROW_TILES = D_MODEL // LANE
KEY_SHIFT = 20
VMEM_LIMIT = 56 * 1024 * 1024


def _rms(x, g):
    return x * lax.rsqrt(jnp.mean(x * x, axis=-1, keepdims=True) + EPS) * g


def _dot(a, b):
    return jnp.dot(a, b, preferred_element_type=F32)


def _dot_nt(a, b):
    return lax.dot_general(a, b, (((1,), (1,)), ((), ())), preferred_element_type=F32)


def _dot_tn(a, b):
    return lax.dot_general(a, b, (((0,), (0,)), ((), ())), preferred_element_type=F32)


def _const_spec(shape):
    nd = len(shape)
    return pl.BlockSpec(shape, lambda *_: (0,) * nd)


def _memkv_kernel(m_ref, g_ref, wk_ref, wv_ref, k_ref, v_ref):
    mn = _rms(m_ref[...], g_ref[...]).astype(BF16)
    k_ref[...] = _dot(mn, wk_ref[...])
    v_ref[...] = _dot(mn, wv_ref[...])


def _memkv(mem2d, g, wk, wv):
    rows = mem2d.shape[0]
    tile = 512
    return pl.pallas_call(
        _memkv_kernel,
        grid=(rows // tile,),
        in_specs=[pl.BlockSpec((tile, D_MODEL), lambda i: (i, 0)),
                  _const_spec((1, D_MODEL)), _const_spec((D_MODEL, D_MODEL)), _const_spec((D_MODEL, D_MODEL))],
        out_specs=[pl.BlockSpec((tile, D_MODEL), lambda i: (i, 0))] * 2,
        out_shape=[jax.ShapeDtypeStruct((rows, D_MODEL), F32)] * 2,
        compiler_params=pltpu.CompilerParams(dimension_semantics=("arbitrary",), vmem_limit_bytes=VMEM_LIMIT),
        name="memkv",
    )(mem2d, g, wk, wv)


def _softplus(x):
    return jnp.maximum(x, 0.0) + jnp.log1p(jnp.exp(-jnp.abs(x)))


def _mixer_kernel(n_chunks, x_ref, conv0_ref, ssd0_ref, s5re0_ref, s5im0_ref,
                  gmix_ref, wz_ref, wxbc_ref, wdt_ref, wu_ref, convw_ref, convb_ref,
                  dtb_ref, aneg_ref, eexp_ref, dskip_ref, ssdn_ref,
                  lamre_ref, lamim_ref, wb_ref, wc_ref, s5d_ref, wglu_ref, wout_ref,
                  h_ref, conv_ref, ssd_ref, s5re_ref, s5im_ref,
                  xn_ref, xpad_ref, xc_ref, dte_ref, cse_ref, crow_ref, st_ref, y_ref, u_ref,
                  bure_ref, buim_ref, mix_ref):
    c = pl.program_id(1)
    rows = SEQ_TILE * CHUNK

    @pl.when(c == 0)
    def _():
        xpad_ref[:, PAD_ROWS - (CONV_K - 1):PAD_ROWS, :] = conv0_ref[...]
        s5re_ref[...] = s5re0_ref[...]
        s5im_ref[...] = s5im0_ref[...]

    x = x_ref[...].reshape(rows, D_MODEL)
    xn_ref[...] = _rms(x, gmix_ref[...]).astype(BF16)

    xpad_ref[:, PAD_ROWS:, :] = _dot(xn_ref[...], wxbc_ref[...]).reshape(SEQ_TILE, CHUNK, CONV_CH)
    acc = convb_ref[...].reshape(1, 1, CONV_CH)
    for k in range(CONV_K):
        lo = PAD_ROWS - (CONV_K - 1) + k
        acc = acc + convw_ref[k:k + 1, :].reshape(1, 1, CONV_CH) * xpad_ref[:, lo:lo + CHUNK, :]
    xc_ref[...] = (acc * jax.nn.sigmoid(acc)).reshape(rows, CONV_CH)
    hist = xpad_ref[:, PAD_ROWS + CHUNK - (CONV_K - 1):, :]
    conv_ref[...] = hist
    xpad_ref[:, PAD_ROWS - (CONV_K - 1):PAD_ROWS, :] = hist

    dt = _softplus(_dot(xn_ref[...], wdt_ref[...]) + dtb_ref[...])
    a = dt * aneg_ref[...]
    tpos = lax.broadcasted_iota(jnp.int32, (rows, LANE), 0) % CHUNK
    sh = 1
    while sh < CHUNK:
        a = a + jnp.where(tpos >= sh, pltpu.roll(a, sh, axis=0), 0.0)
        sh *= 2

    def expand_heads(v):
        hi = v.astype(BF16)
        r1 = v - hi.astype(F32)
        mid = r1.astype(BF16)
        lo = (r1 - mid.astype(F32)).astype(BF16)
        e = eexp_ref[...]
        return _dot(hi, e) + _dot(mid, e) + _dot(lo, e)

    dte_ref[...] = expand_heads(dt)
    cse_ref[...] = expand_heads(a)
    for b in range(SEQ_TILE):
        at = a[b * CHUNK:(b + 1) * CHUNK, :].T
        crow_ref[pl.ds(b, 1), :] = jnp.concatenate([at[h:h + 1, :] for h in range(SSD_HEADS)], axis=1)

    @pl.when(c == 0)
    def _():
        for b in range(SEQ_TILE):
            for h in range(SSD_HEADS):
                st_ref[b, :, h * SSD_HEAD_DIM:(h + 1) * SSD_HEAD_DIM] = ssd0_ref[b, h].T

    gw = SSD_W // SSD_GROUPS
    heads_per_group = SSD_HEADS // SSD_GROUPS
    tri = (lax.broadcasted_iota(jnp.int32, (CHUNK, gw), 0)
           >= lax.broadcasted_iota(jnp.int32, (CHUNK, gw), 1) % CHUNK)
    same_head = ((lax.broadcasted_iota(jnp.int32, (gw, gw), 0) // SSD_HEAD_DIM)
                 == (lax.broadcasted_iota(jnp.int32, (gw, gw), 1) // SSD_HEAD_DIM))

    def seq_body(b, carry):
        r0 = pl.multiple_of(b * CHUNK, CHUNK)
        rs = pl.ds(r0, CHUNK)
        for g in range(SSD_GROUPS):
            ls = pl.ds(g * gw, gw)
            cse = cse_ref[rs, ls]
            cs_last = cse_ref[pl.ds(r0 + CHUNK - 1, 1), ls]
            decay = jnp.exp(jnp.where(tri, cse - crow_ref[pl.ds(b, 1), ls], -jnp.inf))
            xs = xc_ref[rs, ls]
            xdt = xs * dte_ref[rs, ls]
            b_bf = xc_ref[rs, pl.ds(SSD_W + g * SSD_STATE, SSD_STATE)].astype(BF16)
            c_bf = xc_ref[rs, pl.ds(SSD_W + SSD_GROUPS * SSD_STATE + g * SSD_STATE, SSD_STATE)].astype(BF16)
            gram = _dot_nt(c_bf, jnp.concatenate([b_bf] * heads_per_group, axis=0))
            xbd = jnp.where(same_head, jnp.concatenate([xdt] * heads_per_group, axis=0), 0.0).astype(BF16)
            y = _dot((gram * decay).astype(BF16), xbd)
            st = st_ref[b, :, ls]
            y = y + _dot(c_bf, st.astype(BF16)) * jnp.exp(cse)
            y = y + dskip_ref[:, ls] * xs
            y_ref[rs, ls] = y
            upd = _dot_tn(b_bf, (xdt * jnp.exp(cs_last - cse)).astype(BF16))
            st_ref[b, :, ls] = st * jnp.exp(cs_last) + upd
        return carry

    lax.fori_loop(0, SEQ_TILE, seq_body, 0)

    @pl.when(c == n_chunks - 1)
    def _():
        for b in range(SEQ_TILE):
            for h in range(SSD_HEADS):
                ssd_ref[b, h] = st_ref[b, :, h * SSD_HEAD_DIM:(h + 1) * SSD_HEAD_DIM].T

    z = _dot(xn_ref[...], wz_ref[...])
    y = y_ref[...] * (z * jax.nn.sigmoid(z))
    mix_ref[:, 0:SSD_W] = _rms(y, ssdn_ref[...]).astype(BF16)

    u = _dot(xn_ref[...], wu_ref[...])
    u_ref[...] = u
    half_ch = S5_W // 2
    half_st = S5_LANES // 2
    half_tiles = half_st // LANE
    for hf in range(2):
        bu = _dot(u[:, hf * half_ch:(hf + 1) * half_ch].astype(BF16), wb_ref[hf])
        for k in range(half_tiles):
            for b in range(SEQ_TILE):
                dst = pl.ds(b * S5_PITCH, CHUNK)
                src = slice(b * CHUNK, (b + 1) * CHUNK)
                bure_ref[hf * half_tiles + k, dst, :] = bu[src, k * LANE:(k + 1) * LANE]
                buim_ref[hf * half_tiles + k, dst, :] = bu[src, half_st + k * LANE:half_st + (k + 1) * LANE]
    scan_tiles = 4
    for j in range(S5_LANES // LANE // scan_tiles):
        tiles = [j * scan_tiles + k for k in range(scan_tiles)]
        lr = [lamre_ref[:, pl.ds(k * LANE, LANE)] for k in tiles]
        li = [lamim_ref[:, pl.ds(k * LANE, LANE)] for k in tiles]
        sr = [s5re_ref[:, pl.ds(k * LANE, LANE)] for k in tiles]
        si = [s5im_ref[:, pl.ds(k * LANE, LANE)] for k in tiles]
        for t in range(CHUNK):
            ts = pl.ds(t, SEQ_TILE, stride=S5_PITCH)
            for q, k in enumerate(tiles):
                nr = lr[q] * sr[q] - li[q] * si[q] + bure_ref[k, ts, :]
                ni = lr[q] * si[q] + li[q] * sr[q] + buim_ref[k, ts, :]
                sr[q], si[q] = nr, ni
                bure_ref[k, ts, :] = nr
                buim_ref[k, ts, :] = ni
        for q, k in enumerate(tiles):
            s5re_ref[:, pl.ds(k * LANE, LANE)] = sr[q]
            s5im_ref[:, pl.ds(k * LANE, LANE)] = si[q]
    def seq_rows(ref, k):
        return jnp.concatenate([ref[k, pl.ds(b * S5_PITCH, CHUNK), :] for b in range(SEQ_TILE)], axis=0)

    ys = []
    for j in range(4):
        s_re = jnp.concatenate([seq_rows(bure_ref, 4 * j + k) for k in range(4)], axis=1)
        s_im = jnp.concatenate([seq_rows(buim_ref, 4 * j + k) for k in range(4)], axis=1)
        ys.append(_dot(s_re.astype(BF16), wc_ref[j, 0]) + _dot(s_im.astype(BF16), wc_ref[j, 1]))
    y5 = jnp.concatenate(ys, axis=1) + s5d_ref[...] * u_ref[...]
    y5 = jax.nn.gelu(y5)
    y5 = y5 * jax.nn.sigmoid(_dot(y5.astype(BF16), wglu_ref[...]))
    mix_ref[:, SSD_W:] = y5.astype(BF16)

    h = x_ref[...].reshape(rows, D_MODEL) + _dot(mix_ref[...], wout_ref[...])
    h_ref[...] = h.reshape(SEQ_TILE, CHUNK, D_MODEL)


def _mixer(x, conv0, ssd0, s5re0, s5im0, w):
    nb, seq, _ = x.shape
    grid = (nb // SEQ_TILE, seq // CHUNK)
    rows = SEQ_TILE * CHUNK
    weights = [w["gmix"], w["wz"], w["wxbc"], w["wdt"], w["wu"], w["convw"], w["convb"],
               w["dtb"], w["aneg"], w["eexp"], w["dskip"], w["ssdn"],
               w["lamre"], w["lamim"], w["wb"], w["wc"], w["s5d"], w["wglu"], w["wout"]]
    state_specs = [pl.BlockSpec((SEQ_TILE, CONV_K - 1, CONV_CH), lambda i, c: (i, 0, 0)),
                   pl.BlockSpec((SEQ_TILE, SSD_HEADS, SSD_HEAD_DIM, SSD_STATE), lambda i, c: (i, 0, 0, 0)),
                   pl.BlockSpec((SEQ_TILE, S5_LANES), lambda i, c: (i, 0)),
                   pl.BlockSpec((SEQ_TILE, S5_LANES), lambda i, c: (i, 0))]
    x_spec = pl.BlockSpec((SEQ_TILE, CHUNK, D_MODEL), lambda i, c: (i, c, 0))
    return pl.pallas_call(
        functools.partial(_mixer_kernel, grid[1]),
        grid=grid,
        in_specs=[x_spec] + state_specs + [_const_spec(a.shape) for a in weights],
        out_specs=[x_spec] + state_specs,
        out_shape=[jax.ShapeDtypeStruct(x.shape, F32),
                   jax.ShapeDtypeStruct(conv0.shape, F32), jax.ShapeDtypeStruct(ssd0.shape, F32),
                   jax.ShapeDtypeStruct(s5re0.shape, F32), jax.ShapeDtypeStruct(s5im0.shape, F32)],
        scratch_shapes=[
            pltpu.VMEM((rows, D_MODEL), BF16),
            pltpu.VMEM((SEQ_TILE, PAD_ROWS + CHUNK, CONV_CH), F32),
            pltpu.VMEM((rows, CONV_CH), F32),
            pltpu.VMEM((rows, SSD_W), F32),
            pltpu.VMEM((rows, SSD_W), F32),
            pltpu.VMEM((SEQ_TILE, SSD_W), F32),
            pltpu.VMEM((SEQ_TILE, SSD_STATE, SSD_W), F32),
            pltpu.VMEM((rows, SSD_W), F32),
            pltpu.VMEM((rows, S5_W), F32),
            pltpu.VMEM((S5_LANES // LANE, SEQ_TILE * S5_PITCH, LANE), F32),
            pltpu.VMEM((S5_LANES // LANE, SEQ_TILE * S5_PITCH, LANE), F32),
            pltpu.VMEM((rows, D_MODEL), BF16),
        ],
        compiler_params=pltpu.CompilerParams(dimension_semantics=("arbitrary", "arbitrary"),
                                             vmem_limit_bytes=VMEM_LIMIT),
        name="mixer",
    )(x, conv0, ssd0, s5re0, s5im0, *weights)


def _attn_kernel(h_ref, k_ref, v_ref, gx_ref, wq_ref, wo_ref, gffn_ref, wr_ref, br_ref,
                 h2_ref, tn_ref, eid_ref, gate_ref, o_ref):
    h1 = h_ref[0]
    xn = _rms(h1, gx_ref[...]).astype(BF16)
    q = _dot(xn, wq_ref[...])
    scale = X_HEAD_DIM ** -0.5
    for hd in range(X_HEADS):
        ls = pl.ds(hd * X_HEAD_DIM, X_HEAD_DIM)
        kh = k_ref[0, :, ls].astype(BF16)
        vh = v_ref[0, :, ls].astype(BF16)
        s = _dot_nt(q[:, hd * X_HEAD_DIM:(hd + 1) * X_HEAD_DIM].astype(BF16), kh) * scale
        s = s - jnp.max(s, axis=-1, keepdims=True)
        p = jnp.exp(s)
        p = p / jnp.sum(p, axis=-1, keepdims=True)
        o_ref[:, ls] = _dot(p.astype(BF16), vh).astype(BF16)
    h2 = h1 + _dot(o_ref[...], wo_ref[...])
    h2_ref[0] = h2

    tn = _rms(h2, gffn_ref[...]).astype(BF16)
    tn32 = tn.astype(F32)
    rows = tn32.shape[0]
    for sb in range(ROW_TILES):
        tn_ref[0, pl.ds(sb, rows, stride=ROW_TILES), :] = tn32[:, sb * LANE:(sb + 1) * LANE]
    logits = _dot(tn, wr_ref[...]) + br_ref[...]
    lane = lax.broadcasted_iota(jnp.int32, logits.shape, 1)
    big = jnp.int32(2 ** 30)
    neg = -jnp.inf
    is_g = (lane >= N_EXPERTS) & (lane < N_EXPERTS + N_EXPERT_GROUPS)
    gl = jnp.where(is_g, logits, neg)
    gmax = jnp.max(gl, axis=-1, keepdims=True)
    g_idx = jnp.min(jnp.where(gl == gmax, lane - N_EXPERTS, big), axis=-1, keepdims=True)
    g_prob = 1.0 / jnp.sum(jnp.exp(gl - gmax), axis=-1, keepdims=True)
    in_grp = (lane < N_EXPERTS) & ((lane // EXPERTS_PER_GROUP) == g_idx)
    el = jnp.where(in_grp, logits, neg)
    m1 = jnp.max(el, axis=-1, keepdims=True)
    i1 = jnp.min(jnp.where(el == m1, lane, big), axis=-1, keepdims=True)
    el2 = jnp.where(lane == i1, neg, el)
    m2 = jnp.max(el2, axis=-1, keepdims=True)
    i2 = jnp.min(jnp.where(el2 == m2, lane, big), axis=-1, keepdims=True)
    e2 = jnp.exp(m2 - m1)
    den = 1.0 + e2
    eid_ref[0] = jnp.where(lane == 0, i1, jnp.where(lane == 1, i2, 0))
    gate_ref[0] = jnp.where(lane == 0, (1.0 / den) * g_prob, jnp.where(lane == 1, (e2 / den) * g_prob, 0.0))


def _attn(h1, mem_k, mem_v, w, tile):
    nb, seq, _ = h1.shape
    row_spec = pl.BlockSpec((1, tile, D_MODEL), lambda b, i: (b, i, 0))
    kv_spec = pl.BlockSpec((1, MEM_LEN, D_MODEL), lambda b, i: (b, 0, 0))
    lane_spec = pl.BlockSpec((1, tile, LANE), lambda b, i: (b, i, 0))
    weights = [w["gx"], w["wq"], w["wo"], w["gffn"], w["wr"], w["br"]]
    return pl.pallas_call(
        _attn_kernel,
        grid=(nb, seq // tile),
        in_specs=[row_spec, kv_spec, kv_spec] + [_const_spec(a.shape) for a in weights],
        out_specs=[row_spec, pl.BlockSpec((1, tile * ROW_TILES, LANE), lambda b, i: (b, i, 0)),
                   lane_spec, lane_spec],
        out_shape=[jax.ShapeDtypeStruct(h1.shape, F32),
                   jax.ShapeDtypeStruct((nb, seq * ROW_TILES, LANE), F32),
                   jax.ShapeDtypeStruct((nb, seq, LANE), jnp.int32),
                   jax.ShapeDtypeStruct((nb, seq, LANE), F32)],
        scratch_shapes=[pltpu.VMEM((tile, D_MODEL), BF16)],
        compiler_params=pltpu.CompilerParams(dimension_semantics=("arbitrary", "arbitrary"),
                                             vmem_limit_bytes=VMEM_LIMIT),
        name="attn_router",
    )(h1, mem_k, mem_v, *weights)


def _route_plan(eid):
    n_tok = eid.shape[0]
    n_asg = 2 * n_tok
    assert n_asg % MOE_TILE == 0 and n_asg < (1 << KEY_SHIFT)
    n_tiles = n_asg // MOE_TILE
    n_items = n_tiles + N_EXPERTS
    i32 = jnp.int32
    e_flat = eid.reshape(n_asg)
    a_idx = jnp.arange(n_asg, dtype=i32)
    order = lax.sort(e_flat * (1 << KEY_SHIFT) + a_idx) & ((1 << KEY_SHIFT) - 1)
    row_token = (order // 2).reshape(n_tiles, 1, MOE_TILE)
    onehot = (e_flat[:, None] == jnp.arange(N_EXPERTS, dtype=i32)[None, :]).astype(i32)
    csum = jnp.cumsum(onehot, axis=0)
    rank = jnp.sum(csum * onehot, axis=1) - 1
    counts = csum[-1]
    seg_end = jnp.cumsum(counts)
    seg_start = seg_end - counts
    pos = jnp.sum(onehot * seg_start[None, :], axis=1) + rank
    first_tile = seg_start // MOE_TILE
    last_tile = (seg_end - 1) // MOE_TILE
    items_e = jnp.where(counts > 0, last_tile - first_tile + 1, 0)
    it_end = jnp.cumsum(items_e)
    it_start = it_end - items_e
    w = jnp.arange(n_items, dtype=i32)
    wc = jnp.minimum(w, it_end[-1] - 1)
    it_expert = jnp.sum((wc[:, None] >= it_end[None, :]).astype(i32), axis=1)
    it_onehot = (it_expert[:, None] == jnp.arange(N_EXPERTS, dtype=i32)[None, :]).astype(i32)

    def of_item(table):
        return jnp.sum(it_onehot * table[None, :], axis=1)

    it_tile = of_item(first_tile) + (wc - of_item(it_start))
    it_lo = jnp.clip(of_item(seg_start) - it_tile * MOE_TILE, 0, MOE_TILE)
    it_hi = jnp.clip(of_item(seg_end) - it_tile * MOE_TILE, 0, MOE_TILE)
    it_valid = (w < it_end[-1]).astype(i32)
    items = tuple(a.astype(i32) for a in (it_tile, it_expert, it_lo, it_hi, it_valid))
    return pos.reshape(n_tok, 2), row_token, items


def _token_copy(src_hbm, src_row, dst, dst_row, sem):
    return pltpu.make_async_copy(src_hbm.at[pl.ds(pl.multiple_of(src_row * ROW_TILES, ROW_TILES), ROW_TILES), :],
                                 dst.at[pl.ds(pl.multiple_of(dst_row * ROW_TILES, ROW_TILES), ROW_TILES), :], sem)


def _experts_kernel(n_tiles, tile_ref, exp_ref, lo_ref, hi_ref, valid_ref, rt_cur_ref, rt_nxt_ref, tn_hbm,
                    wgu_ref, wd_ref, y_ref, xbuf, xb_ref, sem):
    w = pl.program_id(0)
    j = tile_ref[w]
    lo = lo_ref[w]
    hi = hi_ref[w]
    valid = valid_ref[w] == 1
    first = jnp.logical_and(valid, lo == 0)
    slot = j % 2

    def start_gather(rt_ref, s):
        def body(r, carry):
            _token_copy(tn_hbm, rt_ref[0, 0, r], xbuf.at[s], r, sem.at[s]).start()
            return carry
        lax.fori_loop(0, MOE_TILE, body, 0, unroll=8)

    @pl.when(w == 0)
    def _():
        start_gather(rt_cur_ref, 0)

    @pl.when(jnp.logical_and(first, j + 1 < n_tiles))
    def _():
        start_gather(rt_nxt_ref, 1 - slot)

    @pl.when(first)
    def _():
        def wait_body(r, carry):
            _token_copy(tn_hbm, 0, xbuf.at[slot], r, sem.at[slot]).wait()
            return carry
        lax.fori_loop(0, MOE_TILE, wait_body, 0, unroll=8)
        for sb in range(ROW_TILES):
            xb_ref[:, sb * LANE:(sb + 1) * LANE] = xbuf[slot, pl.ds(sb, MOE_TILE, stride=ROW_TILES), :].astype(BF16)

    @pl.when(valid)
    def _():
        gu = _dot(xb_ref[...], wgu_ref[0])
        gate = gu[:, :EXPERT_FF]
        hid = (gate * jax.nn.sigmoid(gate)) * gu[:, EXPERT_FF:]
        y = _dot(hid.astype(BF16), wd_ref[0])
        row = lax.broadcasted_iota(jnp.int32, (MOE_TILE, LANE), 0)
        keep = jnp.logical_and(row >= lo, row < hi)

        @pl.when(lo == 0)
        def _():
            for sb in range(ROW_TILES):
                y_ref[pl.ds(sb, MOE_TILE, stride=ROW_TILES), :] = y[:, sb * LANE:(sb + 1) * LANE]

        @pl.when(lo != 0)
        def _():
            for sb in range(ROW_TILES):
                rows = pl.ds(sb, MOE_TILE, stride=ROW_TILES)
                y_ref[rows, :] = jnp.where(keep, y[:, sb * LANE:(sb + 1) * LANE], y_ref[rows, :])


def _experts(tn, row_token, items, w):
    n_tiles = row_token.shape[0]
    n_items = items[0].shape[0]

    def smem_tile(imap):
        return pl.BlockSpec((1, 1, MOE_TILE), imap, memory_space=pltpu.SMEM)

    grid_spec = pltpu.PrefetchScalarGridSpec(
        num_scalar_prefetch=5,
        grid=(n_items,),
        in_specs=[smem_tile(lambda i, tl, ex, lo, hi, va: (tl[i], 0, 0)),
                  smem_tile(lambda i, tl, ex, lo, hi, va: (jnp.minimum(tl[i] + 1, n_tiles - 1), 0, 0)),
                  pl.BlockSpec(memory_space=pl.ANY),
                  pl.BlockSpec((1, D_MODEL, 2 * EXPERT_FF), lambda i, tl, ex, lo, hi, va: (ex[i], 0, 0)),
                  pl.BlockSpec((1, EXPERT_FF, D_MODEL), lambda i, tl, ex, lo, hi, va: (ex[i], 0, 0))],
        out_specs=pl.BlockSpec((MOE_TILE * ROW_TILES, LANE), lambda i, tl, ex, lo, hi, va: (tl[i], 0)),
        scratch_shapes=[pltpu.VMEM((2, MOE_TILE * ROW_TILES, LANE), F32),
                        pltpu.VMEM((MOE_TILE, D_MODEL), BF16),
                        pltpu.SemaphoreType.DMA((2,))],
    )
    return pl.pallas_call(
        functools.partial(_experts_kernel, n_tiles),
        grid_spec=grid_spec,
        out_shape=jax.ShapeDtypeStruct((n_tiles * MOE_TILE * ROW_TILES, LANE), F32),
        compiler_params=pltpu.CompilerParams(dimension_semantics=("arbitrary",), vmem_limit_bytes=VMEM_LIMIT),
        name="experts",
    )(*items, row_token, row_token, tn, w["wgu"], w["wdown"])


def _combine_kernel(pos_cur_ref, pos_nxt_ref, ys_hbm, gate_ref, h2_ref, gfin_ref, out_ref, ybuf, sem):
    i = pl.program_id(0)
    n = pl.num_programs(0)
    slot = i % 2

    def start_gather(pos_ref, s):
        def body(r, carry):
            for k in range(2):
                _token_copy(ys_hbm, pos_ref[0, 0, k * COMBINE_TILE + r], ybuf.at[s, k], r, sem.at[s]).start()
            return carry
        lax.fori_loop(0, COMBINE_TILE, body, 0, unroll=8)

    @pl.when(i == 0)
    def _():
        start_gather(pos_cur_ref, 0)

    @pl.when(i + 1 < n)
    def _():
        start_gather(pos_nxt_ref, 1 - slot)

    def wait_body(r, carry):
        for k in range(2):
            _token_copy(ys_hbm, 0, ybuf.at[slot, k], r, sem.at[slot]).wait()
        return carry
    lax.fori_loop(0, COMBINE_TILE, wait_body, 0, unroll=8)

    gates = gate_ref[...]
    g0 = gates[:, 0:1]
    g1 = gates[:, 1:2]
    ssq = jnp.zeros((COMBINE_TILE, 1), F32)
    for sb in range(ROW_TILES):
        rows = pl.ds(sb, COMBINE_TILE, stride=ROW_TILES)
        cols = pl.ds(sb * LANE, LANE)
        v = h2_ref[:, cols] + (ybuf[slot, 0, rows, :] * g0 + ybuf[slot, 1, rows, :] * g1)
        ssq = ssq + jnp.sum(v * v, axis=-1, keepdims=True)
        out_ref[:, cols] = v
    out_ref[...] = out_ref[...] * lax.rsqrt(ssq * (1.0 / D_MODEL) + EPS) * gfin_ref[...]


def _combine(ys, pos, gates, h2, gfin):
    n_tok = h2.shape[0]
    n_tiles = n_tok // COMBINE_TILE
    pos_tiles = pos.reshape(n_tiles, COMBINE_TILE, 2).transpose(0, 2, 1).reshape(n_tiles, 1, 2 * COMBINE_TILE)

    def smem_tile(imap):
        return pl.BlockSpec((1, 1, 2 * COMBINE_TILE), imap, memory_space=pltpu.SMEM)

    row_spec = pl.BlockSpec((COMBINE_TILE, D_MODEL), lambda i: (i, 0))
    return pl.pallas_call(
        _combine_kernel,
        grid=(n_tiles,),
        in_specs=[smem_tile(lambda i: (i, 0, 0)),
                  smem_tile(lambda i: (jnp.minimum(i + 1, n_tiles - 1), 0, 0)),
                  pl.BlockSpec(memory_space=pl.ANY),
                  pl.BlockSpec((COMBINE_TILE, LANE), lambda i: (i, 0)),
                  row_spec, _const_spec((1, D_MODEL))],
        out_specs=row_spec,
        out_shape=jax.ShapeDtypeStruct((n_tok, D_MODEL), F32),
        scratch_shapes=[pltpu.VMEM((2, 2, COMBINE_TILE * ROW_TILES, LANE), F32), pltpu.SemaphoreType.DMA((2,))],
        compiler_params=pltpu.CompilerParams(dimension_semantics=("arbitrary",), vmem_limit_bytes=VMEM_LIMIT),
        name="combine",
    )(pos_tiles, pos_tiles, ys, gates, h2, gfin)


def _row(v, width=None):
    v = v.astype(F32).reshape(1, -1)
    if width is not None and v.shape[1] < width:
        v = jnp.pad(v, ((0, 0), (0, width - v.shape[1])))
    return v


def _s5_params(a_re, a_im, log_dt, b_re, b_im, c_re, c_im):
    dt = jnp.exp(log_dt)[:, None]
    mag = jnp.exp(a_re * dt)
    lam_re, lam_im = mag * jnp.cos(a_im * dt), mag * jnp.sin(a_im * dt)
    den = a_re * a_re + a_im * a_im
    f_re = ((lam_re - 1.0) * a_re + lam_im * a_im) / den
    f_im = (lam_im * a_re - (lam_re - 1.0) * a_im) / den
    bb_re = f_re[..., None] * b_re - f_im[..., None] * b_im
    bb_im = f_re[..., None] * b_im + f_im[..., None] * b_re
    half = S5_GROUPS // 2

    def in_proj(bb):
        bb = bb.reshape(2, half, S5_STATE, S5_GROUP_CH)
        eye = jnp.eye(half, dtype=F32)
        m = jnp.einsum("hgnc,gk->hgckn", bb, eye)
        return m.reshape(2, half * S5_GROUP_CH, half * S5_STATE)

    wb = jnp.concatenate([in_proj(bb_re), in_proj(bb_im)], axis=2).astype(BF16)

    def out_proj(cc):
        q = S5_GROUPS // 4
        cc = cc.reshape(4, q, S5_GROUP_CH, S5_STATE)
        eye = jnp.eye(q, dtype=F32)
        m = jnp.einsum("qgcn,gk->qgnkc", cc, eye)
        return m.reshape(4, q * S5_STATE, q * S5_GROUP_CH)

    wc = jnp.stack([out_proj(c_re), -out_proj(c_im)], axis=1).astype(BF16)
    return lam_re.reshape(1, S5_LANES), lam_im.reshape(1, S5_LANES), wb, wc


def kernel(x_prompt, x_sample, cache_conv, state_ssd, state_s5_re, state_s5_im, cache_mem_k, cache_mem_v,
           mem_prompt, norm_mix, w_in, conv_w, conv_b, dt_bias, a_log, d_skip, ssd_norm, s5_a_re, s5_a_im,
           s5_log_dt, s5_b_re, s5_b_im, s5_c_re, s5_c_im, s5_d, s5_w_glu, w_out, norm_x, norm_mem, w_q,
           w_k, w_v, w_o, norm_ffn, w_router_group, b_router_group, w_router_expert, b_router_expert,
           w_gate, w_up, w_down, norm_final):
    depth = norm_mix.shape[0]
    assert depth == 1
    l = 0
    bp, seq_p, _ = x_prompt.shape
    bs, seq_s, _ = x_sample.shape

    o1 = SSD_W
    o2 = o1 + CONV_CH
    o3 = o2 + SSD_HEADS
    lam_re, lam_im, wb, wc = _s5_params(s5_a_re[l], s5_a_im[l], s5_log_dt[l], s5_b_re[l], s5_b_im[l],
                                        s5_c_re[l], s5_c_im[l])
    wm = {
        "gmix": _row(norm_mix[l]),
        "wz": w_in[l][:, :o1].astype(BF16),
        "wxbc": w_in[l][:, o1:o2].astype(BF16),
        "wdt": jnp.pad(w_in[l][:, o2:o3], ((0, 0), (0, LANE - SSD_HEADS))).astype(BF16),
        "wu": w_in[l][:, o3:].astype(BF16),
        "convw": conv_w[l].astype(F32), "convb": _row(conv_b[l]),
        "dtb": _row(dt_bias[l], LANE), "aneg": _row(-jnp.exp(a_log[l].astype(F32)), LANE),
        "eexp": (jnp.arange(LANE, dtype=jnp.int32)[:, None]
                 == jnp.arange(SSD_W, dtype=jnp.int32)[None, :] // SSD_HEAD_DIM).astype(BF16),
        "dskip": _row(jnp.repeat(d_skip[l].astype(F32), SSD_HEAD_DIM)), "ssdn": _row(ssd_norm[l]),
        "lamre": lam_re, "lamim": lam_im, "wb": wb, "wc": wc,
        "s5d": _row(s5_d[l]), "wglu": s5_w_glu[l].astype(BF16), "wout": w_out[l].astype(BF16),
    }
    wr = jnp.concatenate([w_router_expert[l].reshape(D_MODEL, N_EXPERTS), w_router_group[l]], axis=1)
    br = jnp.concatenate([b_router_expert[l].reshape(N_EXPERTS), b_router_group[l]])
    wa = {
        "gx": _row(norm_x[l]), "wq": w_q[l].astype(BF16), "wo": w_o[l].astype(BF16),
        "gffn": _row(norm_ffn[l]),
        "wr": jnp.pad(wr, ((0, 0), (0, LANE - wr.shape[1]))).astype(BF16), "br": _row(br, LANE),
    }
    we = {"wgu": jnp.concatenate([w_gate[l], w_up[l]], axis=2).astype(BF16), "wdown": w_down[l].astype(BF16)}
    gfin = _row(norm_final)

    mk_p, mv_p = _memkv(mem_prompt.reshape(bp * MEM_LEN, D_MODEL), _row(norm_mem[l]),
                        w_k[l].astype(BF16), w_v[l].astype(BF16))
    mk_p = mk_p.reshape(bp, MEM_LEN, D_MODEL)
    mv_p = mv_p.reshape(bp, MEM_LEN, D_MODEL)

    def group(x, conv0, ssd0, s5re0, s5im0, mem_k, mem_v, attn_tile):
        nb, seq, _ = x.shape
        n_tok = nb * seq
        h1, conv, ssd, s5re, s5im = _mixer(x, conv0, ssd0, s5re0.reshape(nb, S5_LANES),
                                           s5im0.reshape(nb, S5_LANES), wm)
        h2, tn, eid, gates = _attn(h1, mem_k, mem_v, wa, attn_tile)
        pos, row_token, items = _route_plan(eid.reshape(n_tok, LANE)[:, :2])
        ys = _experts(tn.reshape(n_tok * ROW_TILES, LANE), row_token, items, we)
        y = _combine(ys, pos, gates.reshape(n_tok, LANE), h2.reshape(n_tok, D_MODEL), gfin)
        return (y.reshape(nb, seq, D_MODEL), conv[None], ssd[None],
                s5re.reshape(1, nb, S5_GROUPS, S5_STATE), s5im.reshape(1, nb, S5_GROUPS, S5_STATE))

    zeros = lambda *s: jnp.zeros(s, F32)
    y_p, conv_p, ssd_p, s5re_p, s5im_p = group(
        x_prompt, zeros(bp, CONV_K - 1, CONV_CH), zeros(bp, SSD_HEADS, SSD_HEAD_DIM, SSD_STATE),
        zeros(bp, S5_GROUPS, S5_STATE), zeros(bp, S5_GROUPS, S5_STATE), mk_p, mv_p,
        min(512, seq_p))
    y_s, conv_s, ssd_s, s5re_s, s5im_s = group(
        x_sample, cache_conv.reshape(cache_conv.shape[1:]), state_ssd.reshape(state_ssd.shape[1:]),
        state_s5_re.reshape(state_s5_re.shape[1:]), state_s5_im.reshape(state_s5_im.shape[1:]),
        cache_mem_k.reshape(bs, MEM_LEN, D_MODEL), cache_mem_v.reshape(bs, MEM_LEN, D_MODEL), seq_s)

    shape_kv = (1, bp, MEM_LEN, X_HEADS, X_HEAD_DIM)
    return (y_p, y_s, conv_p, ssd_p, s5re_p, s5im_p, mk_p.reshape(shape_kv), mv_p.reshape(shape_kv),
            conv_s, ssd_s, s5re_s, s5im_s)
```

```python
import functools
import math

import jax
import jax.numpy as jnp
from jax import lax
from jax.experimental import pallas as pl
from jax.experimental.pallas import tpu as pltpu

F32 = jnp.float32
BF16 = jnp.bfloat16
EPS = 1e-6

D_MODEL = 1024
CHUNK = 64
SEQ_TILE = 8
SSD_W = 512
SSD_HEAD_DIM = 64
SSD_HEADS = 8
SSD_GROUPS = 2
SSD_STATE = 128
CONV_K = 4
CONV_CH = 1024
S5_W = 512
S5_GROUPS = 32
S5_GROUP_CH = 16
S5_STATE = 64
S5_LANES = S5_GROUPS * S5_STATE
MEM_LEN = 256
X_HEADS = 4
X_HEAD_DIM = 256
N_EXPERT_GROUPS = 4
EXPERTS_PER_GROUP = 8
N_EXPERTS = 32
EXPERT_FF = 256
LANE = 128
PAD_ROWS = 8
S5_PITCH = CHUNK + PAD_ROWS
MOE_TILE = 256
COMBINE_TILE = 256
SAMPLE_ATTN_SEQS = 4
ROW_TILES = D_MODEL // LANE
KEY_SHIFT = 20
VMEM_LIMIT = 56 * 1024 * 1024


def _rms(x, g):
    return x * lax.rsqrt(jnp.mean(x * x, axis=-1, keepdims=True) + EPS) * g


def _dot(a, b):
    return jnp.dot(a, b, preferred_element_type=F32)


def _dot_nt(a, b):
    return lax.dot_general(a, b, (((1,), (1,)), ((), ())), preferred_element_type=F32)


def _dot_tn(a, b):
    return lax.dot_general(a, b, (((0,), (0,)), ((), ())), preferred_element_type=F32)


def _const_spec(shape):
    nd = len(shape)
    return pl.BlockSpec(shape, lambda *_: (0,) * nd)


def _memkv_kernel(m_ref, g_ref, wk_ref, wv_ref, k_ref, v_ref, k5_ref, v5_ref):
    mn = _rms(m_ref[0], g_ref[...]).astype(BF16)
    k = _dot(mn, wk_ref[...])
    v = _dot(mn, wv_ref[...])
    k_ref[0] = k
    v_ref[0] = v
    for hd in range(X_HEADS):
        k5_ref[0, 0, :, hd, :] = k[:, hd * X_HEAD_DIM:(hd + 1) * X_HEAD_DIM]
        v5_ref[0, 0, :, hd, :] = v[:, hd * X_HEAD_DIM:(hd + 1) * X_HEAD_DIM]


def _memkv(mem, g, wk, wv):
    nb = mem.shape[0]
    flat_spec = pl.BlockSpec((1, MEM_LEN, D_MODEL), lambda i: (i, 0, 0))
    head_spec = pl.BlockSpec((1, 1, MEM_LEN, X_HEADS, X_HEAD_DIM), lambda i: (0, i, 0, 0, 0))
    flat_shape = jax.ShapeDtypeStruct((nb, MEM_LEN, D_MODEL), F32)
    head_shape = jax.ShapeDtypeStruct((1, nb, MEM_LEN, X_HEADS, X_HEAD_DIM), F32)
    return pl.pallas_call(
        _memkv_kernel,
        grid=(nb,),
        in_specs=[flat_spec, _const_spec((1, D_MODEL)), _const_spec((D_MODEL, D_MODEL)),
                  _const_spec((D_MODEL, D_MODEL))],
        out_specs=[flat_spec, flat_spec, head_spec, head_spec],
        out_shape=[flat_shape, flat_shape, head_shape, head_shape],
        compiler_params=pltpu.CompilerParams(dimension_semantics=("arbitrary",), vmem_limit_bytes=VMEM_LIMIT),
        name="memkv",
    )(mem, g, wk, wv)


def _softplus(x):
    return jnp.maximum(x, 0.0) + jnp.log1p(jnp.exp(-jnp.abs(x)))


def _mixer_kernel(n_chunks, x_ref, conv0_ref, ssd0_ref, s5re0_ref, s5im0_ref,
                  gmix_ref, wz_ref, wxbc_ref, wdt_ref, wu_ref, convw_ref, convb_ref,
                  dtb_ref, aneg_ref, eexp_ref, dskip_ref, ssdn_ref,
                  lamre_ref, lamim_ref, wb_ref, wc_ref, s5d_ref, wglu_ref, wout_ref,
                  h_ref, conv_ref, ssd_ref, s5re_ref, s5im_ref,
                  xn_ref, xpad_ref, xc_ref, dte_ref, cse_ref, crow_ref, st_ref, y_ref, u_ref,
                  bure_ref, buim_ref, mix_ref):
    c = pl.program_id(1)
    rows = SEQ_TILE * CHUNK

    @pl.when(c == 0)
    def _():
        xpad_ref[:, PAD_ROWS - (CONV_K - 1):PAD_ROWS, :] = conv0_ref[...]
        s5re_ref[...] = s5re0_ref[...]
        s5im_ref[...] = s5im0_ref[...]

    x = x_ref[...].reshape(rows, D_MODEL)
    xn_ref[...] = _rms(x, gmix_ref[...]).astype(BF16)

    xpad_ref[:, PAD_ROWS:, :] = _dot(xn_ref[...], wxbc_ref[...]).reshape(SEQ_TILE, CHUNK, CONV_CH)
    acc = convb_ref[...].reshape(1, 1, CONV_CH)
    for k in range(CONV_K):
        lo = PAD_ROWS - (CONV_K - 1) + k
        acc = acc + convw_ref[k:k + 1, :].reshape(1, 1, CONV_CH) * xpad_ref[:, lo:lo + CHUNK, :]
    xc_ref[...] = (acc * jax.nn.sigmoid(acc)).reshape(rows, CONV_CH)
    hist = xpad_ref[:, PAD_ROWS + CHUNK - (CONV_K - 1):, :]
    conv_ref[...] = hist
    xpad_ref[:, PAD_ROWS - (CONV_K - 1):PAD_ROWS, :] = hist

    dt = _softplus(_dot(xn_ref[...], wdt_ref[...]) + dtb_ref[...])
    a = dt * aneg_ref[...]
    tpos = lax.broadcasted_iota(jnp.int32, (rows, LANE), 0) % CHUNK
    sh = 1
    while sh < CHUNK:
        a = a + jnp.where(tpos >= sh, pltpu.roll(a, sh, axis=0), 0.0)
        sh *= 2

    def expand_heads(v):
        hi = v.astype(BF16)
        r1 = v - hi.astype(F32)
        mid = r1.astype(BF16)
        lo = (r1 - mid.astype(F32)).astype(BF16)
        e = eexp_ref[...]
        return _dot(hi, e) + _dot(mid, e) + _dot(lo, e)

    dte_ref[...] = expand_heads(dt)
    cse_ref[...] = expand_heads(a)
    for b in range(SEQ_TILE):
        at = a[b * CHUNK:(b + 1) * CHUNK, :].T
        crow_ref[pl.ds(b, 1), :] = jnp.concatenate([at[h:h + 1, :] for h in range(SSD_HEADS)], axis=1)

    @pl.when(c == 0)
    def _():
        for b in range(SEQ_TILE):
            for h in range(SSD_HEADS):
                st_ref[b, :, h * SSD_HEAD_DIM:(h + 1) * SSD_HEAD_DIM] = ssd0_ref[b, h].T

    gw = SSD_W // SSD_GROUPS
    heads_per_group = SSD_HEADS // SSD_GROUPS
    tri = (lax.broadcasted_iota(jnp.int32, (CHUNK, gw), 0)
           >= lax.broadcasted_iota(jnp.int32, (CHUNK, gw), 1) % CHUNK)
    same_head = ((lax.broadcasted_iota(jnp.int32, (gw, gw), 0) // SSD_HEAD_DIM)
                 == (lax.broadcasted_iota(jnp.int32, (gw, gw), 1) // SSD_HEAD_DIM))

    def seq_body(b, carry):
        r0 = pl.multiple_of(b * CHUNK, CHUNK)
        rs = pl.ds(r0, CHUNK)
        for g in range(SSD_GROUPS):
            ls = pl.ds(g * gw, gw)
            cse = cse_ref[rs, ls]
            cs_last = cse_ref[pl.ds(r0 + CHUNK - 1, 1), ls]
            decay = jnp.exp(jnp.where(tri, cse - crow_ref[pl.ds(b, 1), ls], -jnp.inf))
            xs = xc_ref[rs, ls]
            xdt = xs * dte_ref[rs, ls]
            b_bf = xc_ref[rs, pl.ds(SSD_W + g * SSD_STATE, SSD_STATE)].astype(BF16)
            c_bf = xc_ref[rs, pl.ds(SSD_W + SSD_GROUPS * SSD_STATE + g * SSD_STATE, SSD_STATE)].astype(BF16)
            gram = _dot_nt(c_bf, jnp.concatenate([b_bf] * heads_per_group, axis=0))
            xbd = jnp.where(same_head, jnp.concatenate([xdt] * heads_per_group, axis=0), 0.0).astype(BF16)
            y = _dot((gram * decay).astype(BF16), xbd)
            st = st_ref[b, :, ls]
            y = y + _dot(c_bf, st.astype(BF16)) * jnp.exp(cse)
            y = y + dskip_ref[:, ls] * xs
            y_ref[rs, ls] = y
            upd = _dot_tn(b_bf, (xdt * jnp.exp(cs_last - cse)).astype(BF16))
            st_ref[b, :, ls] = st * jnp.exp(cs_last) + upd
        return carry

    lax.fori_loop(0, SEQ_TILE, seq_body, 0)

    @pl.when(c == n_chunks - 1)
    def _():
        for b in range(SEQ_TILE):
            for h in range(SSD_HEADS):
                ssd_ref[b, h] = st_ref[b, :, h * SSD_HEAD_DIM:(h + 1) * SSD_HEAD_DIM].T

    z = _dot(xn_ref[...], wz_ref[...])
    y = y_ref[...] * (z * jax.nn.sigmoid(z))
    mix_ref[:, 0:SSD_W] = _rms(y, ssdn_ref[...]).astype(BF16)

    u = _dot(xn_ref[...], wu_ref[...])
    u_ref[...] = u
    half_ch = S5_W // 2
    half_st = S5_LANES // 2
    half_tiles = half_st // LANE
    for hf in range(2):
        bu = _dot(u[:, hf * half_ch:(hf + 1) * half_ch].astype(BF16), wb_ref[hf])
        for k in range(half_tiles):
            for b in range(SEQ_TILE):
                dst = pl.ds(b * S5_PITCH, CHUNK)
                src = slice(b * CHUNK, (b + 1) * CHUNK)
                bure_ref[hf * half_tiles + k, dst, :] = bu[src, k * LANE:(k + 1) * LANE]
                buim_ref[hf * half_tiles + k, dst, :] = bu[src, half_st + k * LANE:half_st + (k + 1) * LANE]
    scan_tiles = 4
    for j in range(S5_LANES // LANE // scan_tiles):
        tiles = [j * scan_tiles + k for k in range(scan_tiles)]
        lr = [lamre_ref[:, pl.ds(k * LANE, LANE)] for k in tiles]
        li = [lamim_ref[:, pl.ds(k * LANE, LANE)] for k in tiles]
        sr = [s5re_ref[:, pl.ds(k * LANE, LANE)] for k in tiles]
        si = [s5im_ref[:, pl.ds(k * LANE, LANE)] for k in tiles]
        for t in range(CHUNK):
            ts = pl.ds(t, SEQ_TILE, stride=S5_PITCH)
            for q, k in enumerate(tiles):
                nr = lr[q] * sr[q] - li[q] * si[q] + bure_ref[k, ts, :]
                ni = lr[q] * si[q] + li[q] * sr[q] + buim_ref[k, ts, :]
                sr[q], si[q] = nr, ni
                bure_ref[k, ts, :] = nr
                buim_ref[k, ts, :] = ni
        for q, k in enumerate(tiles):
            s5re_ref[:, pl.ds(k * LANE, LANE)] = sr[q]
            s5im_ref[:, pl.ds(k * LANE, LANE)] = si[q]
    def seq_rows(ref, k):
        return jnp.concatenate([ref[k, pl.ds(b * S5_PITCH, CHUNK), :] for b in range(SEQ_TILE)], axis=0)

    ys = []
    for j in range(4):
        s_re = jnp.concatenate([seq_rows(bure_ref, 4 * j + k) for k in range(4)], axis=1)
        s_im = jnp.concatenate([seq_rows(buim_ref, 4 * j + k) for k in range(4)], axis=1)
        ys.append(_dot(s_re.astype(BF16), wc_ref[j, 0]) + _dot(s_im.astype(BF16), wc_ref[j, 1]))
    y5 = jnp.concatenate(ys, axis=1) + s5d_ref[...] * u_ref[...]
    y5 = jax.nn.gelu(y5)
    y5 = y5 * jax.nn.sigmoid(_dot(y5.astype(BF16), wglu_ref[...]))
    mix_ref[:, SSD_W:] = y5.astype(BF16)

    h = x_ref[...].reshape(rows, D_MODEL) + _dot(mix_ref[...], wout_ref[...])
    h_ref[...] = h.reshape(SEQ_TILE, CHUNK, D_MODEL)


def _mixer(x, conv0, ssd0, s5re0, s5im0, w):
    nb, seq, _ = x.shape
    grid = (nb // SEQ_TILE, seq // CHUNK)
    rows = SEQ_TILE * CHUNK
    weights = [w["gmix"], w["wz"], w["wxbc"], w["wdt"], w["wu"], w["convw"], w["convb"],
               w["dtb"], w["aneg"], w["eexp"], w["dskip"], w["ssdn"],
               w["lamre"], w["lamim"], w["wb"], w["wc"], w["s5d"], w["wglu"], w["wout"]]
    state_specs = [pl.BlockSpec((SEQ_TILE, CONV_K - 1, CONV_CH), lambda i, c: (i, 0, 0)),
                   pl.BlockSpec((SEQ_TILE, SSD_HEADS, SSD_HEAD_DIM, SSD_STATE), lambda i, c: (i, 0, 0, 0)),
                   pl.BlockSpec((SEQ_TILE, S5_LANES), lambda i, c: (i, 0)),
                   pl.BlockSpec((SEQ_TILE, S5_LANES), lambda i, c: (i, 0))]
    x_spec = pl.BlockSpec((SEQ_TILE, CHUNK, D_MODEL), lambda i, c: (i, c, 0))
    return pl.pallas_call(
        functools.partial(_mixer_kernel, grid[1]),
        grid=grid,
        in_specs=[x_spec] + state_specs + [_const_spec(a.shape) for a in weights],
        out_specs=[x_spec] + state_specs,
        out_shape=[jax.ShapeDtypeStruct(x.shape, F32),
                   jax.ShapeDtypeStruct(conv0.shape, F32), jax.ShapeDtypeStruct(ssd0.shape, F32),
                   jax.ShapeDtypeStruct(s5re0.shape, F32), jax.ShapeDtypeStruct(s5im0.shape, F32)],
        scratch_shapes=[
            pltpu.VMEM((rows, D_MODEL), BF16),
            pltpu.VMEM((SEQ_TILE, PAD_ROWS + CHUNK, CONV_CH), F32),
            pltpu.VMEM((rows, CONV_CH), F32),
            pltpu.VMEM((rows, SSD_W), F32),
            pltpu.VMEM((rows, SSD_W), F32),
            pltpu.VMEM((SEQ_TILE, SSD_W), F32),
            pltpu.VMEM((SEQ_TILE, SSD_STATE, SSD_W), F32),
            pltpu.VMEM((rows, SSD_W), F32),
            pltpu.VMEM((rows, S5_W), F32),
            pltpu.VMEM((S5_LANES // LANE, SEQ_TILE * S5_PITCH, LANE), F32),
            pltpu.VMEM((S5_LANES // LANE, SEQ_TILE * S5_PITCH, LANE), F32),
            pltpu.VMEM((rows, D_MODEL), BF16),
        ],
        compiler_params=pltpu.CompilerParams(dimension_semantics=("arbitrary", "arbitrary"),
                                             vmem_limit_bytes=VMEM_LIMIT),
        name="mixer",
    )(x, conv0, ssd0, s5re0, s5im0, *weights)


def _attn_kernel(n_seq, tile, heads_axis, h_ref, k_ref, v_ref, gx_ref, wq_ref, wo_ref, gffn_ref, wr_ref, br_ref,
                 h2_ref, tn_ref, eid_ref, gate_ref, o_ref, kv_ref):
    rows = n_seq * tile
    h1 = h_ref[...].reshape(rows, D_MODEL)
    xn = _rms(h1, gx_ref[...]).astype(BF16)
    q = _dot(xn, wq_ref[...])
    scale = X_HEAD_DIM ** -0.5
    for sq in range(n_seq):
        rs = slice(sq * tile, (sq + 1) * tile)
        for hd in range(X_HEADS):
            ls = slice(hd * X_HEAD_DIM, (hd + 1) * X_HEAD_DIM)
            if heads_axis:
                kv_ref[0] = k_ref[0, sq, :, hd, :]
                kv_ref[1] = v_ref[0, sq, :, hd, :]
                kh = kv_ref[0].astype(BF16)
                vh = kv_ref[1].astype(BF16)
            else:
                kh = k_ref[sq, :, ls].astype(BF16)
                vh = v_ref[sq, :, ls].astype(BF16)
            s = _dot_nt(q[rs, ls].astype(BF16), kh) * scale
            s = s - jnp.max(s, axis=-1, keepdims=True)
            p = jnp.exp(s)
            p = p / jnp.sum(p, axis=-1, keepdims=True)
            o_ref[rs, ls] = _dot(p.astype(BF16), vh).astype(BF16)
    h2 = h1 + _dot(o_ref[...], wo_ref[...])
    h2_ref[...] = h2.reshape(n_seq, tile, D_MODEL)

    tn = _rms(h2, gffn_ref[...]).astype(BF16)
    tn32 = tn.astype(F32)
    for sq in range(n_seq):
        for sb in range(ROW_TILES):
            tn_ref[sq, pl.ds(sb, tile, stride=ROW_TILES), :] = tn32[sq * tile:(sq + 1) * tile,
                                                                  sb * LANE:(sb + 1) * LANE]
    logits = _dot(tn, wr_ref[...]) + br_ref[...]
    lane = lax.broadcasted_iota(jnp.int32, logits.shape, 1)
    big = jnp.int32(2 ** 30)
    neg = -jnp.inf
    is_g = (lane >= N_EXPERTS) & (lane < N_EXPERTS + N_EXPERT_GROUPS)
    gl = jnp.where(is_g, logits, neg)
    gmax = jnp.max(gl, axis=-1, keepdims=True)
    g_idx = jnp.min(jnp.where(gl == gmax, lane - N_EXPERTS, big), axis=-1, keepdims=True)
    g_prob = 1.0 / jnp.sum(jnp.exp(gl - gmax), axis=-1, keepdims=True)
    in_grp = (lane < N_EXPERTS) & ((lane // EXPERTS_PER_GROUP) == g_idx)
    el = jnp.where(in_grp, logits, neg)
    m1 = jnp.max(el, axis=-1, keepdims=True)
    i1 = jnp.min(jnp.where(el == m1, lane, big), axis=-1, keepdims=True)
    el2 = jnp.where(lane == i1, neg, el)
    m2 = jnp.max(el2, axis=-1, keepdims=True)
    i2 = jnp.min(jnp.where(el2 == m2, lane, big), axis=-1, keepdims=True)
    e2 = jnp.exp(m2 - m1)
    den = 1.0 + e2
    eid = jnp.where(lane == 0, i1, jnp.where(lane == 1, i2, 0))
    gate = jnp.where(lane == 0, (1.0 / den) * g_prob, jnp.where(lane == 1, (e2 / den) * g_prob, 0.0))
    eid_ref[...] = eid.reshape(n_seq, tile, LANE)
    gate_ref[...] = gate.reshape(n_seq, tile, LANE)


def _attn(h1, mem_k, mem_v, w, n_seq, tile):
    nb, seq, _ = h1.shape
    heads_axis = mem_k.ndim == 5
    row_spec = pl.BlockSpec((n_seq, tile, D_MODEL), lambda b, i: (b, i, 0))
    if heads_axis:
        kv_spec = pl.BlockSpec((1, n_seq, MEM_LEN, X_HEADS, X_HEAD_DIM), lambda b, i: (0, b, 0, 0, 0))
    else:
        kv_spec = pl.BlockSpec((n_seq, MEM_LEN, D_MODEL), lambda b, i: (b, 0, 0))
    lane_spec = pl.BlockSpec((n_seq, tile, LANE), lambda b, i: (b, i, 0))
    weights = [w["gx"], w["wq"], w["wo"], w["gffn"], w["wr"], w["br"]]
    return pl.pallas_call(
        functools.partial(_attn_kernel, n_seq, tile, heads_axis),
        grid=(nb // n_seq, seq // tile),
        in_specs=[row_spec, kv_spec, kv_spec] + [_const_spec(a.shape) for a in weights],
        out_specs=[row_spec, pl.BlockSpec((n_seq, tile * ROW_TILES, LANE), lambda b, i: (b, i, 0)),
                   lane_spec, lane_spec],
        out_shape=[jax.ShapeDtypeStruct(h1.shape, F32),
                   jax.ShapeDtypeStruct((nb, seq * ROW_TILES, LANE), F32),
                   jax.ShapeDtypeStruct((nb, seq, LANE), jnp.int32),
                   jax.ShapeDtypeStruct((nb, seq, LANE), F32)],
        scratch_shapes=[pltpu.VMEM((n_seq * tile, D_MODEL), BF16),
                        pltpu.VMEM((2, MEM_LEN, X_HEAD_DIM), F32)],
        compiler_params=pltpu.CompilerParams(dimension_semantics=("arbitrary", "arbitrary"),
                                             vmem_limit_bytes=VMEM_LIMIT),
        name="attn_router",
    )(h1, mem_k, mem_v, *weights)


def _route_plan(eid):
    n_tok = eid.shape[0]
    n_asg = 2 * n_tok
    assert n_asg % MOE_TILE == 0 and n_asg < (1 << KEY_SHIFT)
    n_tiles = n_asg // MOE_TILE
    n_items = n_tiles + N_EXPERTS
    i32 = jnp.int32
    e_flat = eid.reshape(n_asg)
    a_idx = jnp.arange(n_asg, dtype=i32)
    order = lax.sort(e_flat * (1 << KEY_SHIFT) + a_idx) & ((1 << KEY_SHIFT) - 1)
    row_token = (order // 2).reshape(n_tiles, 1, MOE_TILE)
    onehot = (e_flat[:, None] == jnp.arange(N_EXPERTS, dtype=i32)[None, :]).astype(i32)
    csum = jnp.cumsum(onehot, axis=0)
    rank = jnp.sum(csum * onehot, axis=1) - 1
    counts = csum[-1]
    seg_end = jnp.cumsum(counts)
    seg_start = seg_end - counts
    pos = jnp.sum(onehot * seg_start[None, :], axis=1) + rank
    first_tile = seg_start // MOE_TILE
    last_tile = (seg_end - 1) // MOE_TILE
    items_e = jnp.where(counts > 0, last_tile - first_tile + 1, 0)
    it_end = jnp.cumsum(items_e)
    it_start = it_end - items_e
    w = jnp.arange(n_items, dtype=i32)
    wc = jnp.minimum(w, it_end[-1] - 1)
    it_expert = jnp.sum((wc[:, None] >= it_end[None, :]).astype(i32), axis=1)
    it_onehot = (it_expert[:, None] == jnp.arange(N_EXPERTS, dtype=i32)[None, :]).astype(i32)

    def of_item(table):
        return jnp.sum(it_onehot * table[None, :], axis=1)

    it_tile = of_item(first_tile) + (wc - of_item(it_start))
    it_lo = jnp.clip(of_item(seg_start) - it_tile * MOE_TILE, 0, MOE_TILE)
    it_hi = jnp.clip(of_item(seg_end) - it_tile * MOE_TILE, 0, MOE_TILE)
    it_valid = (w < it_end[-1]).astype(i32)
    items = tuple(a.astype(i32) for a in (it_tile, it_expert, it_lo, it_hi, it_valid))
    return pos.reshape(n_tok, 2), row_token, items


def _token_copy(src_hbm, src_row, dst, dst_row, sem):
    return pltpu.make_async_copy(src_hbm.at[pl.ds(pl.multiple_of(src_row * ROW_TILES, ROW_TILES), ROW_TILES), :],
                                 dst.at[pl.ds(pl.multiple_of(dst_row * ROW_TILES, ROW_TILES), ROW_TILES), :], sem)


def _experts_kernel(n_tiles, tile_ref, exp_ref, lo_ref, hi_ref, valid_ref, rt_cur_ref, rt_nxt_ref, tn_hbm,
                    wg_ref, wu_ref, wd_ref, y_ref, xbuf, xb_ref, wgu_bf, wd_bf, sem):
    w = pl.program_id(0)
    j = tile_ref[w]
    lo = lo_ref[w]
    hi = hi_ref[w]
    valid = valid_ref[w] == 1
    first = jnp.logical_and(valid, lo == 0)
    slot = j % 2

    @pl.when(jnp.logical_or(w == 0, exp_ref[w] != exp_ref[jnp.maximum(w - 1, 0)]))
    def _():
        wgu_bf[:, :EXPERT_FF] = wg_ref[0, 0].astype(BF16)
        wgu_bf[:, EXPERT_FF:] = wu_ref[0, 0].astype(BF16)
        wd_bf[...] = wd_ref[0, 0].astype(BF16)

    def start_gather(rt_ref, s):
        def body(r, carry):
            _token_copy(tn_hbm, rt_ref[0, 0, r], xbuf.at[s], r, sem.at[s]).start()
            return carry
        lax.fori_loop(0, MOE_TILE, body, 0, unroll=8)

    @pl.when(w == 0)
    def _():
        start_gather(rt_cur_ref, 0)

    @pl.when(jnp.logical_and(first, j + 1 < n_tiles))
    def _():
        start_gather(rt_nxt_ref, 1 - slot)

    @pl.when(first)
    def _():
        def wait_body(r, carry):
            _token_copy(tn_hbm, 0, xbuf.at[slot], r, sem.at[slot]).wait()
            return carry
        lax.fori_loop(0, MOE_TILE, wait_body, 0, unroll=8)
        for sb in range(ROW_TILES):
            xb_ref[:, sb * LANE:(sb + 1) * LANE] = xbuf[slot, pl.ds(sb, MOE_TILE, stride=ROW_TILES), :].astype(BF16)

    @pl.when(valid)
    def _():
        gu = _dot(xb_ref[...], wgu_bf[...])
        gate = gu[:, :EXPERT_FF]
        hid = (gate * jax.nn.sigmoid(gate)) * gu[:, EXPERT_FF:]
        y = _dot(hid.astype(BF16), wd_bf[...])
        row = lax.broadcasted_iota(jnp.int32, (MOE_TILE, LANE), 0)
        keep = jnp.logical_and(row >= lo, row < hi)

        @pl.when(lo == 0)
        def _():
            for sb in range(ROW_TILES):
                y_ref[pl.ds(sb, MOE_TILE, stride=ROW_TILES), :] = y[:, sb * LANE:(sb + 1) * LANE]

        @pl.when(lo != 0)
        def _():
            for sb in range(ROW_TILES):
                rows = pl.ds(sb, MOE_TILE, stride=ROW_TILES)
                y_ref[rows, :] = jnp.where(keep, y[:, sb * LANE:(sb + 1) * LANE], y_ref[rows, :])


def _experts(tn, row_token, items, w):
    n_tiles = row_token.shape[0]
    n_items = items[0].shape[0]

    def smem_tile(imap):
        return pl.BlockSpec((1, 1, MOE_TILE), imap, memory_space=pltpu.SMEM)

    grid_spec = pltpu.PrefetchScalarGridSpec(
        num_scalar_prefetch=5,
        grid=(n_items,),
        in_specs=[smem_tile(lambda i, tl, ex, lo, hi, va: (tl[i], 0, 0)),
                  smem_tile(lambda i, tl, ex, lo, hi, va: (jnp.minimum(tl[i] + 1, n_tiles - 1), 0, 0)),
                  pl.BlockSpec(memory_space=pl.ANY),
                  pl.BlockSpec((1, 1, D_MODEL, EXPERT_FF), lambda i, tl, ex, lo, hi, va: (0, ex[i], 0, 0)),
                  pl.BlockSpec((1, 1, D_MODEL, EXPERT_FF), lambda i, tl, ex, lo, hi, va: (0, ex[i], 0, 0)),
                  pl.BlockSpec((1, 1, EXPERT_FF, D_MODEL), lambda i, tl, ex, lo, hi, va: (0, ex[i], 0, 0))],
        out_specs=pl.BlockSpec((MOE_TILE * ROW_TILES, LANE), lambda i, tl, ex, lo, hi, va: (tl[i], 0)),
        scratch_shapes=[pltpu.VMEM((2, MOE_TILE * ROW_TILES, LANE), F32),
                        pltpu.VMEM((MOE_TILE, D_MODEL), BF16),
                        pltpu.VMEM((D_MODEL, 2 * EXPERT_FF), BF16),
                        pltpu.VMEM((EXPERT_FF, D_MODEL), BF16),
                        pltpu.SemaphoreType.DMA((2,))],
    )
    return pl.pallas_call(
        functools.partial(_experts_kernel, n_tiles),
        grid_spec=grid_spec,
        out_shape=jax.ShapeDtypeStruct((n_tiles * MOE_TILE * ROW_TILES, LANE), F32),
        compiler_params=pltpu.CompilerParams(dimension_semantics=("arbitrary",), vmem_limit_bytes=VMEM_LIMIT),
        name="experts",
    )(*items, row_token, row_token, tn, w["wgate"], w["wup"], w["wdown"])


def _combine_kernel(pos_cur_ref, pos_nxt_ref, ys_hbm, gate_ref, h2_ref, gfin_ref, out_ref, ybuf, sem):
    i = pl.program_id(0)
    n = pl.num_programs(0)
    slot = i % 2

    def start_gather(pos_ref, s):
        def body(r, carry):
            for k in range(2):
                _token_copy(ys_hbm, pos_ref[0, 0, k * COMBINE_TILE + r], ybuf.at[s, k], r, sem.at[s]).start()
            return carry
        lax.fori_loop(0, COMBINE_TILE, body, 0, unroll=8)

    @pl.when(i == 0)
    def _():
        start_gather(pos_cur_ref, 0)

    @pl.when(i + 1 < n)
    def _():
        start_gather(pos_nxt_ref, 1 - slot)

    def wait_body(r, carry):
        for k in range(2):
            _token_copy(ys_hbm, 0, ybuf.at[slot, k], r, sem.at[slot]).wait()
        return carry
    lax.fori_loop(0, COMBINE_TILE, wait_body, 0, unroll=8)

    gates = gate_ref[...]
    g0 = gates[:, 0:1]
    g1 = gates[:, 1:2]
    ssq = jnp.zeros((COMBINE_TILE, 1), F32)
    for sb in range(ROW_TILES):
        rows = pl.ds(sb, COMBINE_TILE, stride=ROW_TILES)
        cols = pl.ds(sb * LANE, LANE)
        v = h2_ref[:, cols] + (ybuf[slot, 0, rows, :] * g0 + ybuf[slot, 1, rows, :] * g1)
        ssq = ssq + jnp.sum(v * v, axis=-1, keepdims=True)
        out_ref[:, cols] = v
    out_ref[...] = out_ref[...] * lax.rsqrt(ssq * (1.0 / D_MODEL) + EPS) * gfin_ref[...]


def _combine(ys, pos, gates, h2, gfin):
    n_tok = h2.shape[0]
    n_tiles = n_tok // COMBINE_TILE
    pos_tiles = pos.reshape(n_tiles, COMBINE_TILE, 2).transpose(0, 2, 1).reshape(n_tiles, 1, 2 * COMBINE_TILE)

    def smem_tile(imap):
        return pl.BlockSpec((1, 1, 2 * COMBINE_TILE), imap, memory_space=pltpu.SMEM)

    row_spec = pl.BlockSpec((COMBINE_TILE, D_MODEL), lambda i: (i, 0))
    return pl.pallas_call(
        _combine_kernel,
        grid=(n_tiles,),
        in_specs=[smem_tile(lambda i: (i, 0, 0)),
                  smem_tile(lambda i: (jnp.minimum(i + 1, n_tiles - 1), 0, 0)),
                  pl.BlockSpec(memory_space=pl.ANY),
                  pl.BlockSpec((COMBINE_TILE, LANE), lambda i: (i, 0)),
                  row_spec, _const_spec((1, D_MODEL))],
        out_specs=row_spec,
        out_shape=jax.ShapeDtypeStruct((n_tok, D_MODEL), F32),
        scratch_shapes=[pltpu.VMEM((2, 2, COMBINE_TILE * ROW_TILES, LANE), F32), pltpu.SemaphoreType.DMA((2,))],
        compiler_params=pltpu.CompilerParams(dimension_semantics=("arbitrary",), vmem_limit_bytes=VMEM_LIMIT),
        name="combine",
    )(pos_tiles, pos_tiles, ys, gates, h2, gfin)


def _row(v, width=None):
    v = v.astype(F32).reshape(1, -1)
    if width is not None and v.shape[1] < width:
        v = jnp.pad(v, ((0, 0), (0, width - v.shape[1])))
    return v


def _s5_params(a_re, a_im, log_dt, b_re, b_im, c_re, c_im):
    dt = jnp.exp(log_dt)[:, None]
    mag = jnp.exp(a_re * dt)
    lam_re, lam_im = mag * jnp.cos(a_im * dt), mag * jnp.sin(a_im * dt)
    den = a_re * a_re + a_im * a_im
    f_re = ((lam_re - 1.0) * a_re + lam_im * a_im) / den
    f_im = (lam_im * a_re - (lam_re - 1.0) * a_im) / den
    bb_re = f_re[..., None] * b_re - f_im[..., None] * b_im
    bb_im = f_re[..., None] * b_im + f_im[..., None] * b_re
    half = S5_GROUPS // 2

    def in_proj(bb):
        bb = bb.reshape(2, half, S5_STATE, S5_GROUP_CH)
        eye = jnp.eye(half, dtype=F32)
        m = jnp.einsum("hgnc,gk->hgckn", bb, eye)
        return m.reshape(2, half * S5_GROUP_CH, half * S5_STATE)

    wb = jnp.concatenate([in_proj(bb_re), in_proj(bb_im)], axis=2).astype(BF16)

    def out_proj(cc):
        q = S5_GROUPS // 4
        cc = cc.reshape(4, q, S5_GROUP_CH, S5_STATE)
        eye = jnp.eye(q, dtype=F32)
        m = jnp.einsum("qgcn,gk->qgnkc", cc, eye)
        return m.reshape(4, q * S5_STATE, q * S5_GROUP_CH)

    wc = jnp.stack([out_proj(c_re), -out_proj(c_im)], axis=1).astype(BF16)
    return lam_re.reshape(1, S5_LANES), lam_im.reshape(1, S5_LANES), wb, wc


def kernel(x_prompt, x_sample, cache_conv, state_ssd, state_s5_re, state_s5_im, cache_mem_k, cache_mem_v,
           mem_prompt, norm_mix, w_in, conv_w, conv_b, dt_bias, a_log, d_skip, ssd_norm, s5_a_re, s5_a_im,
           s5_log_dt, s5_b_re, s5_b_im, s5_c_re, s5_c_im, s5_d, s5_w_glu, w_out, norm_x, norm_mem, w_q,
           w_k, w_v, w_o, norm_ffn, w_router_group, b_router_group, w_router_expert, b_router_expert,
           w_gate, w_up, w_down, norm_final):
    depth = norm_mix.shape[0]
    assert depth == 1
    l = 0
    bp, seq_p, _ = x_prompt.shape
    bs, seq_s, _ = x_sample.shape

    o1 = SSD_W
    o2 = o1 + CONV_CH
    o3 = o2 + SSD_HEADS
    lam_re, lam_im, wb, wc = _s5_params(s5_a_re[l], s5_a_im[l], s5_log_dt[l], s5_b_re[l], s5_b_im[l],
                                        s5_c_re[l], s5_c_im[l])
    wm = {
        "gmix": _row(norm_mix[l]),
        "wz": w_in[l][:, :o1].astype(BF16),
        "wxbc": w_in[l][:, o1:o2].astype(BF16),
        "wdt": jnp.pad(w_in[l][:, o2:o3], ((0, 0), (0, LANE - SSD_HEADS))).astype(BF16),
        "wu": w_in[l][:, o3:].astype(BF16),
        "convw": conv_w[l].astype(F32), "convb": _row(conv_b[l]),
        "dtb": _row(dt_bias[l], LANE), "aneg": _row(-jnp.exp(a_log[l].astype(F32)), LANE),
        "eexp": (jnp.arange(LANE, dtype=jnp.int32)[:, None]
                 == jnp.arange(SSD_W, dtype=jnp.int32)[None, :] // SSD_HEAD_DIM).astype(BF16),
        "dskip": _row(jnp.repeat(d_skip[l].astype(F32), SSD_HEAD_DIM)), "ssdn": _row(ssd_norm[l]),
        "lamre": lam_re, "lamim": lam_im, "wb": wb, "wc": wc,
        "s5d": _row(s5_d[l]), "wglu": s5_w_glu[l].astype(BF16), "wout": w_out[l].astype(BF16),
    }
    wr = jnp.concatenate([w_router_expert[l].reshape(D_MODEL, N_EXPERTS), w_router_group[l]], axis=1)
    br = jnp.concatenate([b_router_expert[l].reshape(N_EXPERTS), b_router_group[l]])
    wa = {
        "gx": _row(norm_x[l]), "wq": w_q[l].astype(BF16), "wo": w_o[l].astype(BF16),
        "gffn": _row(norm_ffn[l]),
        "wr": jnp.pad(wr, ((0, 0), (0, LANE - wr.shape[1]))).astype(BF16), "br": _row(br, LANE),
    }
    we = {"wgate": w_gate, "wup": w_up, "wdown": w_down}
    gfin = _row(norm_final)

    mk_p, mv_p, mk_state, mv_state = _memkv(mem_prompt, _row(norm_mem[l]), w_k[l].astype(BF16), w_v[l].astype(BF16))

    def group(x, conv0, ssd0, s5re0, s5im0, mem_k, mem_v, attn_seqs, attn_tile):
        nb, seq, _ = x.shape
        n_tok = nb * seq
        h1, conv, ssd, s5re, s5im = _mixer(x, conv0, ssd0, s5re0.reshape(nb, S5_LANES),
                                           s5im0.reshape(nb, S5_LANES), wm)
        h2, tn, eid, gates = _attn(h1, mem_k, mem_v, wa, attn_seqs, attn_tile)
        pos, row_token, items = _route_plan(eid.reshape(n_tok, LANE)[:, :2])
        ys = _experts(tn.reshape(n_tok * ROW_TILES, LANE), row_token, items, we)
        y = _combine(ys, pos, gates.reshape(n_tok, LANE), h2.reshape(n_tok, D_MODEL), gfin)
        return (y.reshape(nb, seq, D_MODEL), conv[None], ssd[None],
                s5re.reshape(1, nb, S5_GROUPS, S5_STATE), s5im.reshape(1, nb, S5_GROUPS, S5_STATE))

    zeros = lambda *s: jnp.zeros(s, F32)
    y_p, conv_p, ssd_p, s5re_p, s5im_p = group(
        x_prompt, zeros(bp, CONV_K - 1, CONV_CH), zeros(bp, SSD_HEADS, SSD_HEAD_DIM, SSD_STATE),
        zeros(bp, S5_GROUPS, S5_STATE), zeros(bp, S5_GROUPS, S5_STATE), mk_p, mv_p,
        1, min(512, seq_p))
    y_s, conv_s, ssd_s, s5re_s, s5im_s = group(
        x_sample, cache_conv.reshape(cache_conv.shape[1:]), state_ssd.reshape(state_ssd.shape[1:]),
        state_s5_re.reshape(state_s5_re.shape[1:]), state_s5_im.reshape(state_s5_im.shape[1:]),
        cache_mem_k, cache_mem_v, SAMPLE_ATTN_SEQS, seq_s)

    return (y_p, y_s, conv_p, ssd_p, s5re_p, s5im_p, mk_state, mv_state, conv_s, ssd_s, s5re_s, s5im_s)
```

```python
import functools
import math

import jax
import jax.numpy as jnp
from jax import lax
from jax.experimental import pallas as pl
from jax.experimental.pallas import tpu as pltpu

F32 = jnp.float32
BF16 = jnp.bfloat16
EPS = 1e-6

D_MODEL = 1024
CHUNK = 64
SEQ_TILE = 8
SSD_W = 512
SSD_HEAD_DIM = 64
SSD_HEADS = 8
SSD_GROUPS = 2
SSD_STATE = 128
CONV_K = 4
CONV_CH = 1024
S5_W = 512
S5_GROUPS = 32
S5_GROUP_CH = 16
S5_STATE = 64
S5_LANES = S5_GROUPS * S5_STATE
MEM_LEN = 256
X_HEADS = 4
X_HEAD_DIM = 256
N_EXPERT_GROUPS = 4
EXPERTS_PER_GROUP = 8
N_EXPERTS = 32
EXPERT_FF = 256
LANE = 128
PAD_ROWS = 8
S5_PITCH = CHUNK + PAD_ROWS
MOE_TILE = 256
COMBINE_TILE = 256
SAMPLE_ATTN_SEQS = 4
ROW_TILES = D_MODEL // LANE
KEY_SHIFT = 20
VMEM_LIMIT = 56 * 1024 * 1024


def _rms(x, g):
    return x * lax.rsqrt(jnp.mean(x * x, axis=-1, keepdims=True) + EPS) * g


def _dot(a, b):
    return jnp.dot(a, b, preferred_element_type=F32)


def _dot_nt(a, b):
    return lax.dot_general(a, b, (((1,), (1,)), ((), ())), preferred_element_type=F32)


def _dot_tn(a, b):
    return lax.dot_general(a, b, (((0,), (0,)), ((), ())), preferred_element_type=F32)


def _const_spec(shape):
    nd = len(shape)
    return pl.BlockSpec(shape, lambda *_: (0,) * nd)


def _memkv_kernel(m_ref, g_ref, wk_ref, wv_ref, k_ref, v_ref, k5_ref, v5_ref):
    mn = _rms(m_ref[0], g_ref[...]).astype(BF16)
    k = _dot(mn, wk_ref[...])
    v = _dot(mn, wv_ref[...])
    k_ref[0] = k
    v_ref[0] = v
    for hd in range(X_HEADS):
        k5_ref[0, 0, :, hd, :] = k[:, hd * X_HEAD_DIM:(hd + 1) * X_HEAD_DIM]
        v5_ref[0, 0, :, hd, :] = v[:, hd * X_HEAD_DIM:(hd + 1) * X_HEAD_DIM]


def _memkv(mem, g, wk, wv):
    nb = mem.shape[0]
    flat_spec = pl.BlockSpec((1, MEM_LEN, D_MODEL), lambda i: (i, 0, 0))
    head_spec = pl.BlockSpec((1, 1, MEM_LEN, X_HEADS, X_HEAD_DIM), lambda i: (0, i, 0, 0, 0))
    flat_shape = jax.ShapeDtypeStruct((nb, MEM_LEN, D_MODEL), F32)
    head_shape = jax.ShapeDtypeStruct((1, nb, MEM_LEN, X_HEADS, X_HEAD_DIM), F32)
    return pl.pallas_call(
        _memkv_kernel,
        grid=(nb,),
        in_specs=[flat_spec, _const_spec((1, D_MODEL)), _const_spec((D_MODEL, D_MODEL)),
                  _const_spec((D_MODEL, D_MODEL))],
        out_specs=[flat_spec, flat_spec, head_spec, head_spec],
        out_shape=[flat_shape, flat_shape, head_shape, head_shape],
        compiler_params=pltpu.CompilerParams(dimension_semantics=("arbitrary",), vmem_limit_bytes=VMEM_LIMIT),
        name="memkv",
    )(mem, g, wk, wv)


def _softplus(x):
    return jnp.maximum(x, 0.0) + jnp.log1p(jnp.exp(-jnp.abs(x)))


def _mixer_kernel(n_chunks, x_ref, conv0_ref, ssd0_ref, s5re0_ref, s5im0_ref,
                  gmix_ref, wz_ref, wxbc_ref, wdt_ref, wu_ref, convw_ref, convb_ref,
                  dtb_ref, aneg_ref, eexp_ref, dskip_ref, ssdn_ref,
                  lamre_ref, lamim_ref, wb_ref, wc_ref, s5d_ref, wglu_ref, wout_ref,
                  h_ref, conv_ref, ssd_ref, s5re_ref, s5im_ref,
                  xn_ref, xpad_ref, xc_ref, dte_ref, cse_ref, crow_ref, st_ref, y_ref, u_ref,
                  bure_ref, buim_ref, mix_ref):
    c = pl.program_id(1)
    rows = SEQ_TILE * CHUNK

    @pl.when(c == 0)
    def _():
        xpad_ref[:, PAD_ROWS - (CONV_K - 1):PAD_ROWS, :] = conv0_ref[...]
        s5re_ref[...] = s5re0_ref[...]
        s5im_ref[...] = s5im0_ref[...]

    x = x_ref[...].reshape(rows, D_MODEL)
    xn_ref[...] = _rms(x, gmix_ref[...]).astype(BF16)

    xpad_ref[:, PAD_ROWS:, :] = _dot(xn_ref[...], wxbc_ref[...]).reshape(SEQ_TILE, CHUNK, CONV_CH)
    acc = convb_ref[...].reshape(1, 1, CONV_CH)
    for k in range(CONV_K):
        lo = PAD_ROWS - (CONV_K - 1) + k
        acc = acc + convw_ref[k:k + 1, :].reshape(1, 1, CONV_CH) * xpad_ref[:, lo:lo + CHUNK, :]
    xc_ref[...] = (acc * jax.nn.sigmoid(acc)).reshape(rows, CONV_CH)
    hist = xpad_ref[:, PAD_ROWS + CHUNK - (CONV_K - 1):, :]
    conv_ref[...] = hist
    xpad_ref[:, PAD_ROWS - (CONV_K - 1):PAD_ROWS, :] = hist

    dt = _softplus(_dot(xn_ref[...], wdt_ref[...]) + dtb_ref[...])
    a = dt * aneg_ref[...]
    tpos = lax.broadcasted_iota(jnp.int32, (rows, LANE), 0) % CHUNK
    sh = 1
    while sh < CHUNK:
        a = a + jnp.where(tpos >= sh, pltpu.roll(a, sh, axis=0), 0.0)
        sh *= 2

    def expand_heads(v):
        hi = v.astype(BF16)
        r1 = v - hi.astype(F32)
        mid = r1.astype(BF16)
        lo = (r1 - mid.astype(F32)).astype(BF16)
        e = eexp_ref[...]
        return _dot(hi, e) + _dot(mid, e) + _dot(lo, e)

    dte_ref[...] = expand_heads(dt)
    cse_ref[...] = expand_heads(a)
    for b in range(SEQ_TILE):
        at = a[b * CHUNK:(b + 1) * CHUNK, :].T
        crow_ref[pl.ds(b, 1), :] = jnp.concatenate([at[h:h + 1, :] for h in range(SSD_HEADS)], axis=1)

    @pl.when(c == 0)
    def _():
        for b in range(SEQ_TILE):
            for h in range(SSD_HEADS):
                st_ref[b, :, h * SSD_HEAD_DIM:(h + 1) * SSD_HEAD_DIM] = ssd0_ref[b, h].T

    gw = SSD_W // SSD_GROUPS
    heads_per_group = SSD_HEADS // SSD_GROUPS
    tri = (lax.broadcasted_iota(jnp.int32, (CHUNK, gw), 0)
           >= lax.broadcasted_iota(jnp.int32, (CHUNK, gw), 1) % CHUNK)
    same_head = ((lax.broadcasted_iota(jnp.int32, (gw, gw), 0) // SSD_HEAD_DIM)
                 == (lax.broadcasted_iota(jnp.int32, (gw, gw), 1) // SSD_HEAD_DIM))

    def seq_body(b, carry):
        r0 = pl.multiple_of(b * CHUNK, CHUNK)
        rs = pl.ds(r0, CHUNK)
        for g in range(SSD_GROUPS):
            ls = pl.ds(g * gw, gw)
            cse = cse_ref[rs, ls]
            cs_last = cse_ref[pl.ds(r0 + CHUNK - 1, 1), ls]
            decay = jnp.exp(jnp.where(tri, cse - crow_ref[pl.ds(b, 1), ls], -jnp.inf))
            xs = xc_ref[rs, ls]
            xdt = xs * dte_ref[rs, ls]
            b_bf = xc_ref[rs, pl.ds(SSD_W + g * SSD_STATE, SSD_STATE)].astype(BF16)
            c_bf = xc_ref[rs, pl.ds(SSD_W + SSD_GROUPS * SSD_STATE + g * SSD_STATE, SSD_STATE)].astype(BF16)
            gram = _dot_nt(c_bf, jnp.concatenate([b_bf] * heads_per_group, axis=0))
            xbd = jnp.where(same_head, jnp.concatenate([xdt] * heads_per_group, axis=0), 0.0).astype(BF16)
            y = _dot((gram * decay).astype(BF16), xbd)
            st = st_ref[b, :, ls]
            y = y + _dot(c_bf, st.astype(BF16)) * jnp.exp(cse)
            y = y + dskip_ref[:, ls] * xs
            y_ref[rs, ls] = y
            upd = _dot_tn(b_bf, (xdt * jnp.exp(cs_last - cse)).astype(BF16))
            st_ref[b, :, ls] = st * jnp.exp(cs_last) + upd
        return carry

    lax.fori_loop(0, SEQ_TILE, seq_body, 0)

    @pl.when(c == n_chunks - 1)
    def _():
        for b in range(SEQ_TILE):
            for h in range(SSD_HEADS):
                ssd_ref[b, h] = st_ref[b, :, h * SSD_HEAD_DIM:(h + 1) * SSD_HEAD_DIM].T

    z = _dot(xn_ref[...], wz_ref[...])
    y = y_ref[...] * (z * jax.nn.sigmoid(z))
    mix_ref[:, 0:SSD_W] = _rms(y, ssdn_ref[...]).astype(BF16)

    u = _dot(xn_ref[...], wu_ref[...])
    u_ref[...] = u
    half_ch = S5_W // 2
    half_st = S5_LANES // 2
    half_tiles = half_st // LANE
    for hf in range(2):
        bu = _dot(u[:, hf * half_ch:(hf + 1) * half_ch].astype(BF16), wb_ref[hf])
        for k in range(half_tiles):
            for b in range(SEQ_TILE):
                dst = pl.ds(b * S5_PITCH, CHUNK)
                src = slice(b * CHUNK, (b + 1) * CHUNK)
                bure_ref[hf * half_tiles + k, dst, :] = bu[src, k * LANE:(k + 1) * LANE]
                buim_ref[hf * half_tiles + k, dst, :] = bu[src, half_st + k * LANE:half_st + (k + 1) * LANE]
    scan_tiles = 4
    for j in range(S5_LANES // LANE // scan_tiles):
        tiles = [j * scan_tiles + k for k in range(scan_tiles)]
        lr = [lamre_ref[:, pl.ds(k * LANE, LANE)] for k in tiles]
        li = [lamim_ref[:, pl.ds(k * LANE, LANE)] for k in tiles]
        sr = [s5re_ref[:, pl.ds(k * LANE, LANE)] for k in tiles]
        si = [s5im_ref[:, pl.ds(k * LANE, LANE)] for k in tiles]
        for t in range(CHUNK):
            ts = pl.ds(t, SEQ_TILE, stride=S5_PITCH)
            for q, k in enumerate(tiles):
                nr = lr[q] * sr[q] - li[q] * si[q] + bure_ref[k, ts, :]
                ni = lr[q] * si[q] + li[q] * sr[q] + buim_ref[k, ts, :]
                sr[q], si[q] = nr, ni
                bure_ref[k, ts, :] = nr
                buim_ref[k, ts, :] = ni
        for q, k in enumerate(tiles):
            s5re_ref[:, pl.ds(k * LANE, LANE)] = sr[q]
            s5im_ref[:, pl.ds(k * LANE, LANE)] = si[q]
    def seq_rows(ref, k):
        return jnp.concatenate([ref[k, pl.ds(b * S5_PITCH, CHUNK), :] for b in range(SEQ_TILE)], axis=0)

    ys = []
    for j in range(4):
        s_re = jnp.concatenate([seq_rows(bure_ref, 4 * j + k) for k in range(4)], axis=1)
        s_im = jnp.concatenate([seq_rows(buim_ref, 4 * j + k) for k in range(4)], axis=1)
        ys.append(_dot(s_re.astype(BF16), wc_ref[j, 0]) + _dot(s_im.astype(BF16), wc_ref[j, 1]))
    y5 = jnp.concatenate(ys, axis=1) + s5d_ref[...] * u_ref[...]
    y5 = jax.nn.gelu(y5)
    y5 = y5 * jax.nn.sigmoid(_dot(y5.astype(BF16), wglu_ref[...]))
    mix_ref[:, SSD_W:] = y5.astype(BF16)

    h = x_ref[...].reshape(rows, D_MODEL) + _dot(mix_ref[...], wout_ref[...])
    h_ref[...] = h.reshape(SEQ_TILE, CHUNK, D_MODEL)


def _mixer(x, conv0, ssd0, s5re0, s5im0, w):
    nb, seq, _ = x.shape
    grid = (nb // SEQ_TILE, seq // CHUNK)
    rows = SEQ_TILE * CHUNK
    weights = [w["gmix"], w["wz"], w["wxbc"], w["wdt"], w["wu"], w["convw"], w["convb"],
               w["dtb"], w["aneg"], w["eexp"], w["dskip"], w["ssdn"],
               w["lamre"], w["lamim"], w["wb"], w["wc"], w["s5d"], w["wglu"], w["wout"]]
    state_specs = [pl.BlockSpec((SEQ_TILE, CONV_K - 1, CONV_CH), lambda i, c: (i, 0, 0)),
                   pl.BlockSpec((SEQ_TILE, SSD_HEADS, SSD_HEAD_DIM, SSD_STATE), lambda i, c: (i, 0, 0, 0)),
                   pl.BlockSpec((SEQ_TILE, S5_LANES), lambda i, c: (i, 0)),
                   pl.BlockSpec((SEQ_TILE, S5_LANES), lambda i, c: (i, 0))]
    x_spec = pl.BlockSpec((SEQ_TILE, CHUNK, D_MODEL), lambda i, c: (i, c, 0))
    return pl.pallas_call(
        functools.partial(_mixer_kernel, grid[1]),
        grid=grid,
        in_specs=[x_spec] + state_specs + [_const_spec(a.shape) for a in weights],
        out_specs=[x_spec] + state_specs,
        out_shape=[jax.ShapeDtypeStruct(x.shape, F32),
                   jax.ShapeDtypeStruct(conv0.shape, F32), jax.ShapeDtypeStruct(ssd0.shape, F32),
                   jax.ShapeDtypeStruct(s5re0.shape, F32), jax.ShapeDtypeStruct(s5im0.shape, F32)],
        scratch_shapes=[
            pltpu.VMEM((rows, D_MODEL), BF16),
            pltpu.VMEM((SEQ_TILE, PAD_ROWS + CHUNK, CONV_CH), F32),
            pltpu.VMEM((rows, CONV_CH), F32),
            pltpu.VMEM((rows, SSD_W), F32),
            pltpu.VMEM((rows, SSD_W), F32),
            pltpu.VMEM((SEQ_TILE, SSD_W), F32),
            pltpu.VMEM((SEQ_TILE, SSD_STATE, SSD_W), F32),
            pltpu.VMEM((rows, SSD_W), F32),
            pltpu.VMEM((rows, S5_W), F32),
            pltpu.VMEM((S5_LANES // LANE, SEQ_TILE * S5_PITCH, LANE), F32),
            pltpu.VMEM((S5_LANES // LANE, SEQ_TILE * S5_PITCH, LANE), F32),
            pltpu.VMEM((rows, D_MODEL), BF16),
        ],
        compiler_params=pltpu.CompilerParams(dimension_semantics=("arbitrary", "arbitrary"),
                                             vmem_limit_bytes=VMEM_LIMIT),
        name="mixer",
    )(x, conv0, ssd0, s5re0, s5im0, *weights)


def _attn_kernel(n_seq, tile, heads_axis, h_ref, k_ref, v_ref, gx_ref, wq_ref, wo_ref, gffn_ref, wr_ref, br_ref,
                 h2_ref, tn_ref, eid_ref, gate_ref, o_ref, kv_ref):
    rows = n_seq * tile
    h1 = h_ref[...].reshape(rows, D_MODEL)
    xn = _rms(h1, gx_ref[...]).astype(BF16)
    q = _dot(xn, wq_ref[...])
    scale = X_HEAD_DIM ** -0.5
    for sq in range(n_seq):
        rs = slice(sq * tile, (sq + 1) * tile)
        for hd in range(X_HEADS):
            ls = slice(hd * X_HEAD_DIM, (hd + 1) * X_HEAD_DIM)
            if heads_axis:
                kv_ref[0] = k_ref[0, sq, :, hd, :]
                kv_ref[1] = v_ref[0, sq, :, hd, :]
                kh = kv_ref[0].astype(BF16)
                vh = kv_ref[1].astype(BF16)
            else:
                kh = k_ref[sq, :, ls].astype(BF16)
                vh = v_ref[sq, :, ls].astype(BF16)
            s = _dot_nt(q[rs, ls].astype(BF16), kh) * scale
            s = s - jnp.max(s, axis=-1, keepdims=True)
            p = jnp.exp(s)
            p = p / jnp.sum(p, axis=-1, keepdims=True)
            o_ref[rs, ls] = _dot(p.astype(BF16), vh).astype(BF16)
    h2 = h1 + _dot(o_ref[...], wo_ref[...])
    h2_ref[...] = h2.reshape(n_seq, tile, D_MODEL)

    tn = _rms(h2, gffn_ref[...]).astype(BF16)
    tn32 = tn.astype(F32)
    for sq in range(n_seq):
        for sb in range(ROW_TILES):
            tn_ref[sq, pl.ds(sb, tile, stride=ROW_TILES), :] = tn32[sq * tile:(sq + 1) * tile,
                                                                  sb * LANE:(sb + 1) * LANE]
    logits = _dot(tn, wr_ref[...]) + br_ref[...]
    lane = lax.broadcasted_iota(jnp.int32, logits.shape, 1)
    big = jnp.int32(2 ** 30)
    neg = -jnp.inf
    is_g = (lane >= N_EXPERTS) & (lane < N_EXPERTS + N_EXPERT_GROUPS)
    gl = jnp.where(is_g, logits, neg)
    gmax = jnp.max(gl, axis=-1, keepdims=True)
    g_idx = jnp.min(jnp.where(gl == gmax, lane - N_EXPERTS, big), axis=-1, keepdims=True)
    g_prob = 1.0 / jnp.sum(jnp.exp(gl - gmax), axis=-1, keepdims=True)
    in_grp = (lane < N_EXPERTS) & ((lane // EXPERTS_PER_GROUP) == g_idx)
    el = jnp.where(in_grp, logits, neg)
    m1 = jnp.max(el, axis=-1, keepdims=True)
    i1 = jnp.min(jnp.where(el == m1, lane, big), axis=-1, keepdims=True)
    el2 = jnp.where(lane == i1, neg, el)
    m2 = jnp.max(el2, axis=-1, keepdims=True)
    i2 = jnp.min(jnp.where(el2 == m2, lane, big), axis=-1, keepdims=True)
    e2 = jnp.exp(m2 - m1)
    den = 1.0 + e2
    eid = jnp.where(lane == 0, i1, jnp.where(lane == 1, i2, 0))
    gate = jnp.where(lane == 0, (1.0 / den) * g_prob, jnp.where(lane == 1, (e2 / den) * g_prob, 0.0))
    eid_ref[...] = eid.reshape(n_seq, tile, LANE)
    gate_ref[...] = gate.reshape(n_seq, tile, LANE)


def _attn(h1, mem_k, mem_v, w, n_seq, tile):
    nb, seq, _ = h1.shape
    heads_axis = mem_k.ndim == 5
    row_spec = pl.BlockSpec((n_seq, tile, D_MODEL), lambda b, i: (b, i, 0))
    if heads_axis:
        kv_spec = pl.BlockSpec((1, n_seq, MEM_LEN, X_HEADS, X_HEAD_DIM), lambda b, i: (0, b, 0, 0, 0))
    else:
        kv_spec = pl.BlockSpec((n_seq, MEM_LEN, D_MODEL), lambda b, i: (b, 0, 0))
    lane_spec = pl.BlockSpec((n_seq, tile, LANE), lambda b, i: (b, i, 0))
    weights = [w["gx"], w["wq"], w["wo"], w["gffn"], w["wr"], w["br"]]
    return pl.pallas_call(
        functools.partial(_attn_kernel, n_seq, tile, heads_axis),
        grid=(nb // n_seq, seq // tile),
        in_specs=[row_spec, kv_spec, kv_spec] + [_const_spec(a.shape) for a in weights],
        out_specs=[row_spec, pl.BlockSpec((n_seq, tile * ROW_TILES, LANE), lambda b, i: (b, i, 0)),
                   lane_spec, lane_spec],
        out_shape=[jax.ShapeDtypeStruct(h1.shape, F32),
                   jax.ShapeDtypeStruct((nb, seq * ROW_TILES, LANE), F32),
                   jax.ShapeDtypeStruct((nb, seq, LANE), jnp.int32),
                   jax.ShapeDtypeStruct((nb, seq, LANE), F32)],
        scratch_shapes=[pltpu.VMEM((n_seq * tile, D_MODEL), BF16),
                        pltpu.VMEM((2, MEM_LEN, X_HEAD_DIM), F32)],
        compiler_params=pltpu.CompilerParams(dimension_semantics=("arbitrary", "arbitrary"),
                                             vmem_limit_bytes=VMEM_LIMIT),
        name="attn_router",
    )(h1, mem_k, mem_v, *weights)


def _route_plan(eid):
    n_tok = eid.shape[0]
    n_asg = 2 * n_tok
    assert n_asg % MOE_TILE == 0 and n_asg < (1 << KEY_SHIFT)
    n_tiles = n_asg // MOE_TILE
    n_items = n_tiles + N_EXPERTS
    i32 = jnp.int32
    e_flat = eid.reshape(n_asg)
    a_idx = jnp.arange(n_asg, dtype=i32)
    order = lax.sort(e_flat * (1 << KEY_SHIFT) + a_idx) & ((1 << KEY_SHIFT) - 1)
    row_token = (order // 2).reshape(n_tiles, 1, MOE_TILE)
    onehot = (e_flat[:, None] == jnp.arange(N_EXPERTS, dtype=i32)[None, :]).astype(i32)
    csum = jnp.cumsum(onehot, axis=0)
    rank = jnp.sum(csum * onehot, axis=1) - 1
    counts = csum[-1]
    seg_end = jnp.cumsum(counts)
    seg_start = seg_end - counts
    pos = jnp.sum(onehot * seg_start[None, :], axis=1) + rank
    first_tile = seg_start // MOE_TILE
    last_tile = (seg_end - 1) // MOE_TILE
    items_e = jnp.where(counts > 0, last_tile - first_tile + 1, 0)
    it_end = jnp.cumsum(items_e)
    it_start = it_end - items_e
    w = jnp.arange(n_items, dtype=i32)
    wc = jnp.minimum(w, it_end[-1] - 1)
    it_expert = jnp.sum((wc[:, None] >= it_end[None, :]).astype(i32), axis=1)
    it_onehot = (it_expert[:, None] == jnp.arange(N_EXPERTS, dtype=i32)[None, :]).astype(i32)

    def of_item(table):
        return jnp.sum(it_onehot * table[None, :], axis=1)

    it_tile = of_item(first_tile) + (wc - of_item(it_start))
    it_lo = jnp.clip(of_item(seg_start) - it_tile * MOE_TILE, 0, MOE_TILE)
    it_hi = jnp.clip(of_item(seg_end) - it_tile * MOE_TILE, 0, MOE_TILE)
    it_valid = (w < it_end[-1]).astype(i32)
    items = tuple(a.astype(i32) for a in (it_tile, it_expert, it_lo, it_hi, it_valid))
    return pos.reshape(n_tok, 2), row_token, items


def _token_copy(src_hbm, src_row, dst, dst_row, sem):
    return pltpu.make_async_copy(src_hbm.at[pl.ds(pl.multiple_of(src_row * ROW_TILES, ROW_TILES), ROW_TILES), :],
                                 dst.at[pl.ds(pl.multiple_of(dst_row * ROW_TILES, ROW_TILES), ROW_TILES), :], sem)


def _experts_kernel(n_tiles, n_items, tile_ref, exp_ref, lo_ref, hi_ref, valid_ref, rt_cur_ref, rt_nxt_ref, tn_hbm,
                    wg_ref, wu_ref, wd_ref, y_ref, xbuf, xb_ref, wgu_bf, wd_bf, sem):
    w = pl.program_id(0)
    j = tile_ref[w]
    lo = lo_ref[w]
    hi = hi_ref[w]
    valid = valid_ref[w] == 1
    first = jnp.logical_and(valid, lo == 0)
    slot = j % 2

    @pl.when(jnp.logical_or(w == 0, exp_ref[w] != exp_ref[jnp.maximum(w - 1, 0)]))
    def _():
        wgu_bf[:, :EXPERT_FF] = wg_ref[0, 0].astype(BF16)
        wgu_bf[:, EXPERT_FF:] = wu_ref[0, 0].astype(BF16)
        wd_bf[...] = wd_ref[0, 0].astype(BF16)

    def wait_gather(s):
        def body(r, carry):
            _token_copy(tn_hbm, 0, xbuf.at[s], r, sem.at[s]).wait()
            return carry
        lax.fori_loop(0, MOE_TILE, body, 0, unroll=8)

    def expert_rows():
        gu = _dot(xb_ref[...], wgu_bf[...])
        gate = gu[:, :EXPERT_FF]
        hid = (gate * jax.nn.sigmoid(gate)) * gu[:, EXPERT_FF:]
        return _dot(hid.astype(BF16), wd_bf[...])

    @pl.when(w == 0)
    def _():
        def body(r, carry):
            _token_copy(tn_hbm, rt_cur_ref[0, 0, r], xbuf.at[0], r, sem.at[0]).start()
            return carry
        lax.fori_loop(0, MOE_TILE, body, 0, unroll=8)

    @pl.when(first)
    def _():
        wait_gather(slot)
        for sb in range(ROW_TILES):
            xb_ref[:, sb * LANE:(sb + 1) * LANE] = xbuf[slot, pl.ds(sb, MOE_TILE, stride=ROW_TILES), :].astype(BF16)
        for r in range(MOE_TILE):
            _token_copy(tn_hbm, rt_nxt_ref[0, 0, r], xbuf.at[1 - slot], r, sem.at[1 - slot]).start()
        y = expert_rows()
        for sb in range(ROW_TILES):
            y_ref[pl.ds(sb, MOE_TILE, stride=ROW_TILES), :] = y[:, sb * LANE:(sb + 1) * LANE]

    @pl.when(jnp.logical_and(valid, lo != 0))
    def _():
        y = expert_rows()
        row = lax.broadcasted_iota(jnp.int32, (MOE_TILE, LANE), 0)
        keep = jnp.logical_and(row >= lo, row < hi)
        for sb in range(ROW_TILES):
            rows = pl.ds(sb, MOE_TILE, stride=ROW_TILES)
            y_ref[rows, :] = jnp.where(keep, y[:, sb * LANE:(sb + 1) * LANE], y_ref[rows, :])

    @pl.when(w == n_items - 1)
    def _():
        wait_gather(n_tiles % 2)


def _experts(tn, row_token, items, w):
    n_tiles = row_token.shape[0]
    n_items = items[0].shape[0]

    def smem_tile(imap):
        return pl.BlockSpec((1, 1, MOE_TILE), imap, memory_space=pltpu.SMEM)

    grid_spec = pltpu.PrefetchScalarGridSpec(
        num_scalar_prefetch=5,
        grid=(n_items,),
        in_specs=[smem_tile(lambda i, tl, ex, lo, hi, va: (tl[i], 0, 0)),
                  smem_tile(lambda i, tl, ex, lo, hi, va: (jnp.minimum(tl[i] + 1, n_tiles - 1), 0, 0)),
                  pl.BlockSpec(memory_space=pl.ANY),
                  pl.BlockSpec((1, 1, D_MODEL, EXPERT_FF), lambda i, tl, ex, lo, hi, va: (0, ex[i], 0, 0)),
                  pl.BlockSpec((1, 1, D_MODEL, EXPERT_FF), lambda i, tl, ex, lo, hi, va: (0, ex[i], 0, 0)),
                  pl.BlockSpec((1, 1, EXPERT_FF, D_MODEL), lambda i, tl, ex, lo, hi, va: (0, ex[i], 0, 0))],
        out_specs=pl.BlockSpec((MOE_TILE * ROW_TILES, LANE), lambda i, tl, ex, lo, hi, va: (tl[i], 0)),
        scratch_shapes=[pltpu.VMEM((2, MOE_TILE * ROW_TILES, LANE), F32),
                        pltpu.VMEM((MOE_TILE, D_MODEL), BF16),
                        pltpu.VMEM((D_MODEL, 2 * EXPERT_FF), BF16),
                        pltpu.VMEM((EXPERT_FF, D_MODEL), BF16),
                        pltpu.SemaphoreType.DMA((2,))],
    )
    return pl.pallas_call(
        functools.partial(_experts_kernel, n_tiles, n_items),
        grid_spec=grid_spec,
        out_shape=jax.ShapeDtypeStruct((n_tiles * MOE_TILE * ROW_TILES, LANE), F32),
        compiler_params=pltpu.CompilerParams(dimension_semantics=("arbitrary",), vmem_limit_bytes=VMEM_LIMIT),
        name="experts",
    )(*items, row_token, row_token, tn, w["wgate"], w["wup"], w["wdown"])


def _combine_kernel(n_steps, pos_cur_ref, pos_nxt_ref, ys_hbm, gate_ref, h2_ref, gfin_ref, out_ref, ybuf, sem):
    i = pl.program_id(0)
    slot = i % 2

    def wait_gather(s):
        def body(r, carry):
            for k in range(2):
                _token_copy(ys_hbm, 0, ybuf.at[s, k], r, sem.at[s]).wait()
            return carry
        lax.fori_loop(0, COMBINE_TILE, body, 0, unroll=8)

    @pl.when(i == 0)
    def _():
        def body(r, carry):
            for k in range(2):
                _token_copy(ys_hbm, pos_cur_ref[0, 0, k * COMBINE_TILE + r], ybuf.at[0, k], r, sem.at[0]).start()
            return carry
        lax.fori_loop(0, COMBINE_TILE, body, 0, unroll=8)

    wait_gather(slot)
    for r in range(COMBINE_TILE):
        for k in range(2):
            _token_copy(ys_hbm, pos_nxt_ref[0, 0, k * COMBINE_TILE + r], ybuf.at[1 - slot, k], r,
                        sem.at[1 - slot]).start()

    gates = gate_ref[...]
    g0 = gates[:, 0:1]
    g1 = gates[:, 1:2]
    ssq = jnp.zeros((COMBINE_TILE, 1), F32)
    for sb in range(ROW_TILES):
        rows = pl.ds(sb, COMBINE_TILE, stride=ROW_TILES)
        cols = pl.ds(sb * LANE, LANE)
        v = h2_ref[:, cols] + (ybuf[slot, 0, rows, :] * g0 + ybuf[slot, 1, rows, :] * g1)
        ssq = ssq + jnp.sum(v * v, axis=-1, keepdims=True)
        out_ref[:, cols] = v
    out_ref[...] = out_ref[...] * lax.rsqrt(ssq * (1.0 / D_MODEL) + EPS) * gfin_ref[...]

    @pl.when(i == n_steps - 1)
    def _():
        wait_gather(1 - slot)


def _combine(ys, pos, gates, h2, gfin):
    n_tok = h2.shape[0]
    n_tiles = n_tok // COMBINE_TILE
    pos_tiles = pos.reshape(n_tiles, COMBINE_TILE, 2).transpose(0, 2, 1).reshape(n_tiles, 1, 2 * COMBINE_TILE)

    def smem_tile(imap):
        return pl.BlockSpec((1, 1, 2 * COMBINE_TILE), imap, memory_space=pltpu.SMEM)

    row_spec = pl.BlockSpec((COMBINE_TILE, D_MODEL), lambda i: (i, 0))
    return pl.pallas_call(
        functools.partial(_combine_kernel, n_tiles),
        grid=(n_tiles,),
        in_specs=[smem_tile(lambda i: (i, 0, 0)),
                  smem_tile(lambda i: (jnp.minimum(i + 1, n_tiles - 1), 0, 0)),
                  pl.BlockSpec(memory_space=pl.ANY),
                  pl.BlockSpec((COMBINE_TILE, LANE), lambda i: (i, 0)),
                  row_spec, _const_spec((1, D_MODEL))],
        out_specs=row_spec,
        out_shape=jax.ShapeDtypeStruct((n_tok, D_MODEL), F32),
        scratch_shapes=[pltpu.VMEM((2, 2, COMBINE_TILE * ROW_TILES, LANE), F32), pltpu.SemaphoreType.DMA((2,))],
        compiler_params=pltpu.CompilerParams(dimension_semantics=("arbitrary",), vmem_limit_bytes=VMEM_LIMIT),
        name="combine",
    )(pos_tiles, pos_tiles, ys, gates, h2, gfin)


def _row(v, width=None):
    v = v.astype(F32).reshape(1, -1)
    if width is not None and v.shape[1] < width:
        v = jnp.pad(v, ((0, 0), (0, width - v.shape[1])))
    return v


def _s5_params(a_re, a_im, log_dt, b_re, b_im, c_re, c_im):
    dt = jnp.exp(log_dt)[:, None]
    mag = jnp.exp(a_re * dt)
    lam_re, lam_im = mag * jnp.cos(a_im * dt), mag * jnp.sin(a_im * dt)
    den = a_re * a_re + a_im * a_im
    f_re = ((lam_re - 1.0) * a_re + lam_im * a_im) / den
    f_im = (lam_im * a_re - (lam_re - 1.0) * a_im) / den
    bb_re = f_re[..., None] * b_re - f_im[..., None] * b_im
    bb_im = f_re[..., None] * b_im + f_im[..., None] * b_re
    half = S5_GROUPS // 2

    def in_proj(bb):
        bb = bb.reshape(2, half, S5_STATE, S5_GROUP_CH)
        eye = jnp.eye(half, dtype=F32)
        m = jnp.einsum("hgnc,gk->hgckn", bb, eye)
        return m.reshape(2, half * S5_GROUP_CH, half * S5_STATE)

    wb = jnp.concatenate([in_proj(bb_re), in_proj(bb_im)], axis=2).astype(BF16)

    def out_proj(cc):
        q = S5_GROUPS // 4
        cc = cc.reshape(4, q, S5_GROUP_CH, S5_STATE)
        eye = jnp.eye(q, dtype=F32)
        m = jnp.einsum("qgcn,gk->qgnkc", cc, eye)
        return m.reshape(4, q * S5_STATE, q * S5_GROUP_CH)

    wc = jnp.stack([out_proj(c_re), -out_proj(c_im)], axis=1).astype(BF16)
    return lam_re.reshape(1, S5_LANES), lam_im.reshape(1, S5_LANES), wb, wc


def kernel(x_prompt, x_sample, cache_conv, state_ssd, state_s5_re, state_s5_im, cache_mem_k, cache_mem_v,
           mem_prompt, norm_mix, w_in, conv_w, conv_b, dt_bias, a_log, d_skip, ssd_norm, s5_a_re, s5_a_im,
           s5_log_dt, s5_b_re, s5_b_im, s5_c_re, s5_c_im, s5_d, s5_w_glu, w_out, norm_x, norm_mem, w_q,
           w_k, w_v, w_o, norm_ffn, w_router_group, b_router_group, w_router_expert, b_router_expert,
           w_gate, w_up, w_down, norm_final):
    depth = norm_mix.shape[0]
    assert depth == 1
    l = 0
    bp, seq_p, _ = x_prompt.shape
    bs, seq_s, _ = x_sample.shape

    o1 = SSD_W
    o2 = o1 + CONV_CH
    o3 = o2 + SSD_HEADS
    lam_re, lam_im, wb, wc = _s5_params(s5_a_re[l], s5_a_im[l], s5_log_dt[l], s5_b_re[l], s5_b_im[l],
                                        s5_c_re[l], s5_c_im[l])
    wm = {
        "gmix": _row(norm_mix[l]),
        "wz": w_in[l][:, :o1].astype(BF16),
        "wxbc": w_in[l][:, o1:o2].astype(BF16),
        "wdt": jnp.pad(w_in[l][:, o2:o3], ((0, 0), (0, LANE - SSD_HEADS))).astype(BF16),
        "wu": w_in[l][:, o3:].astype(BF16),
        "convw": conv_w[l].astype(F32), "convb": _row(conv_b[l]),
        "dtb": _row(dt_bias[l], LANE), "aneg": _row(-jnp.exp(a_log[l].astype(F32)), LANE),
        "eexp": (jnp.arange(LANE, dtype=jnp.int32)[:, None]
                 == jnp.arange(SSD_W, dtype=jnp.int32)[None, :] // SSD_HEAD_DIM).astype(BF16),
        "dskip": _row(jnp.repeat(d_skip[l].astype(F32), SSD_HEAD_DIM)), "ssdn": _row(ssd_norm[l]),
        "lamre": lam_re, "lamim": lam_im, "wb": wb, "wc": wc,
        "s5d": _row(s5_d[l]), "wglu": s5_w_glu[l].astype(BF16), "wout": w_out[l].astype(BF16),
    }
    wr = jnp.concatenate([w_router_expert[l].reshape(D_MODEL, N_EXPERTS), w_router_group[l]], axis=1)
    br = jnp.concatenate([b_router_expert[l].reshape(N_EXPERTS), b_router_group[l]])
    wa = {
        "gx": _row(norm_x[l]), "wq": w_q[l].astype(BF16), "wo": w_o[l].astype(BF16),
        "gffn": _row(norm_ffn[l]),
        "wr": jnp.pad(wr, ((0, 0), (0, LANE - wr.shape[1]))).astype(BF16), "br": _row(br, LANE),
    }
    we = {"wgate": w_gate, "wup": w_up, "wdown": w_down}
    gfin = _row(norm_final)

    mk_p, mv_p, mk_state, mv_state = _memkv(mem_prompt, _row(norm_mem[l]), w_k[l].astype(BF16), w_v[l].astype(BF16))

    def group(x, conv0, ssd0, s5re0, s5im0, mem_k, mem_v, attn_seqs, attn_tile):
        nb, seq, _ = x.shape
        n_tok = nb * seq
        h1, conv, ssd, s5re, s5im = _mixer(x, conv0, ssd0, s5re0.reshape(nb, S5_LANES),
                                           s5im0.reshape(nb, S5_LANES), wm)
        h2, tn, eid, gates = _attn(h1, mem_k, mem_v, wa, attn_seqs, attn_tile)
        pos, row_token, items = _route_plan(eid.reshape(n_tok, LANE)[:, :2])
        ys = _experts(tn.reshape(n_tok * ROW_TILES, LANE), row_token, items, we)
        y = _combine(ys, pos, gates.reshape(n_tok, LANE), h2.reshape(n_tok, D_MODEL), gfin)
        return (y.reshape(nb, seq, D_MODEL), conv[None], ssd[None],
                s5re.reshape(1, nb, S5_GROUPS, S5_STATE), s5im.reshape(1, nb, S5_GROUPS, S5_STATE))

    zeros = lambda *s: jnp.zeros(s, F32)
    y_p, conv_p, ssd_p, s5re_p, s5im_p = group(
        x_prompt, zeros(bp, CONV_K - 1, CONV_CH), zeros(bp, SSD_HEADS, SSD_HEAD_DIM, SSD_STATE),
        zeros(bp, S5_GROUPS, S5_STATE), zeros(bp, S5_GROUPS, S5_STATE), mk_p, mv_p,
        1, min(512, seq_p))
    y_s, conv_s, ssd_s, s5re_s, s5im_s = group(
        x_sample, cache_conv.reshape(cache_conv.shape[1:]), state_ssd.reshape(state_ssd.shape[1:]),
        state_s5_re.reshape(state_s5_re.shape[1:]), state_s5_im.reshape(state_s5_im.shape[1:]),
        cache_mem_k, cache_mem_v, SAMPLE_ATTN_SEQS, seq_s)

    return (y_p, y_s, conv_p, ssd_p, s5re_p, s5im_p, mk_state, mv_state, conv_s, ssd_s, s5re_s, s5im_s)
```

```python
import functools
import math

import jax
import jax.numpy as jnp
from jax import lax
from jax.experimental import pallas as pl
from jax.experimental.pallas import tpu as pltpu

F32 = jnp.float32
BF16 = jnp.bfloat16
EPS = 1e-6

D_MODEL = 1024
CHUNK = 64
SEQ_TILE = 8
SSD_W = 512
SSD_HEAD_DIM = 64
SSD_HEADS = 8
SSD_GROUPS = 2
SSD_STATE = 128
CONV_K = 4
CONV_CH = 1024
S5_W = 512
S5_GROUPS = 32
S5_GROUP_CH = 16
S5_STATE = 64
S5_LANES = S5_GROUPS * S5_STATE
MEM_LEN = 256
X_HEADS = 4
X_HEAD_DIM = 256
N_EXPERT_GROUPS = 4
EXPERTS_PER_GROUP = 8
N_EXPERTS = 32
EXPERT_FF = 256
LANE = 128
PAD_ROWS = 8
S5_PITCH = CHUNK + PAD_ROWS
S5_SCAN_TILES = 8
MOE_TILE = 256
COMBINE_TILE = 256
SAMPLE_ATTN_SEQS = 4
ROW_TILES = D_MODEL // LANE
KEY_SHIFT = 20
VMEM_LIMIT = 56 * 1024 * 1024


def _rms(x, g):
    return x * lax.rsqrt(jnp.mean(x * x, axis=-1, keepdims=True) + EPS) * g


def _dot(a, b):
    return jnp.dot(a, b, preferred_element_type=F32)


def _dot_nt(a, b):
    return lax.dot_general(a, b, (((1,), (1,)), ((), ())), preferred_element_type=F32)


def _dot_tn(a, b):
    return lax.dot_general(a, b, (((0,), (0,)), ((), ())), preferred_element_type=F32)


def _const_spec(shape):
    nd = len(shape)
    return pl.BlockSpec(shape, lambda *_: (0,) * nd)


def _memkv_kernel(m_ref, g_ref, wk_ref, wv_ref, k_ref, v_ref, k5_ref, v5_ref):
    mn = _rms(m_ref[0], g_ref[...]).astype(BF16)
    k = _dot(mn, wk_ref[...])
    v = _dot(mn, wv_ref[...])
    k_ref[0] = k
    v_ref[0] = v
    for hd in range(X_HEADS):
        k5_ref[0, 0, :, hd, :] = k[:, hd * X_HEAD_DIM:(hd + 1) * X_HEAD_DIM]
        v5_ref[0, 0, :, hd, :] = v[:, hd * X_HEAD_DIM:(hd + 1) * X_HEAD_DIM]


def _memkv(mem, g, wk, wv):
    nb = mem.shape[0]
    flat_spec = pl.BlockSpec((1, MEM_LEN, D_MODEL), lambda i: (i, 0, 0))
    head_spec = pl.BlockSpec((1, 1, MEM_LEN, X_HEADS, X_HEAD_DIM), lambda i: (0, i, 0, 0, 0))
    flat_shape = jax.ShapeDtypeStruct((nb, MEM_LEN, D_MODEL), F32)
    head_shape = jax.ShapeDtypeStruct((1, nb, MEM_LEN, X_HEADS, X_HEAD_DIM), F32)
    return pl.pallas_call(
        _memkv_kernel,
        grid=(nb,),
        in_specs=[flat_spec, _const_spec((1, D_MODEL)), _const_spec((D_MODEL, D_MODEL)),
                  _const_spec((D_MODEL, D_MODEL))],
        out_specs=[flat_spec, flat_spec, head_spec, head_spec],
        out_shape=[flat_shape, flat_shape, head_shape, head_shape],
        compiler_params=pltpu.CompilerParams(dimension_semantics=("arbitrary",), vmem_limit_bytes=VMEM_LIMIT),
        name="memkv",
    )(mem, g, wk, wv)


def _softplus(x):
    return jnp.maximum(x, 0.0) + jnp.log1p(jnp.exp(-jnp.abs(x)))


def _mixer_kernel(n_chunks, x_ref, conv0_ref, ssd0_ref, s5re0_ref, s5im0_ref,
                  gmix_ref, wz_ref, wxbc_ref, wdt_ref, wu_ref, convw_ref, convb_ref,
                  dtb_ref, aneg_ref, eexp_ref, dskip_ref, ssdn_ref,
                  lamre_ref, lamim_ref, wb_ref, wc_ref, s5d_ref, wglu_ref, wout_ref,
                  h_ref, conv_ref, ssd_ref, s5re_ref, s5im_ref,
                  xn_ref, xpad_ref, xc_ref, dte_ref, cse_ref, crow_ref, st_ref, y_ref, u_ref,
                  bure_ref, buim_ref, mix_ref):
    c = pl.program_id(1)
    rows = SEQ_TILE * CHUNK

    @pl.when(c == 0)
    def _():
        xpad_ref[:, PAD_ROWS - (CONV_K - 1):PAD_ROWS, :] = conv0_ref[...]
        s5re_ref[...] = s5re0_ref[...]
        s5im_ref[...] = s5im0_ref[...]

    x = x_ref[...].reshape(rows, D_MODEL)
    xn_ref[...] = _rms(x, gmix_ref[...]).astype(BF16)

    xpad_ref[:, PAD_ROWS:, :] = _dot(xn_ref[...], wxbc_ref[...]).reshape(SEQ_TILE, CHUNK, CONV_CH)
    acc = convb_ref[...].reshape(1, 1, CONV_CH)
    for k in range(CONV_K):
        lo = PAD_ROWS - (CONV_K - 1) + k
        acc = acc + convw_ref[k:k + 1, :].reshape(1, 1, CONV_CH) * xpad_ref[:, lo:lo + CHUNK, :]
    xc_ref[...] = (acc * jax.nn.sigmoid(acc)).reshape(rows, CONV_CH)
    hist = xpad_ref[:, PAD_ROWS + CHUNK - (CONV_K - 1):, :]
    conv_ref[...] = hist
    xpad_ref[:, PAD_ROWS - (CONV_K - 1):PAD_ROWS, :] = hist

    dt = _softplus(_dot(xn_ref[...], wdt_ref[...]) + dtb_ref[...])
    a = dt * aneg_ref[...]
    tpos = lax.broadcasted_iota(jnp.int32, (rows, LANE), 0) % CHUNK
    sh = 1
    while sh < CHUNK:
        a = a + jnp.where(tpos >= sh, pltpu.roll(a, sh, axis=0), 0.0)
        sh *= 2

    def expand_heads(v):
        hi = v.astype(BF16)
        r1 = v - hi.astype(F32)
        mid = r1.astype(BF16)
        lo = (r1 - mid.astype(F32)).astype(BF16)
        e = eexp_ref[...]
        return _dot(hi, e) + _dot(mid, e) + _dot(lo, e)

    dte_ref[...] = expand_heads(dt)
    cse_ref[...] = expand_heads(a)
    for b in range(SEQ_TILE):
        at = a[b * CHUNK:(b + 1) * CHUNK, :].T
        crow_ref[pl.ds(b, 1), :] = jnp.concatenate([at[h:h + 1, :] for h in range(SSD_HEADS)], axis=1)

    @pl.when(c == 0)
    def _():
        for b in range(SEQ_TILE):
            for h in range(SSD_HEADS):
                st_ref[b, :, h * SSD_HEAD_DIM:(h + 1) * SSD_HEAD_DIM] = ssd0_ref[b, h].T

    gw = SSD_W // SSD_GROUPS
    heads_per_group = SSD_HEADS // SSD_GROUPS
    tri = (lax.broadcasted_iota(jnp.int32, (CHUNK, gw), 0)
           >= lax.broadcasted_iota(jnp.int32, (CHUNK, gw), 1) % CHUNK)
    same_head = ((lax.broadcasted_iota(jnp.int32, (gw, gw), 0) // SSD_HEAD_DIM)
                 == (lax.broadcasted_iota(jnp.int32, (gw, gw), 1) // SSD_HEAD_DIM))

    def seq_body(b, carry):
        r0 = pl.multiple_of(b * CHUNK, CHUNK)
        rs = pl.ds(r0, CHUNK)
        for g in range(SSD_GROUPS):
            ls = pl.ds(g * gw, gw)
            cse = cse_ref[rs, ls]
            cs_last = cse_ref[pl.ds(r0 + CHUNK - 1, 1), ls]
            decay = jnp.exp(jnp.where(tri, cse - crow_ref[pl.ds(b, 1), ls], -jnp.inf))
            xs = xc_ref[rs, ls]
            xdt = xs * dte_ref[rs, ls]
            b_bf = xc_ref[rs, pl.ds(SSD_W + g * SSD_STATE, SSD_STATE)].astype(BF16)
            c_bf = xc_ref[rs, pl.ds(SSD_W + SSD_GROUPS * SSD_STATE + g * SSD_STATE, SSD_STATE)].astype(BF16)
            gram = _dot_nt(c_bf, jnp.concatenate([b_bf] * heads_per_group, axis=0))
            xbd = jnp.where(same_head, jnp.concatenate([xdt] * heads_per_group, axis=0), 0.0).astype(BF16)
            y = _dot((gram * decay).astype(BF16), xbd)
            st = st_ref[b, :, ls]
            y = y + _dot(c_bf, st.astype(BF16)) * jnp.exp(cse)
            y = y + dskip_ref[:, ls] * xs
            y_ref[rs, ls] = y
            upd = _dot_tn(b_bf, (xdt * jnp.exp(cs_last - cse)).astype(BF16))
            st_ref[b, :, ls] = st * jnp.exp(cs_last) + upd
        return carry

    lax.fori_loop(0, SEQ_TILE, seq_body, 0)

    @pl.when(c == n_chunks - 1)
    def _():
        for b in range(SEQ_TILE):
            for h in range(SSD_HEADS):
                ssd_ref[b, h] = st_ref[b, :, h * SSD_HEAD_DIM:(h + 1) * SSD_HEAD_DIM].T

    z = _dot(xn_ref[...], wz_ref[...])
    y = y_ref[...] * (z * jax.nn.sigmoid(z))
    mix_ref[:, 0:SSD_W] = _rms(y, ssdn_ref[...]).astype(BF16)

    u = _dot(xn_ref[...], wu_ref[...])
    u_ref[...] = u
    half_ch = S5_W // 2
    half_st = S5_LANES // 2
    half_tiles = half_st // LANE
    for hf in range(2):
        bu = _dot(u[:, hf * half_ch:(hf + 1) * half_ch].astype(BF16), wb_ref[hf])
        for k in range(half_tiles):
            for b in range(SEQ_TILE):
                dst = pl.ds(b * S5_PITCH, CHUNK)
                src = slice(b * CHUNK, (b + 1) * CHUNK)
                bure_ref[hf * half_tiles + k, dst, :] = bu[src, k * LANE:(k + 1) * LANE]
                buim_ref[hf * half_tiles + k, dst, :] = bu[src, half_st + k * LANE:half_st + (k + 1) * LANE]
    for j in range(S5_LANES // LANE // S5_SCAN_TILES):
        tiles = [j * S5_SCAN_TILES + k for k in range(S5_SCAN_TILES)]
        lr = [jnp.broadcast_to(lamre_ref[:, pl.ds(k * LANE, LANE)], (SEQ_TILE, LANE)) for k in tiles]
        li = [jnp.broadcast_to(lamim_ref[:, pl.ds(k * LANE, LANE)], (SEQ_TILE, LANE)) for k in tiles]

        def step(t, state, tiles=tiles, lr=lr, li=li):
            sr, si = state
            ts = pl.ds(t, SEQ_TILE, stride=S5_PITCH)
            nrs, nis = [], []
            for q, k in enumerate(tiles):
                nr = lr[q] * sr[q] - li[q] * si[q] + bure_ref[k, ts, :]
                ni = lr[q] * si[q] + li[q] * sr[q] + buim_ref[k, ts, :]
                bure_ref[k, ts, :] = nr
                buim_ref[k, ts, :] = ni
                nrs.append(nr)
                nis.append(ni)
            return tuple(nrs), tuple(nis)

        sr, si = lax.fori_loop(
            0, CHUNK, step,
            (tuple(s5re_ref[:, pl.ds(k * LANE, LANE)] for k in tiles),
             tuple(s5im_ref[:, pl.ds(k * LANE, LANE)] for k in tiles)))
        for q, k in enumerate(tiles):
            s5re_ref[:, pl.ds(k * LANE, LANE)] = sr[q]
            s5im_ref[:, pl.ds(k * LANE, LANE)] = si[q]
    def seq_rows(ref, k):
        return jnp.concatenate([ref[k, pl.ds(b * S5_PITCH, CHUNK), :] for b in range(SEQ_TILE)], axis=0)

    ys = []
    for j in range(4):
        s_re = jnp.concatenate([seq_rows(bure_ref, 4 * j + k) for k in range(4)], axis=1)
        s_im = jnp.concatenate([seq_rows(buim_ref, 4 * j + k) for k in range(4)], axis=1)
        ys.append(_dot(s_re.astype(BF16), wc_ref[j, 0]) + _dot(s_im.astype(BF16), wc_ref[j, 1]))
    y5 = jnp.concatenate(ys, axis=1) + s5d_ref[...] * u_ref[...]
    y5 = jax.nn.gelu(y5)
    y5 = y5 * jax.nn.sigmoid(_dot(y5.astype(BF16), wglu_ref[...]))
    mix_ref[:, SSD_W:] = y5.astype(BF16)

    h = x_ref[...].reshape(rows, D_MODEL) + _dot(mix_ref[...], wout_ref[...])
    h_ref[...] = h.reshape(SEQ_TILE, CHUNK, D_MODEL)


def _mixer(x, conv0, ssd0, s5re0, s5im0, w):
    nb, seq, _ = x.shape
    grid = (nb // SEQ_TILE, seq // CHUNK)
    rows = SEQ_TILE * CHUNK
    weights = [w["gmix"], w["wz"], w["wxbc"], w["wdt"], w["wu"], w["convw"], w["convb"],
               w["dtb"], w["aneg"], w["eexp"], w["dskip"], w["ssdn"],
               w["lamre"], w["lamim"], w["wb"], w["wc"], w["s5d"], w["wglu"], w["wout"]]
    state_specs = [pl.BlockSpec((SEQ_TILE, CONV_K - 1, CONV_CH), lambda i, c: (i, 0, 0)),
                   pl.BlockSpec((SEQ_TILE, SSD_HEADS, SSD_HEAD_DIM, SSD_STATE), lambda i, c: (i, 0, 0, 0)),
                   pl.BlockSpec((SEQ_TILE, S5_LANES), lambda i, c: (i, 0)),
                   pl.BlockSpec((SEQ_TILE, S5_LANES), lambda i, c: (i, 0))]
    x_spec = pl.BlockSpec((SEQ_TILE, CHUNK, D_MODEL), lambda i, c: (i, c, 0))
    return pl.pallas_call(
        functools.partial(_mixer_kernel, grid[1]),
        grid=grid,
        in_specs=[x_spec] + state_specs + [_const_spec(a.shape) for a in weights],
        out_specs=[x_spec] + state_specs,
        out_shape=[jax.ShapeDtypeStruct(x.shape, F32),
                   jax.ShapeDtypeStruct(conv0.shape, F32), jax.ShapeDtypeStruct(ssd0.shape, F32),
                   jax.ShapeDtypeStruct(s5re0.shape, F32), jax.ShapeDtypeStruct(s5im0.shape, F32)],
        scratch_shapes=[
            pltpu.VMEM((rows, D_MODEL), BF16),
            pltpu.VMEM((SEQ_TILE, PAD_ROWS + CHUNK, CONV_CH), F32),
            pltpu.VMEM((rows, CONV_CH), F32),
            pltpu.VMEM((rows, SSD_W), F32),
            pltpu.VMEM((rows, SSD_W), F32),
            pltpu.VMEM((SEQ_TILE, SSD_W), F32),
            pltpu.VMEM((SEQ_TILE, SSD_STATE, SSD_W), F32),
            pltpu.VMEM((rows, SSD_W), F32),
            pltpu.VMEM((rows, S5_W), F32),
            pltpu.VMEM((S5_LANES // LANE, SEQ_TILE * S5_PITCH, LANE), F32),
            pltpu.VMEM((S5_LANES // LANE, SEQ_TILE * S5_PITCH, LANE), F32),
            pltpu.VMEM((rows, D_MODEL), BF16),
        ],
        compiler_params=pltpu.CompilerParams(dimension_semantics=("arbitrary", "arbitrary"),
                                             vmem_limit_bytes=VMEM_LIMIT),
        name="mixer",
    )(x, conv0, ssd0, s5re0, s5im0, *weights)


def _attn_kernel(n_seq, tile, heads_axis, h_ref, k_ref, v_ref, gx_ref, wq_ref, wo_ref, gffn_ref, wr_ref, br_ref,
                 h2_ref, tn_ref, eid_ref, gate_ref, o_ref, kv_ref):
    rows = n_seq * tile
    h1 = h_ref[...].reshape(rows, D_MODEL)
    xn = _rms(h1, gx_ref[...]).astype(BF16)
    q = _dot(xn, wq_ref[...])
    scale = X_HEAD_DIM ** -0.5
    for sq in range(n_seq):
        rs = slice(sq * tile, (sq + 1) * tile)
        for hd in range(X_HEADS):
            ls = slice(hd * X_HEAD_DIM, (hd + 1) * X_HEAD_DIM)
            if heads_axis:
                kv_ref[0] = k_ref[0, sq, :, hd, :]
                kv_ref[1] = v_ref[0, sq, :, hd, :]
                kh = kv_ref[0].astype(BF16)
                vh = kv_ref[1].astype(BF16)
            else:
                kh = k_ref[sq, :, ls].astype(BF16)
                vh = v_ref[sq, :, ls].astype(BF16)
            s = _dot_nt(q[rs, ls].astype(BF16), kh) * scale
            s = s - jnp.max(s, axis=-1, keepdims=True)
            p = jnp.exp(s)
            p = p / jnp.sum(p, axis=-1, keepdims=True)
            o_ref[rs, ls] = _dot(p.astype(BF16), vh).astype(BF16)
    h2 = h1 + _dot(o_ref[...], wo_ref[...])
    h2_ref[...] = h2.reshape(n_seq, tile, D_MODEL)

    tn = _rms(h2, gffn_ref[...]).astype(BF16)
    tn32 = tn.astype(F32)
    for sq in range(n_seq):
        for sb in range(ROW_TILES):
            tn_ref[sq, pl.ds(sb, tile, stride=ROW_TILES), :] = tn32[sq * tile:(sq + 1) * tile,
                                                                  sb * LANE:(sb + 1) * LANE]
    logits = _dot(tn, wr_ref[...]) + br_ref[...]
    lane = lax.broadcasted_iota(jnp.int32, logits.shape, 1)
    big = jnp.int32(2 ** 30)
    neg = -jnp.inf
    is_g = (lane >= N_EXPERTS) & (lane < N_EXPERTS + N_EXPERT_GROUPS)
    gl = jnp.where(is_g, logits, neg)
    gmax = jnp.max(gl, axis=-1, keepdims=True)
    g_idx = jnp.min(jnp.where(gl == gmax, lane - N_EXPERTS, big), axis=-1, keepdims=True)
    g_prob = 1.0 / jnp.sum(jnp.exp(gl - gmax), axis=-1, keepdims=True)
    in_grp = (lane < N_EXPERTS) & ((lane // EXPERTS_PER_GROUP) == g_idx)
    el = jnp.where(in_grp, logits, neg)
    m1 = jnp.max(el, axis=-1, keepdims=True)
    i1 = jnp.min(jnp.where(el == m1, lane, big), axis=-1, keepdims=True)
    el2 = jnp.where(lane == i1, neg, el)
    m2 = jnp.max(el2, axis=-1, keepdims=True)
    i2 = jnp.min(jnp.where(el2 == m2, lane, big), axis=-1, keepdims=True)
    e2 = jnp.exp(m2 - m1)
    den = 1.0 + e2
    eid = jnp.where(lane == 0, i1, jnp.where(lane == 1, i2, 0))
    gate = jnp.where(lane == 0, (1.0 / den) * g_prob, jnp.where(lane == 1, (e2 / den) * g_prob, 0.0))
    eid_ref[...] = eid.reshape(n_seq, tile, LANE)
    gate_ref[...] = gate.reshape(n_seq, tile, LANE)


def _attn(h1, mem_k, mem_v, w, n_seq, tile):
    nb, seq, _ = h1.shape
    heads_axis = mem_k.ndim == 5
    row_spec = pl.BlockSpec((n_seq, tile, D_MODEL), lambda b, i: (b, i, 0))
    if heads_axis:
        kv_spec = pl.BlockSpec((1, n_seq, MEM_LEN, X_HEADS, X_HEAD_DIM), lambda b, i: (0, b, 0, 0, 0))
    else:
        kv_spec = pl.BlockSpec((n_seq, MEM_LEN, D_MODEL), lambda b, i: (b, 0, 0))
    lane_spec = pl.BlockSpec((n_seq, tile, LANE), lambda b, i: (b, i, 0))
    weights = [w["gx"], w["wq"], w["wo"], w["gffn"], w["wr"], w["br"]]
    return pl.pallas_call(
        functools.partial(_attn_kernel, n_seq, tile, heads_axis),
        grid=(nb // n_seq, seq // tile),
        in_specs=[row_spec, kv_spec, kv_spec] + [_const_spec(a.shape) for a in weights],
        out_specs=[row_spec, pl.BlockSpec((n_seq, tile * ROW_TILES, LANE), lambda b, i: (b, i, 0)),
                   lane_spec, lane_spec],
        out_shape=[jax.ShapeDtypeStruct(h1.shape, F32),
                   jax.ShapeDtypeStruct((nb, seq * ROW_TILES, LANE), F32),
                   jax.ShapeDtypeStruct((nb, seq, LANE), jnp.int32),
                   jax.ShapeDtypeStruct((nb, seq, LANE), F32)],
        scratch_shapes=[pltpu.VMEM((n_seq * tile, D_MODEL), BF16),
                        pltpu.VMEM((2, MEM_LEN, X_HEAD_DIM), F32)],
        compiler_params=pltpu.CompilerParams(dimension_semantics=("arbitrary", "arbitrary"),
                                             vmem_limit_bytes=VMEM_LIMIT),
        name="attn_router",
    )(h1, mem_k, mem_v, *weights)


def _route_plan(eid):
    n_tok = eid.shape[0]
    n_asg = 2 * n_tok
    assert n_asg % MOE_TILE == 0 and n_asg < (1 << KEY_SHIFT)
    n_tiles = n_asg // MOE_TILE
    n_items = n_tiles + N_EXPERTS
    i32 = jnp.int32
    e_flat = eid.reshape(n_asg)
    a_idx = jnp.arange(n_asg, dtype=i32)
    order = lax.sort(e_flat * (1 << KEY_SHIFT) + a_idx) & ((1 << KEY_SHIFT) - 1)
    row_token = (order // 2).reshape(n_tiles, 1, MOE_TILE)
    onehot = (e_flat[:, None] == jnp.arange(N_EXPERTS, dtype=i32)[None, :]).astype(i32)
    csum = jnp.cumsum(onehot, axis=0)
    rank = jnp.sum(csum * onehot, axis=1) - 1
    counts = csum[-1]
    seg_end = jnp.cumsum(counts)
    seg_start = seg_end - counts
    pos = jnp.sum(onehot * seg_start[None, :], axis=1) + rank
    first_tile = seg_start // MOE_TILE
    last_tile = (seg_end - 1) // MOE_TILE
    items_e = jnp.where(counts > 0, last_tile - first_tile + 1, 0)
    it_end = jnp.cumsum(items_e)
    it_start = it_end - items_e
    w = jnp.arange(n_items, dtype=i32)
    wc = jnp.minimum(w, it_end[-1] - 1)
    it_expert = jnp.sum((wc[:, None] >= it_end[None, :]).astype(i32), axis=1)
    it_onehot = (it_expert[:, None] == jnp.arange(N_EXPERTS, dtype=i32)[None, :]).astype(i32)

    def of_item(table):
        return jnp.sum(it_onehot * table[None, :], axis=1)

    it_tile = of_item(first_tile) + (wc - of_item(it_start))
    it_lo = jnp.clip(of_item(seg_start) - it_tile * MOE_TILE, 0, MOE_TILE)
    it_hi = jnp.clip(of_item(seg_end) - it_tile * MOE_TILE, 0, MOE_TILE)
    it_valid = (w < it_end[-1]).astype(i32)
    items = tuple(a.astype(i32) for a in (it_tile, it_expert, it_lo, it_hi, it_valid))
    return pos.reshape(n_tok, 2), row_token, items


def _token_copy(src_hbm, src_row, dst, dst_row, sem):
    return pltpu.make_async_copy(src_hbm.at[pl.ds(pl.multiple_of(src_row * ROW_TILES, ROW_TILES), ROW_TILES), :],
                                 dst.at[pl.ds(pl.multiple_of(dst_row * ROW_TILES, ROW_TILES), ROW_TILES), :], sem)


def _experts_kernel(n_tiles, tile_ref, exp_ref, lo_ref, hi_ref, valid_ref, rt_cur_ref, rt_nxt_ref, tn_hbm,
                    wg_ref, wu_ref, wd_ref, y_ref, xbuf, xb_ref, wgu_bf, wd_bf, sem):
    w = pl.program_id(0)
    j = tile_ref[w]
    lo = lo_ref[w]
    hi = hi_ref[w]
    valid = valid_ref[w] == 1
    first = jnp.logical_and(valid, lo == 0)
    slot = j % 2

    @pl.when(jnp.logical_or(w == 0, exp_ref[w] != exp_ref[jnp.maximum(w - 1, 0)]))
    def _():
        wgu_bf[:, :EXPERT_FF] = wg_ref[0, 0].astype(BF16)
        wgu_bf[:, EXPERT_FF:] = wu_ref[0, 0].astype(BF16)
        wd_bf[...] = wd_ref[0, 0].astype(BF16)

    def wait_gather(s):
        def body(r, carry):
            _token_copy(tn_hbm, 0, xbuf.at[s], r, sem.at[s]).wait()
            return carry
        lax.fori_loop(0, MOE_TILE, body, 0, unroll=8)

    def expert_rows():
        gu = _dot(xb_ref[...], wgu_bf[...])
        gate = gu[:, :EXPERT_FF]
        hid = (gate * jax.nn.sigmoid(gate)) * gu[:, EXPERT_FF:]
        return _dot(hid.astype(BF16), wd_bf[...])

    def start_gather(rt_ref, s):
        def body(r2, carry):
            for p in range(2):
                r = 2 * r2 + p
                _token_copy(tn_hbm, rt_ref[0, 0, r], xbuf.at[s], r, sem.at[s]).start(priority=p)
            return carry
        lax.fori_loop(0, MOE_TILE // 2, body, 0, unroll=4)

    @pl.when(w == 0)
    def _():
        start_gather(rt_cur_ref, 0)

    @pl.when(jnp.logical_and(first, j + 1 < n_tiles))
    def _():
        start_gather(rt_nxt_ref, 1 - slot)

    @pl.when(first)
    def _():
        wait_gather(slot)
        for sb in range(ROW_TILES):
            xb_ref[:, sb * LANE:(sb + 1) * LANE] = xbuf[slot, pl.ds(sb, MOE_TILE, stride=ROW_TILES), :].astype(BF16)
        y = expert_rows()
        for sb in range(ROW_TILES):
            y_ref[pl.ds(sb, MOE_TILE, stride=ROW_TILES), :] = y[:, sb * LANE:(sb + 1) * LANE]

    @pl.when(jnp.logical_and(valid, lo != 0))
    def _():
        y = expert_rows()
        row = lax.broadcasted_iota(jnp.int32, (MOE_TILE, LANE), 0)
        keep = jnp.logical_and(row >= lo, row < hi)
        for sb in range(ROW_TILES):
            rows = pl.ds(sb, MOE_TILE, stride=ROW_TILES)
            y_ref[rows, :] = jnp.where(keep, y[:, sb * LANE:(sb + 1) * LANE], y_ref[rows, :])


def _experts(tn, row_token, items, w):
    n_tiles = row_token.shape[0]
    n_items = items[0].shape[0]

    def smem_tile(imap):
        return pl.BlockSpec((1, 1, MOE_TILE), imap, memory_space=pltpu.SMEM)

    grid_spec = pltpu.PrefetchScalarGridSpec(
        num_scalar_prefetch=5,
        grid=(n_items,),
        in_specs=[smem_tile(lambda i, tl, ex, lo, hi, va: (tl[i], 0, 0)),
                  smem_tile(lambda i, tl, ex, lo, hi, va: (jnp.minimum(tl[i] + 1, n_tiles - 1), 0, 0)),
                  pl.BlockSpec(memory_space=pl.ANY),
                  pl.BlockSpec((1, 1, D_MODEL, EXPERT_FF), lambda i, tl, ex, lo, hi, va: (0, ex[i], 0, 0)),
                  pl.BlockSpec((1, 1, D_MODEL, EXPERT_FF), lambda i, tl, ex, lo, hi, va: (0, ex[i], 0, 0)),
                  pl.BlockSpec((1, 1, EXPERT_FF, D_MODEL), lambda i, tl, ex, lo, hi, va: (0, ex[i], 0, 0))],
        out_specs=pl.BlockSpec((MOE_TILE * ROW_TILES, LANE), lambda i, tl, ex, lo, hi, va: (tl[i], 0)),
        scratch_shapes=[pltpu.VMEM((2, MOE_TILE * ROW_TILES, LANE), F32),
                        pltpu.VMEM((MOE_TILE, D_MODEL), BF16),
                        pltpu.VMEM((D_MODEL, 2 * EXPERT_FF), BF16),
                        pltpu.VMEM((EXPERT_FF, D_MODEL), BF16),
                        pltpu.SemaphoreType.DMA((2,))],
    )
    return pl.pallas_call(
        functools.partial(_experts_kernel, n_tiles),
        grid_spec=grid_spec,
        out_shape=jax.ShapeDtypeStruct((n_tiles * MOE_TILE * ROW_TILES, LANE), F32),
        compiler_params=pltpu.CompilerParams(dimension_semantics=("arbitrary",), vmem_limit_bytes=VMEM_LIMIT),
        name="experts",
    )(*items, row_token, row_token, tn, w["wgate"], w["wup"], w["wdown"])


def _combine_kernel(n_steps, pos_cur_ref, pos_nxt_ref, ys_hbm, gate_ref, h2_ref, gfin_ref, out_ref, ybuf, sem):
    i = pl.program_id(0)
    slot = i % 2

    def wait_gather(s):
        def body(r, carry):
            for k in range(2):
                _token_copy(ys_hbm, 0, ybuf.at[s, k], r, sem.at[s]).wait()
            return carry
        lax.fori_loop(0, COMBINE_TILE, body, 0, unroll=8)

    def start_gather(pos_ref, s):
        def body(r, carry):
            for k in range(2):
                _token_copy(ys_hbm, pos_ref[0, 0, k * COMBINE_TILE + r], ybuf.at[s, k], r,
                            sem.at[s]).start(priority=k)
            return carry
        lax.fori_loop(0, COMBINE_TILE, body, 0, unroll=8)

    @pl.when(i == 0)
    def _():
        start_gather(pos_cur_ref, 0)

    @pl.when(i + 1 < n_steps)
    def _():
        start_gather(pos_nxt_ref, 1 - slot)

    wait_gather(slot)

    gates = gate_ref[...]
    g0 = gates[:, 0:1]
    g1 = gates[:, 1:2]
    ssq = jnp.zeros((COMBINE_TILE, 1), F32)
    for sb in range(ROW_TILES):
        rows = pl.ds(sb, COMBINE_TILE, stride=ROW_TILES)
        cols = pl.ds(sb * LANE, LANE)
        v = h2_ref[:, cols] + (ybuf[slot, 0, rows, :] * g0 + ybuf[slot, 1, rows, :] * g1)
        ssq = ssq + jnp.sum(v * v, axis=-1, keepdims=True)
        out_ref[:, cols] = v
    out_ref[...] = out_ref[...] * lax.rsqrt(ssq * (1.0 / D_MODEL) + EPS) * gfin_ref[...]


def _combine(ys, pos, gates, h2, gfin):
    n_tok = h2.shape[0]
    n_tiles = n_tok // COMBINE_TILE
    pos_tiles = pos.reshape(n_tiles, COMBINE_TILE, 2).transpose(0, 2, 1).reshape(n_tiles, 1, 2 * COMBINE_TILE)

    def smem_tile(imap):
        return pl.BlockSpec((1, 1, 2 * COMBINE_TILE), imap, memory_space=pltpu.SMEM)

    row_spec = pl.BlockSpec((COMBINE_TILE, D_MODEL), lambda i: (i, 0))
    return pl.pallas_call(
        functools.partial(_combine_kernel, n_tiles),
        grid=(n_tiles,),
        in_specs=[smem_tile(lambda i: (i, 0, 0)),
                  smem_tile(lambda i: (jnp.minimum(i + 1, n_tiles - 1), 0, 0)),
                  pl.BlockSpec(memory_space=pl.ANY),
                  pl.BlockSpec((COMBINE_TILE, LANE), lambda i: (i, 0)),
                  row_spec, _const_spec((1, D_MODEL))],
        out_specs=row_spec,
        out_shape=jax.ShapeDtypeStruct((n_tok, D_MODEL), F32),
        scratch_shapes=[pltpu.VMEM((2, 2, COMBINE_TILE * ROW_TILES, LANE), F32), pltpu.SemaphoreType.DMA((2,))],
        compiler_params=pltpu.CompilerParams(dimension_semantics=("arbitrary",), vmem_limit_bytes=VMEM_LIMIT),
        name="combine",
    )(pos_tiles, pos_tiles, ys, gates, h2, gfin)


def _row(v, width=None):
    v = v.astype(F32).reshape(1, -1)
    if width is not None and v.shape[1] < width:
        v = jnp.pad(v, ((0, 0), (0, width - v.shape[1])))
    return v


def _s5_params(a_re, a_im, log_dt, b_re, b_im, c_re, c_im):
    dt = jnp.exp(log_dt)[:, None]
    mag = jnp.exp(a_re * dt)
    lam_re, lam_im = mag * jnp.cos(a_im * dt), mag * jnp.sin(a_im * dt)
    den = a_re * a_re + a_im * a_im
    f_re = ((lam_re - 1.0) * a_re + lam_im * a_im) / den
    f_im = (lam_im * a_re - (lam_re - 1.0) * a_im) / den
    bb_re = f_re[..., None] * b_re - f_im[..., None] * b_im
    bb_im = f_re[..., None] * b_im + f_im[..., None] * b_re
    half = S5_GROUPS // 2

    def in_proj(bb):
        bb = bb.reshape(2, half, S5_STATE, S5_GROUP_CH)
        eye = jnp.eye(half, dtype=F32)
        m = jnp.einsum("hgnc,gk->hgckn", bb, eye)
        return m.reshape(2, half * S5_GROUP_CH, half * S5_STATE)

    wb = jnp.concatenate([in_proj(bb_re), in_proj(bb_im)], axis=2).astype(BF16)

    def out_proj(cc):
        q = S5_GROUPS // 4
        cc = cc.reshape(4, q, S5_GROUP_CH, S5_STATE)
        eye = jnp.eye(q, dtype=F32)
        m = jnp.einsum("qgcn,gk->qgnkc", cc, eye)
        return m.reshape(4, q * S5_STATE, q * S5_GROUP_CH)

    wc = jnp.stack([out_proj(c_re), -out_proj(c_im)], axis=1).astype(BF16)
    return lam_re.reshape(1, S5_LANES), lam_im.reshape(1, S5_LANES), wb, wc


def kernel(x_prompt, x_sample, cache_conv, state_ssd, state_s5_re, state_s5_im, cache_mem_k, cache_mem_v,
           mem_prompt, norm_mix, w_in, conv_w, conv_b, dt_bias, a_log, d_skip, ssd_norm, s5_a_re, s5_a_im,
           s5_log_dt, s5_b_re, s5_b_im, s5_c_re, s5_c_im, s5_d, s5_w_glu, w_out, norm_x, norm_mem, w_q,
           w_k, w_v, w_o, norm_ffn, w_router_group, b_router_group, w_router_expert, b_router_expert,
           w_gate, w_up, w_down, norm_final):
    depth = norm_mix.shape[0]
    assert depth == 1
    l = 0
    bp, seq_p, _ = x_prompt.shape
    bs, seq_s, _ = x_sample.shape

    o1 = SSD_W
    o2 = o1 + CONV_CH
    o3 = o2 + SSD_HEADS
    lam_re, lam_im, wb, wc = _s5_params(s5_a_re[l], s5_a_im[l], s5_log_dt[l], s5_b_re[l], s5_b_im[l],
                                        s5_c_re[l], s5_c_im[l])
    wm = {
        "gmix": _row(norm_mix[l]),
        "wz": w_in[l][:, :o1].astype(BF16),
        "wxbc": w_in[l][:, o1:o2].astype(BF16),
        "wdt": jnp.pad(w_in[l][:, o2:o3], ((0, 0), (0, LANE - SSD_HEADS))).astype(BF16),
        "wu": w_in[l][:, o3:].astype(BF16),
        "convw": conv_w[l].astype(F32), "convb": _row(conv_b[l]),
        "dtb": _row(dt_bias[l], LANE), "aneg": _row(-jnp.exp(a_log[l].astype(F32)), LANE),
        "eexp": (jnp.arange(LANE, dtype=jnp.int32)[:, None]
                 == jnp.arange(SSD_W, dtype=jnp.int32)[None, :] // SSD_HEAD_DIM).astype(BF16),
        "dskip": _row(jnp.repeat(d_skip[l].astype(F32), SSD_HEAD_DIM)), "ssdn": _row(ssd_norm[l]),
        "lamre": lam_re, "lamim": lam_im, "wb": wb, "wc": wc,
        "s5d": _row(s5_d[l]), "wglu": s5_w_glu[l].astype(BF16), "wout": w_out[l].astype(BF16),
    }
    wr = jnp.concatenate([w_router_expert[l].reshape(D_MODEL, N_EXPERTS), w_router_group[l]], axis=1)
    br = jnp.concatenate([b_router_expert[l].reshape(N_EXPERTS), b_router_group[l]])
    wa = {
        "gx": _row(norm_x[l]), "wq": w_q[l].astype(BF16), "wo": w_o[l].astype(BF16),
        "gffn": _row(norm_ffn[l]),
        "wr": jnp.pad(wr, ((0, 0), (0, LANE - wr.shape[1]))).astype(BF16), "br": _row(br, LANE),
    }
    we = {"wgate": w_gate, "wup": w_up, "wdown": w_down}
    gfin = _row(norm_final)

    mk_p, mv_p, mk_state, mv_state = _memkv(mem_prompt, _row(norm_mem[l]), w_k[l].astype(BF16), w_v[l].astype(BF16))

    def group(x, conv0, ssd0, s5re0, s5im0, mem_k, mem_v, attn_seqs, attn_tile):
        nb, seq, _ = x.shape
        n_tok = nb * seq
        h1, conv, ssd, s5re, s5im = _mixer(x, conv0, ssd0, s5re0.reshape(nb, S5_LANES),
                                           s5im0.reshape(nb, S5_LANES), wm)
        h2, tn, eid, gates = _attn(h1, mem_k, mem_v, wa, attn_seqs, attn_tile)
        pos, row_token, items = _route_plan(eid.reshape(n_tok, LANE)[:, :2])
        ys = _experts(tn.reshape(n_tok * ROW_TILES, LANE), row_token, items, we)
        y = _combine(ys, pos, gates.reshape(n_tok, LANE), h2.reshape(n_tok, D_MODEL), gfin)
        return (y.reshape(nb, seq, D_MODEL), conv[None], ssd[None],
                s5re.reshape(1, nb, S5_GROUPS, S5_STATE), s5im.reshape(1, nb, S5_GROUPS, S5_STATE))

    zeros = lambda *s: jnp.zeros(s, F32)
    y_p, conv_p, ssd_p, s5re_p, s5im_p = group(
        x_prompt, zeros(bp, CONV_K - 1, CONV_CH), zeros(bp, SSD_HEADS, SSD_HEAD_DIM, SSD_STATE),
        zeros(bp, S5_GROUPS, S5_STATE), zeros(bp, S5_GROUPS, S5_STATE), mk_p, mv_p,
        1, min(512, seq_p))
    y_s, conv_s, ssd_s, s5re_s, s5im_s = group(
        x_sample, cache_conv.reshape(cache_conv.shape[1:]), state_ssd.reshape(state_ssd.shape[1:]),
        state_s5_re.reshape(state_s5_re.shape[1:]), state_s5_im.reshape(state_s5_im.shape[1:]),
        cache_mem_k, cache_mem_v, SAMPLE_ATTN_SEQS, seq_s)

    return (y_p, y_s, conv_p, ssd_p, s5re_p, s5im_p, mk_state, mv_state, conv_s, ssd_s, s5re_s, s5im_s)
```

```python
import functools
import math

import jax
import jax.numpy as jnp
from jax import lax
from jax.experimental import pallas as pl
from jax.experimental.pallas import tpu as pltpu

F32 = jnp.float32
BF16 = jnp.bfloat16
EPS = 1e-6

D_MODEL = 1024
CHUNK = 64
SEQ_TILE = 8
SSD_W = 512
SSD_HEAD_DIM = 64
SSD_HEADS = 8
SSD_GROUPS = 2
SSD_STATE = 128
CONV_K = 4
CONV_CH = 1024
S5_W = 512
S5_GROUPS = 32
S5_GROUP_CH = 16
S5_STATE = 64
S5_LANES = S5_GROUPS * S5_STATE
MEM_LEN = 256
X_HEADS = 4
X_HEAD_DIM = 256
N_EXPERT_GROUPS = 4
EXPERTS_PER_GROUP = 8
N_EXPERTS = 32
EXPERT_FF = 256
LANE = 128
PAD_ROWS = 8
S5_PITCH = CHUNK + PAD_ROWS
S5_SCAN_TILES = 8
MOE_TILE = 256
COMBINE_TILE = 512
SAMPLE_ATTN_SEQS = 4
ROW_TILES = D_MODEL // LANE
KEY_SHIFT = 20
VMEM_LIMIT = 56 * 1024 * 1024


def _rms(x, g):
    return x * lax.rsqrt(jnp.mean(x * x, axis=-1, keepdims=True) + EPS) * g


def _dot(a, b):
    return jnp.dot(a, b, preferred_element_type=F32)


def _dot_nt(a, b):
    return lax.dot_general(a, b, (((1,), (1,)), ((), ())), preferred_element_type=F32)


def _dot_tn(a, b):
    return lax.dot_general(a, b, (((0,), (0,)), ((), ())), preferred_element_type=F32)


def _const_spec(shape):
    nd = len(shape)
    return pl.BlockSpec(shape, lambda *_: (0,) * nd)


def _memkv_kernel(m_ref, g_ref, wk_ref, wv_ref, k_ref, v_ref, k5_ref, v5_ref):
    mn = _rms(m_ref[0], g_ref[...]).astype(BF16)
    k = _dot(mn, wk_ref[...])
    v = _dot(mn, wv_ref[...])
    k_ref[0] = k
    v_ref[0] = v
    for hd in range(X_HEADS):
        k5_ref[0, 0, :, hd, :] = k[:, hd * X_HEAD_DIM:(hd + 1) * X_HEAD_DIM]
        v5_ref[0, 0, :, hd, :] = v[:, hd * X_HEAD_DIM:(hd + 1) * X_HEAD_DIM]


def _memkv(mem, g, wk, wv):
    nb = mem.shape[0]
    flat_spec = pl.BlockSpec((1, MEM_LEN, D_MODEL), lambda i: (i, 0, 0))
    head_spec = pl.BlockSpec((1, 1, MEM_LEN, X_HEADS, X_HEAD_DIM), lambda i: (0, i, 0, 0, 0))
    flat_shape = jax.ShapeDtypeStruct((nb, MEM_LEN, D_MODEL), F32)
    head_shape = jax.ShapeDtypeStruct((1, nb, MEM_LEN, X_HEADS, X_HEAD_DIM), F32)
    return pl.pallas_call(
        _memkv_kernel,
        grid=(nb,),
        in_specs=[flat_spec, _const_spec((1, D_MODEL)), _const_spec((D_MODEL, D_MODEL)),
                  _const_spec((D_MODEL, D_MODEL))],
        out_specs=[flat_spec, flat_spec, head_spec, head_spec],
        out_shape=[flat_shape, flat_shape, head_shape, head_shape],
        compiler_params=pltpu.CompilerParams(dimension_semantics=("arbitrary",), vmem_limit_bytes=VMEM_LIMIT),
        name="memkv",
    )(mem, g, wk, wv)


def _softplus(x):
    return jnp.maximum(x, 0.0) + jnp.log1p(jnp.exp(-jnp.abs(x)))


def _mixer_kernel(n_chunks, x_ref, conv0_ref, ssd0_ref, s5re0_ref, s5im0_ref,
                  gmix_ref, wz_ref, wxbc_ref, wdt_ref, wu_ref, convw_ref, convb_ref,
                  dtb_ref, aneg_ref, eexp_ref, dskip_ref, ssdn_ref,
                  lamre_ref, lamim_ref, wb_ref, wc_ref, s5d_ref, wglu_ref, wout_ref,
                  h_ref, conv_ref, ssd_ref, s5re_ref, s5im_ref,
                  xn_ref, xpad_ref, xc_ref, dte_ref, cse_ref, crow_ref, st_ref, y_ref, u_ref,
                  bure_ref, buim_ref, mix_ref):
    c = pl.program_id(1)
    rows = SEQ_TILE * CHUNK

    @pl.when(c == 0)
    def _():
        xpad_ref[:, PAD_ROWS - (CONV_K - 1):PAD_ROWS, :] = conv0_ref[...]
        s5re_ref[...] = s5re0_ref[...]
        s5im_ref[...] = s5im0_ref[...]

    x = x_ref[...].reshape(rows, D_MODEL)
    xn_ref[...] = _rms(x, gmix_ref[...]).astype(BF16)

    xpad_ref[:, PAD_ROWS:, :] = _dot(xn_ref[...], wxbc_ref[...]).reshape(SEQ_TILE, CHUNK, CONV_CH)
    acc = convb_ref[...].reshape(1, 1, CONV_CH)
    for k in range(CONV_K):
        lo = PAD_ROWS - (CONV_K - 1) + k
        acc = acc + convw_ref[k:k + 1, :].reshape(1, 1, CONV_CH) * xpad_ref[:, lo:lo + CHUNK, :]
    xc_ref[...] = (acc * jax.nn.sigmoid(acc)).reshape(rows, CONV_CH)
    hist = xpad_ref[:, PAD_ROWS + CHUNK - (CONV_K - 1):, :]
    conv_ref[...] = hist
    xpad_ref[:, PAD_ROWS - (CONV_K - 1):PAD_ROWS, :] = hist

    dt = _softplus(_dot(xn_ref[...], wdt_ref[...]) + dtb_ref[...])
    a = dt * aneg_ref[...]
    tpos = lax.broadcasted_iota(jnp.int32, (rows, LANE), 0) % CHUNK
    sh = 1
    while sh < CHUNK:
        a = a + jnp.where(tpos >= sh, pltpu.roll(a, sh, axis=0), 0.0)
        sh *= 2

    def expand_heads(v):
        hi = v.astype(BF16)
        r1 = v - hi.astype(F32)
        mid = r1.astype(BF16)
        lo = (r1 - mid.astype(F32)).astype(BF16)
        e = eexp_ref[...]
        return _dot(hi, e) + _dot(mid, e) + _dot(lo, e)

    dte_ref[...] = expand_heads(dt)
    cse_ref[...] = expand_heads(a)
    for b in range(SEQ_TILE):
        at = a[b * CHUNK:(b + 1) * CHUNK, :].T
        crow_ref[pl.ds(b, 1), :] = jnp.concatenate([at[h:h + 1, :] for h in range(SSD_HEADS)], axis=1)

    @pl.when(c == 0)
    def _():
        for b in range(SEQ_TILE):
            for h in range(SSD_HEADS):
                st_ref[b, :, h * SSD_HEAD_DIM:(h + 1) * SSD_HEAD_DIM] = ssd0_ref[b, h].T

    gw = SSD_W // SSD_GROUPS
    heads_per_group = SSD_HEADS // SSD_GROUPS
    tri = (lax.broadcasted_iota(jnp.int32, (CHUNK, gw), 0)
           >= lax.broadcasted_iota(jnp.int32, (CHUNK, gw), 1) % CHUNK)
    same_head = ((lax.broadcasted_iota(jnp.int32, (gw, gw), 0) // SSD_HEAD_DIM)
                 == (lax.broadcasted_iota(jnp.int32, (gw, gw), 1) // SSD_HEAD_DIM))

    def seq_body(b, carry):
        r0 = pl.multiple_of(b * CHUNK, CHUNK)
        rs = pl.ds(r0, CHUNK)
        for g in range(SSD_GROUPS):
            ls = pl.ds(g * gw, gw)
            cse = cse_ref[rs, ls]
            cs_last = cse_ref[pl.ds(r0 + CHUNK - 1, 1), ls]
            decay = jnp.exp(jnp.where(tri, cse - crow_ref[pl.ds(b, 1), ls], -jnp.inf))
            xs = xc_ref[rs, ls]
            xdt = xs * dte_ref[rs, ls]
            b_bf = xc_ref[rs, pl.ds(SSD_W + g * SSD_STATE, SSD_STATE)].astype(BF16)
            c_bf = xc_ref[rs, pl.ds(SSD_W + SSD_GROUPS * SSD_STATE + g * SSD_STATE, SSD_STATE)].astype(BF16)
            gram = _dot_nt(c_bf, jnp.concatenate([b_bf] * heads_per_group, axis=0))
            xbd = jnp.where(same_head, jnp.concatenate([xdt] * heads_per_group, axis=0), 0.0).astype(BF16)
            y = _dot((gram * decay).astype(BF16), xbd)
            st = st_ref[b, :, ls]
            y = y + _dot(c_bf, st.astype(BF16)) * jnp.exp(cse)
            y = y + dskip_ref[:, ls] * xs
            y_ref[rs, ls] = y
            upd = _dot_tn(b_bf, (xdt * jnp.exp(cs_last - cse)).astype(BF16))
            st_ref[b, :, ls] = st * jnp.exp(cs_last) + upd
        return carry

    lax.fori_loop(0, SEQ_TILE, seq_body, 0)

    @pl.when(c == n_chunks - 1)
    def _():
        for b in range(SEQ_TILE):
            for h in range(SSD_HEADS):
                ssd_ref[b, h] = st_ref[b, :, h * SSD_HEAD_DIM:(h + 1) * SSD_HEAD_DIM].T

    z = _dot(xn_ref[...], wz_ref[...])
    y = y_ref[...] * (z * jax.nn.sigmoid(z))
    mix_ref[:, 0:SSD_W] = _rms(y, ssdn_ref[...]).astype(BF16)

    u = _dot(xn_ref[...], wu_ref[...])
    u_ref[...] = u
    half_ch = S5_W // 2
    half_st = S5_LANES // 2
    half_tiles = half_st // LANE
    for hf in range(2):
        bu = _dot(u[:, hf * half_ch:(hf + 1) * half_ch].astype(BF16), wb_ref[hf])
        for k in range(half_tiles):
            for b in range(SEQ_TILE):
                dst = pl.ds(b * S5_PITCH, CHUNK)
                src = slice(b * CHUNK, (b + 1) * CHUNK)
                bure_ref[hf * half_tiles + k, dst, :] = bu[src, k * LANE:(k + 1) * LANE]
                buim_ref[hf * half_tiles + k, dst, :] = bu[src, half_st + k * LANE:half_st + (k + 1) * LANE]
    for j in range(S5_LANES // LANE // S5_SCAN_TILES):
        tiles = [j * S5_SCAN_TILES + k for k in range(S5_SCAN_TILES)]
        lr = [jnp.broadcast_to(lamre_ref[:, pl.ds(k * LANE, LANE)], (SEQ_TILE, LANE)) for k in tiles]
        li = [jnp.broadcast_to(lamim_ref[:, pl.ds(k * LANE, LANE)], (SEQ_TILE, LANE)) for k in tiles]

        def step(t, state, tiles=tiles, lr=lr, li=li):
            sr, si = state
            ts = pl.ds(t, SEQ_TILE, stride=S5_PITCH)
            nrs, nis = [], []
            for q, k in enumerate(tiles):
                nr = lr[q] * sr[q] - li[q] * si[q] + bure_ref[k, ts, :]
                ni = lr[q] * si[q] + li[q] * sr[q] + buim_ref[k, ts, :]
                bure_ref[k, ts, :] = nr
                buim_ref[k, ts, :] = ni
                nrs.append(nr)
                nis.append(ni)
            return tuple(nrs), tuple(nis)

        sr, si = lax.fori_loop(
            0, CHUNK, step,
            (tuple(s5re_ref[:, pl.ds(k * LANE, LANE)] for k in tiles),
             tuple(s5im_ref[:, pl.ds(k * LANE, LANE)] for k in tiles)))
        for q, k in enumerate(tiles):
            s5re_ref[:, pl.ds(k * LANE, LANE)] = sr[q]
            s5im_ref[:, pl.ds(k * LANE, LANE)] = si[q]
    def seq_rows(ref, k):
        return jnp.concatenate([ref[k, pl.ds(b * S5_PITCH, CHUNK), :] for b in range(SEQ_TILE)], axis=0)

    ys = []
    for j in range(4):
        s_re = jnp.concatenate([seq_rows(bure_ref, 4 * j + k) for k in range(4)], axis=1)
        s_im = jnp.concatenate([seq_rows(buim_ref, 4 * j + k) for k in range(4)], axis=1)
        ys.append(_dot(s_re.astype(BF16), wc_ref[j, 0]) + _dot(s_im.astype(BF16), wc_ref[j, 1]))
    y5 = jnp.concatenate(ys, axis=1) + s5d_ref[...] * u_ref[...]
    y5 = jax.nn.gelu(y5)
    y5 = y5 * jax.nn.sigmoid(_dot(y5.astype(BF16), wglu_ref[...]))
    mix_ref[:, SSD_W:] = y5.astype(BF16)

    h = x_ref[...].reshape(rows, D_MODEL) + _dot(mix_ref[...], wout_ref[...])
    h_ref[...] = h.reshape(SEQ_TILE, CHUNK, D_MODEL)


def _mixer(x, conv0, ssd0, s5re0, s5im0, w):
    nb, seq, _ = x.shape
    grid = (nb // SEQ_TILE, seq // CHUNK)
    rows = SEQ_TILE * CHUNK
    weights = [w["gmix"], w["wz"], w["wxbc"], w["wdt"], w["wu"], w["convw"], w["convb"],
               w["dtb"], w["aneg"], w["eexp"], w["dskip"], w["ssdn"],
               w["lamre"], w["lamim"], w["wb"], w["wc"], w["s5d"], w["wglu"], w["wout"]]
    state_specs = [pl.BlockSpec((SEQ_TILE, CONV_K - 1, CONV_CH), lambda i, c: (i, 0, 0)),
                   pl.BlockSpec((SEQ_TILE, SSD_HEADS, SSD_HEAD_DIM, SSD_STATE), lambda i, c: (i, 0, 0, 0)),
                   pl.BlockSpec((SEQ_TILE, S5_LANES), lambda i, c: (i, 0)),
                   pl.BlockSpec((SEQ_TILE, S5_LANES), lambda i, c: (i, 0))]
    x_spec = pl.BlockSpec((SEQ_TILE, CHUNK, D_MODEL), lambda i, c: (i, c, 0))
    return pl.pallas_call(
        functools.partial(_mixer_kernel, grid[1]),
        grid=grid,
        in_specs=[x_spec] + state_specs + [_const_spec(a.shape) for a in weights],
        out_specs=[x_spec] + state_specs,
        out_shape=[jax.ShapeDtypeStruct(x.shape, F32),
                   jax.ShapeDtypeStruct(conv0.shape, F32), jax.ShapeDtypeStruct(ssd0.shape, F32),
                   jax.ShapeDtypeStruct(s5re0.shape, F32), jax.ShapeDtypeStruct(s5im0.shape, F32)],
        scratch_shapes=[
            pltpu.VMEM((rows, D_MODEL), BF16),
            pltpu.VMEM((SEQ_TILE, PAD_ROWS + CHUNK, CONV_CH), F32),
            pltpu.VMEM((rows, CONV_CH), F32),
            pltpu.VMEM((rows, SSD_W), F32),
            pltpu.VMEM((rows, SSD_W), F32),
            pltpu.VMEM((SEQ_TILE, SSD_W), F32),
            pltpu.VMEM((SEQ_TILE, SSD_STATE, SSD_W), F32),
            pltpu.VMEM((rows, SSD_W), F32),
            pltpu.VMEM((rows, S5_W), F32),
            pltpu.VMEM((S5_LANES // LANE, SEQ_TILE * S5_PITCH, LANE), F32),
            pltpu.VMEM((S5_LANES // LANE, SEQ_TILE * S5_PITCH, LANE), F32),
            pltpu.VMEM((rows, D_MODEL), BF16),
        ],
        compiler_params=pltpu.CompilerParams(dimension_semantics=("arbitrary", "arbitrary"),
                                             vmem_limit_bytes=VMEM_LIMIT),
        name="mixer",
    )(x, conv0, ssd0, s5re0, s5im0, *weights)


def _attn_kernel(n_seq, tile, heads_axis, h_ref, k_ref, v_ref, gx_ref, wq_ref, wo_ref, gffn_ref, wr_ref, br_ref,
                 h2_ref, tn_ref, eid_ref, gate_ref, o_ref, kv_ref):
    rows = n_seq * tile
    h1 = h_ref[...].reshape(rows, D_MODEL)
    xn = _rms(h1, gx_ref[...]).astype(BF16)
    q = _dot(xn, wq_ref[...])
    scale = X_HEAD_DIM ** -0.5
    for sq in range(n_seq):
        rs = slice(sq * tile, (sq + 1) * tile)
        for hd in range(X_HEADS):
            ls = slice(hd * X_HEAD_DIM, (hd + 1) * X_HEAD_DIM)
            if heads_axis:
                kv_ref[0] = k_ref[0, sq, :, hd, :]
                kv_ref[1] = v_ref[0, sq, :, hd, :]
                kh = kv_ref[0].astype(BF16)
                vh = kv_ref[1].astype(BF16)
            else:
                kh = k_ref[sq, :, ls].astype(BF16)
                vh = v_ref[sq, :, ls].astype(BF16)
            s = _dot_nt(q[rs, ls].astype(BF16), kh) * scale
            s = s - jnp.max(s, axis=-1, keepdims=True)
            p = jnp.exp(s)
            p = p / jnp.sum(p, axis=-1, keepdims=True)
            o_ref[rs, ls] = _dot(p.astype(BF16), vh).astype(BF16)
    h2 = h1 + _dot(o_ref[...], wo_ref[...])
    h2_ref[...] = h2.reshape(n_seq, tile, D_MODEL)

    tn = _rms(h2, gffn_ref[...]).astype(BF16)
    tn32 = tn.astype(F32)
    for sq in range(n_seq):
        for sb in range(ROW_TILES):
            tn_ref[sq, pl.ds(sb, tile, stride=ROW_TILES), :] = tn32[sq * tile:(sq + 1) * tile,
                                                                  sb * LANE:(sb + 1) * LANE]
    lt = (_dot(tn, wr_ref[...]) + br_ref[...]).T
    sub = lax.broadcasted_iota(jnp.int32, (EXPERTS_PER_GROUP, rows), 0)
    big = jnp.int32(2 ** 30)
    neg = -jnp.inf

    def first_max(x):
        m = jnp.max(x, axis=0, keepdims=True)
        return m, jnp.min(jnp.where(x == m, sub, big), axis=0, keepdims=True)

    gl = jnp.where(sub < N_EXPERT_GROUPS, lt[N_EXPERTS:N_EXPERTS + EXPERTS_PER_GROUP, :], neg)
    gmax, g_idx = first_max(gl)
    g_prob = 1.0 / jnp.sum(jnp.exp(gl - gmax), axis=0, keepdims=True)
    el = lt[0:EXPERTS_PER_GROUP, :]
    for g in range(1, N_EXPERT_GROUPS):
        el = jnp.where(g_idx == g, lt[g * EXPERTS_PER_GROUP:(g + 1) * EXPERTS_PER_GROUP, :], el)
    m1, i1 = first_max(el)
    m2, i2 = first_max(jnp.where(sub == i1, neg, el))
    e2 = jnp.exp(m2 - m1)
    den = 1.0 + e2
    base = g_idx * EXPERTS_PER_GROUP
    zeros = jnp.zeros((LANE - EXPERTS_PER_GROUP, rows), F32)
    eid_t = jnp.where(sub == 0, base + i1, jnp.where(sub == 1, base + i2, 0)).astype(F32)
    gate_t = jnp.where(sub == 0, (1.0 / den) * g_prob, jnp.where(sub == 1, (e2 / den) * g_prob, 0.0))
    eid = jnp.concatenate([eid_t, zeros], axis=0).T.astype(jnp.int32)
    gate = jnp.concatenate([gate_t, zeros], axis=0).T
    eid_ref[...] = eid.reshape(n_seq, tile, LANE)
    gate_ref[...] = gate.reshape(n_seq, tile, LANE)


def _attn(h1, mem_k, mem_v, w, n_seq, tile):
    nb, seq, _ = h1.shape
    heads_axis = mem_k.ndim == 5
    row_spec = pl.BlockSpec((n_seq, tile, D_MODEL), lambda b, i: (b, i, 0))
    if heads_axis:
        kv_spec = pl.BlockSpec((1, n_seq, MEM_LEN, X_HEADS, X_HEAD_DIM), lambda b, i: (0, b, 0, 0, 0))
    else:
        kv_spec = pl.BlockSpec((n_seq, MEM_LEN, D_MODEL), lambda b, i: (b, 0, 0))
    lane_spec = pl.BlockSpec((n_seq, tile, LANE), lambda b, i: (b, i, 0))
    weights = [w["gx"], w["wq"], w["wo"], w["gffn"], w["wr"], w["br"]]
    return pl.pallas_call(
        functools.partial(_attn_kernel, n_seq, tile, heads_axis),
        grid=(nb // n_seq, seq // tile),
        in_specs=[row_spec, kv_spec, kv_spec] + [_const_spec(a.shape) for a in weights],
        out_specs=[row_spec, pl.BlockSpec((n_seq, tile * ROW_TILES, LANE), lambda b, i: (b, i, 0)),
                   lane_spec, lane_spec],
        out_shape=[jax.ShapeDtypeStruct(h1.shape, F32),
                   jax.ShapeDtypeStruct((nb, seq * ROW_TILES, LANE), F32),
                   jax.ShapeDtypeStruct((nb, seq, LANE), jnp.int32),
                   jax.ShapeDtypeStruct((nb, seq, LANE), F32)],
        scratch_shapes=[pltpu.VMEM((n_seq * tile, D_MODEL), BF16),
                        pltpu.VMEM((2, MEM_LEN, X_HEAD_DIM), F32)],
        compiler_params=pltpu.CompilerParams(dimension_semantics=("arbitrary", "arbitrary"),
                                             vmem_limit_bytes=VMEM_LIMIT),
        name="attn_router",
    )(h1, mem_k, mem_v, *weights)


def _route_plan(eid):
    n_tok = eid.shape[0]
    n_asg = 2 * n_tok
    assert n_asg % MOE_TILE == 0 and n_asg < (1 << KEY_SHIFT)
    n_tiles = n_asg // MOE_TILE
    n_items = n_tiles + N_EXPERTS
    i32 = jnp.int32
    e_flat = eid.reshape(n_asg)
    a_idx = jnp.arange(n_asg, dtype=i32)
    order = lax.sort(e_flat * (1 << KEY_SHIFT) + a_idx) & ((1 << KEY_SHIFT) - 1)
    row_token = (order // 2).reshape(n_tiles, 1, MOE_TILE)
    onehot = (e_flat[:, None] == jnp.arange(N_EXPERTS, dtype=i32)[None, :]).astype(i32)
    csum = jnp.cumsum(onehot, axis=0)
    rank = jnp.sum(csum * onehot, axis=1) - 1
    counts = csum[-1]
    seg_end = jnp.cumsum(counts)
    seg_start = seg_end - counts
    pos = jnp.sum(onehot * seg_start[None, :], axis=1) + rank
    first_tile = seg_start // MOE_TILE
    last_tile = (seg_end - 1) // MOE_TILE
    items_e = jnp.where(counts > 0, last_tile - first_tile + 1, 0)
    it_end = jnp.cumsum(items_e)
    it_start = it_end - items_e
    w = jnp.arange(n_items, dtype=i32)
    wc = jnp.minimum(w, it_end[-1] - 1)
    it_expert = jnp.sum((wc[:, None] >= it_end[None, :]).astype(i32), axis=1)
    it_onehot = (it_expert[:, None] == jnp.arange(N_EXPERTS, dtype=i32)[None, :]).astype(i32)

    def of_item(table):
        return jnp.sum(it_onehot * table[None, :], axis=1)

    it_tile = of_item(first_tile) + (wc - of_item(it_start))
    it_lo = jnp.clip(of_item(seg_start) - it_tile * MOE_TILE, 0, MOE_TILE)
    it_hi = jnp.clip(of_item(seg_end) - it_tile * MOE_TILE, 0, MOE_TILE)
    it_valid = (w < it_end[-1]).astype(i32)
    items = tuple(a.astype(i32) for a in (it_tile, it_expert, it_lo, it_hi, it_valid))
    return pos.reshape(n_tok, 2), row_token, items


def _token_copy(src_hbm, src_row, dst, dst_row, sem):
    return pltpu.make_async_copy(src_hbm.at[pl.ds(pl.multiple_of(src_row * ROW_TILES, ROW_TILES), ROW_TILES), :],
                                 dst.at[pl.ds(pl.multiple_of(dst_row * ROW_TILES, ROW_TILES), ROW_TILES), :], sem)


def _experts_kernel(n_tiles, n_items, tile_ref, exp_ref, lo_ref, hi_ref, valid_ref, rt_cur_ref, rt_nxt_ref, tn_hbm,
                    wg_ref, wu_ref, wd_ref, y_ref, xbuf, xb_ref, wgu_bf, wd_bf, sem):
    w = pl.program_id(0)
    j = tile_ref[w]
    lo = lo_ref[w]
    hi = hi_ref[w]
    valid = valid_ref[w] == 1
    first = jnp.logical_and(valid, lo == 0)
    slot = j % 2

    @pl.when(jnp.logical_or(w == 0, exp_ref[w] != exp_ref[jnp.maximum(w - 1, 0)]))
    def _():
        wgu_bf[:, :EXPERT_FF] = wg_ref[0, 0].astype(BF16)
        wgu_bf[:, EXPERT_FF:] = wu_ref[0, 0].astype(BF16)
        wd_bf[...] = wd_ref[0, 0].astype(BF16)

    def wait_gather(s):
        def body(r, carry):
            _token_copy(tn_hbm, 0, xbuf.at[s], r, sem.at[s]).wait()
            return carry
        lax.fori_loop(0, MOE_TILE, body, 0, unroll=8)

    def expert_rows():
        gu = _dot(xb_ref[...], wgu_bf[...])
        gate = gu[:, :EXPERT_FF]
        hid = (gate * jax.nn.sigmoid(gate)) * gu[:, EXPERT_FF:]
        return _dot(hid.astype(BF16), wd_bf[...])

    def start_gather(rt_ref, s):
        def body(r2, carry):
            for p in range(2):
                r = 2 * r2 + p
                _token_copy(tn_hbm, rt_ref[0, 0, r], xbuf.at[s], r, sem.at[s]).start(priority=p)
            return carry
        lax.fori_loop(0, MOE_TILE // 2, body, 0, unroll=4)

    @pl.when(w == 0)
    def _():
        start_gather(rt_cur_ref, 0)

    @pl.when(first)
    def _():
        wait_gather(slot)
        for sb in range(ROW_TILES):
            xb_ref[:, sb * LANE:(sb + 1) * LANE] = xbuf[slot, pl.ds(sb, MOE_TILE, stride=ROW_TILES), :].astype(BF16)
        for r in range(MOE_TILE):
            _token_copy(tn_hbm, rt_nxt_ref[0, 0, r], xbuf.at[1 - slot], r, sem.at[1 - slot]).start(priority=r % 2)
        y = expert_rows()
        for sb in range(ROW_TILES):
            y_ref[pl.ds(sb, MOE_TILE, stride=ROW_TILES), :] = y[:, sb * LANE:(sb + 1) * LANE]

    @pl.when(jnp.logical_and(valid, lo != 0))
    def _():
        y = expert_rows()
        row = lax.broadcasted_iota(jnp.int32, (MOE_TILE, LANE), 0)
        keep = jnp.logical_and(row >= lo, row < hi)
        for sb in range(ROW_TILES):
            rows = pl.ds(sb, MOE_TILE, stride=ROW_TILES)
            y_ref[rows, :] = jnp.where(keep, y[:, sb * LANE:(sb + 1) * LANE], y_ref[rows, :])

    @pl.when(w == n_items - 1)
    def _():
        wait_gather(n_tiles % 2)


def _experts(tn, row_token, items, w):
    n_tiles = row_token.shape[0]
    n_items = items[0].shape[0]

    def smem_tile(imap):
        return pl.BlockSpec((1, 1, MOE_TILE), imap, memory_space=pltpu.SMEM)

    grid_spec = pltpu.PrefetchScalarGridSpec(
        num_scalar_prefetch=5,
        grid=(n_items,),
        in_specs=[smem_tile(lambda i, tl, ex, lo, hi, va: (tl[i], 0, 0)),
                  smem_tile(lambda i, tl, ex, lo, hi, va: (jnp.minimum(tl[i] + 1, n_tiles - 1), 0, 0)),
                  pl.BlockSpec(memory_space=pl.ANY),
                  pl.BlockSpec((1, 1, D_MODEL, EXPERT_FF), lambda i, tl, ex, lo, hi, va: (0, ex[i], 0, 0)),
                  pl.BlockSpec((1, 1, D_MODEL, EXPERT_FF), lambda i, tl, ex, lo, hi, va: (0, ex[i], 0, 0)),
                  pl.BlockSpec((1, 1, EXPERT_FF, D_MODEL), lambda i, tl, ex, lo, hi, va: (0, ex[i], 0, 0))],
        out_specs=pl.BlockSpec((MOE_TILE * ROW_TILES, LANE), lambda i, tl, ex, lo, hi, va: (tl[i], 0)),
        scratch_shapes=[pltpu.VMEM((2, MOE_TILE * ROW_TILES, LANE), F32),
                        pltpu.VMEM((MOE_TILE, D_MODEL), BF16),
                        pltpu.VMEM((D_MODEL, 2 * EXPERT_FF), BF16),
                        pltpu.VMEM((EXPERT_FF, D_MODEL), BF16),
                        pltpu.SemaphoreType.DMA((2,))],
    )
    return pl.pallas_call(
        functools.partial(_experts_kernel, n_tiles, n_items),
        grid_spec=grid_spec,
        out_shape=jax.ShapeDtypeStruct((n_tiles * MOE_TILE * ROW_TILES, LANE), F32),
        compiler_params=pltpu.CompilerParams(dimension_semantics=("arbitrary",), vmem_limit_bytes=VMEM_LIMIT),
        name="experts",
    )(*items, row_token, row_token, tn, w["wgate"], w["wup"], w["wdown"])


def _combine_kernel(n_steps, pos_cur_ref, pos_nxt_ref, ys_hbm, gate_ref, h2_ref, gfin_ref, out_ref, ybuf, sem):
    i = pl.program_id(0)
    slot = i % 2

    def wait_gather(s):
        def body(r, carry):
            for k in range(2):
                _token_copy(ys_hbm, 0, ybuf.at[s, k], r, sem.at[s]).wait()
            return carry
        lax.fori_loop(0, COMBINE_TILE, body, 0, unroll=8)

    def start_gather(pos_ref, s):
        def body(r, carry):
            for k in range(2):
                _token_copy(ys_hbm, pos_ref[0, 0, k * COMBINE_TILE + r], ybuf.at[s, k], r,
                            sem.at[s]).start(priority=k)
            return carry
        lax.fori_loop(0, COMBINE_TILE, body, 0, unroll=8)

    @pl.when(i == 0)
    def _():
        start_gather(pos_cur_ref, 0)

    @pl.when(i + 1 < n_steps)
    def _():
        start_gather(pos_nxt_ref, 1 - slot)

    wait_gather(slot)

    gates = gate_ref[...]
    g0 = gates[:, 0:1]
    g1 = gates[:, 1:2]
    ssq = jnp.zeros((COMBINE_TILE, 1), F32)
    for sb in range(ROW_TILES):
        rows = pl.ds(sb, COMBINE_TILE, stride=ROW_TILES)
        cols = pl.ds(sb * LANE, LANE)
        v = h2_ref[:, cols] + (ybuf[slot, 0, rows, :] * g0 + ybuf[slot, 1, rows, :] * g1)
        ssq = ssq + jnp.sum(v * v, axis=-1, keepdims=True)
        out_ref[:, cols] = v
    out_ref[...] = out_ref[...] * lax.rsqrt(ssq * (1.0 / D_MODEL) + EPS) * gfin_ref[...]


def _combine(ys, pos, gates, h2, gfin):
    n_tok = h2.shape[0]
    n_tiles = n_tok // COMBINE_TILE
    pos_tiles = pos.reshape(n_tiles, COMBINE_TILE, 2).transpose(0, 2, 1).reshape(n_tiles, 1, 2 * COMBINE_TILE)

    def smem_tile(imap):
        return pl.BlockSpec((1, 1, 2 * COMBINE_TILE), imap, memory_space=pltpu.SMEM)

    row_spec = pl.BlockSpec((COMBINE_TILE, D_MODEL), lambda i: (i, 0))
    return pl.pallas_call(
        functools.partial(_combine_kernel, n_tiles),
        grid=(n_tiles,),
        in_specs=[smem_tile(lambda i: (i, 0, 0)),
                  smem_tile(lambda i: (jnp.minimum(i + 1, n_tiles - 1), 0, 0)),
                  pl.BlockSpec(memory_space=pl.ANY),
                  pl.BlockSpec((COMBINE_TILE, LANE), lambda i: (i, 0)),
                  row_spec, _const_spec((1, D_MODEL))],
        out_specs=row_spec,
        out_shape=jax.ShapeDtypeStruct((n_tok, D_MODEL), F32),
        scratch_shapes=[pltpu.VMEM((2, 2, COMBINE_TILE * ROW_TILES, LANE), F32), pltpu.SemaphoreType.DMA((2,))],
        compiler_params=pltpu.CompilerParams(dimension_semantics=("arbitrary",), vmem_limit_bytes=VMEM_LIMIT),
        name="combine",
    )(pos_tiles, pos_tiles, ys, gates, h2, gfin)


def _row(v, width=None):
    v = v.astype(F32).reshape(1, -1)
    if width is not None and v.shape[1] < width:
        v = jnp.pad(v, ((0, 0), (0, width - v.shape[1])))
    return v


def _s5_params(a_re, a_im, log_dt, b_re, b_im, c_re, c_im):
    dt = jnp.exp(log_dt)[:, None]
    mag = jnp.exp(a_re * dt)
    lam_re, lam_im = mag * jnp.cos(a_im * dt), mag * jnp.sin(a_im * dt)
    den = a_re * a_re + a_im * a_im
    f_re = ((lam_re - 1.0) * a_re + lam_im * a_im) / den
    f_im = (lam_im * a_re - (lam_re - 1.0) * a_im) / den
    bb_re = f_re[..., None] * b_re - f_im[..., None] * b_im
    bb_im = f_re[..., None] * b_im + f_im[..., None] * b_re
    half = S5_GROUPS // 2

    def in_proj(bb):
        bb = bb.reshape(2, half, S5_STATE, S5_GROUP_CH)
        eye = jnp.eye(half, dtype=F32)
        m = jnp.einsum("hgnc,gk->hgckn", bb, eye)
        return m.reshape(2, half * S5_GROUP_CH, half * S5_STATE)

    wb = jnp.concatenate([in_proj(bb_re), in_proj(bb_im)], axis=2).astype(BF16)

    def out_proj(cc):
        q = S5_GROUPS // 4
        cc = cc.reshape(4, q, S5_GROUP_CH, S5_STATE)
        eye = jnp.eye(q, dtype=F32)
        m = jnp.einsum("qgcn,gk->qgnkc", cc, eye)
        return m.reshape(4, q * S5_STATE, q * S5_GROUP_CH)

    wc = jnp.stack([out_proj(c_re), -out_proj(c_im)], axis=1).astype(BF16)
    return lam_re.reshape(1, S5_LANES), lam_im.reshape(1, S5_LANES), wb, wc


def kernel(x_prompt, x_sample, cache_conv, state_ssd, state_s5_re, state_s5_im, cache_mem_k, cache_mem_v,
           mem_prompt, norm_mix, w_in, conv_w, conv_b, dt_bias, a_log, d_skip, ssd_norm, s5_a_re, s5_a_im,
           s5_log_dt, s5_b_re, s5_b_im, s5_c_re, s5_c_im, s5_d, s5_w_glu, w_out, norm_x, norm_mem, w_q,
           w_k, w_v, w_o, norm_ffn, w_router_group, b_router_group, w_router_expert, b_router_expert,
           w_gate, w_up, w_down, norm_final):
    depth = norm_mix.shape[0]
    assert depth == 1
    l = 0
    bp, seq_p, _ = x_prompt.shape
    bs, seq_s, _ = x_sample.shape

    o1 = SSD_W
    o2 = o1 + CONV_CH
    o3 = o2 + SSD_HEADS
    lam_re, lam_im, wb, wc = _s5_params(s5_a_re[l], s5_a_im[l], s5_log_dt[l], s5_b_re[l], s5_b_im[l],
                                        s5_c_re[l], s5_c_im[l])
    wm = {
        "gmix": _row(norm_mix[l]),
        "wz": w_in[l][:, :o1].astype(BF16),
        "wxbc": w_in[l][:, o1:o2].astype(BF16),
        "wdt": jnp.pad(w_in[l][:, o2:o3], ((0, 0), (0, LANE - SSD_HEADS))).astype(BF16),
        "wu": w_in[l][:, o3:].astype(BF16),
        "convw": conv_w[l].astype(F32), "convb": _row(conv_b[l]),
        "dtb": _row(dt_bias[l], LANE), "aneg": _row(-jnp.exp(a_log[l].astype(F32)), LANE),
        "eexp": (jnp.arange(LANE, dtype=jnp.int32)[:, None]
                 == jnp.arange(SSD_W, dtype=jnp.int32)[None, :] // SSD_HEAD_DIM).astype(BF16),
        "dskip": _row(jnp.repeat(d_skip[l].astype(F32), SSD_HEAD_DIM)), "ssdn": _row(ssd_norm[l]),
        "lamre": lam_re, "lamim": lam_im, "wb": wb, "wc": wc,
        "s5d": _row(s5_d[l]), "wglu": s5_w_glu[l].astype(BF16), "wout": w_out[l].astype(BF16),
    }
    wr = jnp.concatenate([w_router_expert[l].reshape(D_MODEL, N_EXPERTS), w_router_group[l]], axis=1)
    br = jnp.concatenate([b_router_expert[l].reshape(N_EXPERTS), b_router_group[l]])
    wa = {
        "gx": _row(norm_x[l]), "wq": w_q[l].astype(BF16), "wo": w_o[l].astype(BF16),
        "gffn": _row(norm_ffn[l]),
        "wr": jnp.pad(wr, ((0, 0), (0, LANE - wr.shape[1]))).astype(BF16), "br": _row(br, LANE),
    }
    we = {"wgate": w_gate, "wup": w_up, "wdown": w_down}
    gfin = _row(norm_final)

    mk_p, mv_p, mk_state, mv_state = _memkv(mem_prompt, _row(norm_mem[l]), w_k[l].astype(BF16), w_v[l].astype(BF16))

    def group(x, conv0, ssd0, s5re0, s5im0, mem_k, mem_v, attn_seqs, attn_tile):
        nb, seq, _ = x.shape
        n_tok = nb * seq
        h1, conv, ssd, s5re, s5im = _mixer(x, conv0, ssd0, s5re0.reshape(nb, S5_LANES),
                                           s5im0.reshape(nb, S5_LANES), wm)
        h2, tn, eid, gates = _attn(h1, mem_k, mem_v, wa, attn_seqs, attn_tile)
        pos, row_token, items = _route_plan(eid.reshape(n_tok, LANE)[:, :2])
        ys = _experts(tn.reshape(n_tok * ROW_TILES, LANE), row_token, items, we)
        y = _combine(ys, pos, gates.reshape(n_tok, LANE), h2.reshape(n_tok, D_MODEL), gfin)
        return (y.reshape(nb, seq, D_MODEL), conv[None], ssd[None],
                s5re.reshape(1, nb, S5_GROUPS, S5_STATE), s5im.reshape(1, nb, S5_GROUPS, S5_STATE))

    zeros = lambda *s: jnp.zeros(s, F32)
    y_p, conv_p, ssd_p, s5re_p, s5im_p = group(
        x_prompt, zeros(bp, CONV_K - 1, CONV_CH), zeros(bp, SSD_HEADS, SSD_HEAD_DIM, SSD_STATE),
        zeros(bp, S5_GROUPS, S5_STATE), zeros(bp, S5_GROUPS, S5_STATE), mk_p, mv_p,
        1, min(512, seq_p))
    y_s, conv_s, ssd_s, s5re_s, s5im_s = group(
        x_sample, cache_conv.reshape(cache_conv.shape[1:]), state_ssd.reshape(state_ssd.shape[1:]),
        state_s5_re.reshape(state_s5_re.shape[1:]), state_s5_im.reshape(state_s5_im.shape[1:]),
        cache_mem_k, cache_mem_v, SAMPLE_ATTN_SEQS, seq_s)

    return (y_p, y_s, conv_p, ssd_p, s5re_p, s5im_p, mk_state, mv_state, conv_s, ssd_s, s5re_s, s5im_s)
```

```python
import functools
import math

import jax
import jax.numpy as jnp
from jax import lax
from jax.experimental import pallas as pl
from jax.experimental.pallas import tpu as pltpu

F32 = jnp.float32
BF16 = jnp.bfloat16
EPS = 1e-6

D_MODEL = 1024
CHUNK = 64
SEQ_TILE = 8
SSD_W = 512
SSD_HEAD_DIM = 64
SSD_HEADS = 8
SSD_GROUPS = 2
SSD_STATE = 128
CONV_K = 4
CONV_CH = 1024
S5_W = 512
S5_GROUPS = 32
S5_GROUP_CH = 16
S5_STATE = 64
S5_LANES = S5_GROUPS * S5_STATE
MEM_LEN = 256
X_HEADS = 4
X_HEAD_DIM = 256
N_EXPERT_GROUPS = 4
EXPERTS_PER_GROUP = 8
N_EXPERTS = 32
EXPERT_FF = 256
LANE = 128
PAD_ROWS = 8
S5_PITCH = CHUNK + PAD_ROWS
S5_SCAN_TILES = 8
MOE_TILE = 256
COMBINE_TILE = 512
SAMPLE_ATTN_SEQS = 4
ROW_TILES = D_MODEL // LANE
KEY_SHIFT = 20
VMEM_LIMIT = 56 * 1024 * 1024


def _rms(x, g):
    return x * lax.rsqrt(jnp.mean(x * x, axis=-1, keepdims=True) + EPS) * g


def _dot(a, b):
    return jnp.dot(a, b, preferred_element_type=F32)


def _dot_nt(a, b):
    return lax.dot_general(a, b, (((1,), (1,)), ((), ())), preferred_element_type=F32)


def _dot_tn(a, b):
    return lax.dot_general(a, b, (((0,), (0,)), ((), ())), preferred_element_type=F32)


def _const_spec(shape):
    nd = len(shape)
    return pl.BlockSpec(shape, lambda *_: (0,) * nd)


def _memkv_kernel(m_ref, g_ref, wk_ref, wv_ref, k_ref, v_ref, k5_ref, v5_ref):
    mn = _rms(m_ref[0], g_ref[...]).astype(BF16)
    k = _dot(mn, wk_ref[...])
    v = _dot(mn, wv_ref[...])
    k_ref[0] = k
    v_ref[0] = v
    for hd in range(X_HEADS):
        k5_ref[0, 0, :, hd, :] = k[:, hd * X_HEAD_DIM:(hd + 1) * X_HEAD_DIM]
        v5_ref[0, 0, :, hd, :] = v[:, hd * X_HEAD_DIM:(hd + 1) * X_HEAD_DIM]


def _memkv(mem, g, wk, wv):
    nb = mem.shape[0]
    flat_spec = pl.BlockSpec((1, MEM_LEN, D_MODEL), lambda i: (i, 0, 0))
    head_spec = pl.BlockSpec((1, 1, MEM_LEN, X_HEADS, X_HEAD_DIM), lambda i: (0, i, 0, 0, 0))
    flat_shape = jax.ShapeDtypeStruct((nb, MEM_LEN, D_MODEL), F32)
    head_shape = jax.ShapeDtypeStruct((1, nb, MEM_LEN, X_HEADS, X_HEAD_DIM), F32)
    return pl.pallas_call(
        _memkv_kernel,
        grid=(nb,),
        in_specs=[flat_spec, _const_spec((1, D_MODEL)), _const_spec((D_MODEL, D_MODEL)),
                  _const_spec((D_MODEL, D_MODEL))],
        out_specs=[flat_spec, flat_spec, head_spec, head_spec],
        out_shape=[flat_shape, flat_shape, head_shape, head_shape],
        compiler_params=pltpu.CompilerParams(dimension_semantics=("arbitrary",), vmem_limit_bytes=VMEM_LIMIT),
        name="memkv",
    )(mem, g, wk, wv)


def _softplus(x):
    return jnp.maximum(x, 0.0) + jnp.log1p(jnp.exp(-jnp.abs(x)))


def _mixer_kernel(n_chunks, x_ref, conv0_ref, ssd0_ref, s5re0_ref, s5im0_ref,
                  gmix_ref, wz_ref, wxbc_ref, wdt_ref, wu_ref, convw_ref, convb_ref,
                  dtb_ref, aneg_ref, eexp_ref, dskip_ref, ssdn_ref,
                  lamre_ref, lamim_ref, wb_ref, wc_ref, s5d_ref, wglu_ref, wout_ref,
                  h_ref, conv_ref, ssd_ref, s5re_ref, s5im_ref,
                  xn_ref, xpad_ref, xc_ref, dte_ref, cse_ref, crow_ref, st_ref, y_ref, z_ref, u_ref,
                  bure_ref, buim_ref, mix_ref):
    c = pl.program_id(1)
    rows = SEQ_TILE * CHUNK

    @pl.when(c == 0)
    def _():
        xpad_ref[:, PAD_ROWS - (CONV_K - 1):PAD_ROWS, :] = conv0_ref[...]
        s5re_ref[...] = s5re0_ref[...]
        s5im_ref[...] = s5im0_ref[...]

    x = x_ref[...].reshape(rows, D_MODEL)
    xn_ref[...] = _rms(x, gmix_ref[...]).astype(BF16)

    half_ch = S5_W // 2
    half_st = S5_LANES // 2
    half_tiles = half_st // LANE

    def s5_input(hf):
        bu = _dot(u_ref[:, hf * half_ch:(hf + 1) * half_ch].astype(BF16), wb_ref[hf])
        for k in range(half_tiles):
            for b in range(SEQ_TILE):
                dst = pl.ds(b * S5_PITCH, CHUNK)
                src = slice(b * CHUNK, (b + 1) * CHUNK)
                bure_ref[hf * half_tiles + k, dst, :] = bu[src, k * LANE:(k + 1) * LANE]
                buim_ref[hf * half_tiles + k, dst, :] = bu[src, half_st + k * LANE:half_st + (k + 1) * LANE]

    def proj_u():
        u_ref[...] = _dot(xn_ref[...], wu_ref[...])

    def proj_z():
        z_ref[...] = _dot(xn_ref[...], wz_ref[...])

    later_matmuls = [proj_u, functools.partial(s5_input, 0), functools.partial(s5_input, 1), proj_z]
    cw = CONV_CH // len(later_matmuls)
    for cb, matmul in enumerate(later_matmuls):
        ls = pl.ds(cb * cw, cw)
        xpad_ref[:, PAD_ROWS:, ls] = _dot(xn_ref[...], wxbc_ref[:, ls]).reshape(SEQ_TILE, CHUNK, cw)
        matmul()
        acc = convb_ref[:, ls].reshape(1, 1, cw)
        for k in range(CONV_K):
            lo = PAD_ROWS - (CONV_K - 1) + k
            acc = acc + convw_ref[k:k + 1, ls].reshape(1, 1, cw) * xpad_ref[:, lo:lo + CHUNK, ls]
        xc_ref[:, ls] = (acc * jax.nn.sigmoid(acc)).reshape(rows, cw)
    hist = xpad_ref[:, PAD_ROWS + CHUNK - (CONV_K - 1):, :]
    conv_ref[...] = hist
    xpad_ref[:, PAD_ROWS - (CONV_K - 1):PAD_ROWS, :] = hist

    dt = _softplus(_dot(xn_ref[...], wdt_ref[...]) + dtb_ref[...])
    a = dt * aneg_ref[...]
    tpos = lax.broadcasted_iota(jnp.int32, (rows, LANE), 0) % CHUNK
    sh = 1
    while sh < CHUNK:
        a = a + jnp.where(tpos >= sh, pltpu.roll(a, sh, axis=0), 0.0)
        sh *= 2

    def expand_heads(v):
        hi = v.astype(BF16)
        r1 = v - hi.astype(F32)
        mid = r1.astype(BF16)
        lo = (r1 - mid.astype(F32)).astype(BF16)
        e = eexp_ref[...]
        return _dot(hi, e) + _dot(mid, e) + _dot(lo, e)

    dte_ref[...] = expand_heads(dt)
    cse_ref[...] = expand_heads(a)
    for b in range(SEQ_TILE):
        at = a[b * CHUNK:(b + 1) * CHUNK, :].T
        crow_ref[pl.ds(b, 1), :] = jnp.concatenate([at[h:h + 1, :] for h in range(SSD_HEADS)], axis=1)

    @pl.when(c == 0)
    def _():
        for b in range(SEQ_TILE):
            for h in range(SSD_HEADS):
                st_ref[b, :, h * SSD_HEAD_DIM:(h + 1) * SSD_HEAD_DIM] = ssd0_ref[b, h].T

    gw = SSD_W // SSD_GROUPS
    heads_per_group = SSD_HEADS // SSD_GROUPS
    tri = (lax.broadcasted_iota(jnp.int32, (CHUNK, gw), 0)
           >= lax.broadcasted_iota(jnp.int32, (CHUNK, gw), 1) % CHUNK)
    same_head = ((lax.broadcasted_iota(jnp.int32, (gw, gw), 0) // SSD_HEAD_DIM)
                 == (lax.broadcasted_iota(jnp.int32, (gw, gw), 1) // SSD_HEAD_DIM))

    def seq_body(b, carry):
        r0 = pl.multiple_of(b * CHUNK, CHUNK)
        rs = pl.ds(r0, CHUNK)
        for g in range(SSD_GROUPS):
            ls = pl.ds(g * gw, gw)
            cse = cse_ref[rs, ls]
            cs_last = cse_ref[pl.ds(r0 + CHUNK - 1, 1), ls]
            decay = jnp.exp(jnp.where(tri, cse - crow_ref[pl.ds(b, 1), ls], -jnp.inf))
            xs = xc_ref[rs, ls]
            xdt = xs * dte_ref[rs, ls]
            b_bf = xc_ref[rs, pl.ds(SSD_W + g * SSD_STATE, SSD_STATE)].astype(BF16)
            c_bf = xc_ref[rs, pl.ds(SSD_W + SSD_GROUPS * SSD_STATE + g * SSD_STATE, SSD_STATE)].astype(BF16)
            gram = _dot_nt(c_bf, jnp.concatenate([b_bf] * heads_per_group, axis=0))
            xbd = jnp.where(same_head, jnp.concatenate([xdt] * heads_per_group, axis=0), 0.0).astype(BF16)
            y = _dot((gram * decay).astype(BF16), xbd)
            st = st_ref[b, :, ls]
            y = y + _dot(c_bf, st.astype(BF16)) * jnp.exp(cse)
            y = y + dskip_ref[:, ls] * xs
            y_ref[rs, ls] = y
            upd = _dot_tn(b_bf, (xdt * jnp.exp(cs_last - cse)).astype(BF16))
            st_ref[b, :, ls] = st * jnp.exp(cs_last) + upd
        return carry

    lax.fori_loop(0, SEQ_TILE, seq_body, 0)

    @pl.when(c == n_chunks - 1)
    def _():
        for b in range(SEQ_TILE):
            for h in range(SSD_HEADS):
                ssd_ref[b, h] = st_ref[b, :, h * SSD_HEAD_DIM:(h + 1) * SSD_HEAD_DIM].T

    for j in range(S5_LANES // LANE // S5_SCAN_TILES):
        tiles = [j * S5_SCAN_TILES + k for k in range(S5_SCAN_TILES)]
        lr = [jnp.broadcast_to(lamre_ref[:, pl.ds(k * LANE, LANE)], (SEQ_TILE, LANE)) for k in tiles]
        li = [jnp.broadcast_to(lamim_ref[:, pl.ds(k * LANE, LANE)], (SEQ_TILE, LANE)) for k in tiles]

        def step(t, state, tiles=tiles, lr=lr, li=li):
            sr, si = state
            ts = pl.ds(t, SEQ_TILE, stride=S5_PITCH)
            nrs, nis = [], []
            for q, k in enumerate(tiles):
                nr = lr[q] * sr[q] - li[q] * si[q] + bure_ref[k, ts, :]
                ni = lr[q] * si[q] + li[q] * sr[q] + buim_ref[k, ts, :]
                bure_ref[k, ts, :] = nr
                buim_ref[k, ts, :] = ni
                nrs.append(nr)
                nis.append(ni)
            return tuple(nrs), tuple(nis)

        sr, si = lax.fori_loop(
            0, CHUNK, step,
            (tuple(s5re_ref[:, pl.ds(k * LANE, LANE)] for k in tiles),
             tuple(s5im_ref[:, pl.ds(k * LANE, LANE)] for k in tiles)))
        for q, k in enumerate(tiles):
            s5re_ref[:, pl.ds(k * LANE, LANE)] = sr[q]
            s5im_ref[:, pl.ds(k * LANE, LANE)] = si[q]
    def seq_rows(ref, k):
        return jnp.concatenate([ref[k, pl.ds(b * S5_PITCH, CHUNK), :] for b in range(SEQ_TILE)], axis=0)

    n_blk = S5_W // LANE
    blk_rows = rows // n_blk
    for j in range(n_blk):
        s_re = jnp.concatenate([seq_rows(bure_ref, 4 * j + k) for k in range(4)], axis=1)
        s_im = jnp.concatenate([seq_rows(buim_ref, 4 * j + k) for k in range(4)], axis=1)
        yj = _dot(s_re.astype(BF16), wc_ref[j, 0]) + _dot(s_im.astype(BF16), wc_ref[j, 1])
        rs = pl.ds(j * blk_rows, blk_rows)
        z = z_ref[rs, :]
        y = y_ref[rs, :] * (z * jax.nn.sigmoid(z))
        mix_ref[rs, :] = _rms(y, ssdn_ref[...]).astype(BF16)
        ls = pl.ds(j * LANE, LANE)
        u_ref[:, ls] = jax.nn.gelu(yj + s5d_ref[:, ls] * u_ref[:, ls])
    h_ssd = _dot(mix_ref[...], wout_ref[0:SSD_W, :])
    y5 = u_ref[...]
    y5 = y5 * jax.nn.sigmoid(_dot(y5.astype(BF16), wglu_ref[...]))
    h = x_ref[...].reshape(rows, D_MODEL) + (h_ssd + _dot(y5.astype(BF16), wout_ref[SSD_W:, :]))
    h_ref[...] = h.reshape(SEQ_TILE, CHUNK, D_MODEL)


def _mixer(x, conv0, ssd0, s5re0, s5im0, w):
    nb, seq, _ = x.shape
    grid = (nb // SEQ_TILE, seq // CHUNK)
    rows = SEQ_TILE * CHUNK
    weights = [w["gmix"], w["wz"], w["wxbc"], w["wdt"], w["wu"], w["convw"], w["convb"],
               w["dtb"], w["aneg"], w["eexp"], w["dskip"], w["ssdn"],
               w["lamre"], w["lamim"], w["wb"], w["wc"], w["s5d"], w["wglu"], w["wout"]]
    state_specs = [pl.BlockSpec((SEQ_TILE, CONV_K - 1, CONV_CH), lambda i, c: (i, 0, 0)),
                   pl.BlockSpec((SEQ_TILE, SSD_HEADS, SSD_HEAD_DIM, SSD_STATE), lambda i, c: (i, 0, 0, 0)),
                   pl.BlockSpec((SEQ_TILE, S5_LANES), lambda i, c: (i, 0)),
                   pl.BlockSpec((SEQ_TILE, S5_LANES), lambda i, c: (i, 0))]
    x_spec = pl.BlockSpec((SEQ_TILE, CHUNK, D_MODEL), lambda i, c: (i, c, 0))
    return pl.pallas_call(
        functools.partial(_mixer_kernel, grid[1]),
        grid=grid,
        in_specs=[x_spec] + state_specs + [_const_spec(a.shape) for a in weights],
        out_specs=[x_spec] + state_specs,
        out_shape=[jax.ShapeDtypeStruct(x.shape, F32),
                   jax.ShapeDtypeStruct(conv0.shape, F32), jax.ShapeDtypeStruct(ssd0.shape, F32),
                   jax.ShapeDtypeStruct(s5re0.shape, F32), jax.ShapeDtypeStruct(s5im0.shape, F32)],
        scratch_shapes=[
            pltpu.VMEM((rows, D_MODEL), BF16),
            pltpu.VMEM((SEQ_TILE, PAD_ROWS + CHUNK, CONV_CH), F32),
            pltpu.VMEM((rows, CONV_CH), F32),
            pltpu.VMEM((rows, SSD_W), F32),
            pltpu.VMEM((rows, SSD_W), F32),
            pltpu.VMEM((SEQ_TILE, SSD_W), F32),
            pltpu.VMEM((SEQ_TILE, SSD_STATE, SSD_W), F32),
            pltpu.VMEM((rows, SSD_W), F32),
            pltpu.VMEM((rows, SSD_W), F32),
            pltpu.VMEM((rows, S5_W), F32),
            pltpu.VMEM((S5_LANES // LANE, SEQ_TILE * S5_PITCH, LANE), F32),
            pltpu.VMEM((S5_LANES // LANE, SEQ_TILE * S5_PITCH, LANE), F32),
            pltpu.VMEM((rows, SSD_W), BF16),
        ],
        compiler_params=pltpu.CompilerParams(dimension_semantics=("arbitrary", "arbitrary"),
                                             vmem_limit_bytes=VMEM_LIMIT),
        name="mixer",
    )(x, conv0, ssd0, s5re0, s5im0, *weights)


def _attn_kernel(n_seq, tile, heads_axis, h_ref, k_ref, v_ref, gx_ref, wq_ref, wo_ref, gffn_ref, wr_ref, br_ref,
                 h2_ref, tn_ref, eid_ref, gate_ref, o_ref, kv_ref):
    rows = n_seq * tile
    h1 = h_ref[...].reshape(rows, D_MODEL)
    xn = _rms(h1, gx_ref[...]).astype(BF16)
    q = _dot(xn, wq_ref[...])
    scale = X_HEAD_DIM ** -0.5
    for sq in range(n_seq):
        rs = slice(sq * tile, (sq + 1) * tile)
        for hd in range(X_HEADS):
            ls = slice(hd * X_HEAD_DIM, (hd + 1) * X_HEAD_DIM)
            if heads_axis:
                kv_ref[0] = k_ref[0, sq, :, hd, :]
                kv_ref[1] = v_ref[0, sq, :, hd, :]
                kh = kv_ref[0].astype(BF16)
                vh = kv_ref[1].astype(BF16)
            else:
                kh = k_ref[sq, :, ls].astype(BF16)
                vh = v_ref[sq, :, ls].astype(BF16)
            s = _dot_nt(q[rs, ls].astype(BF16), kh) * scale
            s = s - jnp.max(s, axis=-1, keepdims=True)
            p = jnp.exp(s)
            p = p / jnp.sum(p, axis=-1, keepdims=True)
            o_ref[rs, ls] = _dot(p.astype(BF16), vh).astype(BF16)
    h2 = h1 + _dot(o_ref[...], wo_ref[...])
    h2_ref[...] = h2.reshape(n_seq, tile, D_MODEL)

    tn = _rms(h2, gffn_ref[...]).astype(BF16)
    tn32 = tn.astype(F32)
    for sq in range(n_seq):
        for sb in range(ROW_TILES):
            tn_ref[sq, pl.ds(sb, tile, stride=ROW_TILES), :] = tn32[sq * tile:(sq + 1) * tile,
                                                                  sb * LANE:(sb + 1) * LANE]
    lt = (_dot(tn, wr_ref[...]) + br_ref[...]).T
    sub = lax.broadcasted_iota(jnp.int32, (EXPERTS_PER_GROUP, rows), 0)
    big = jnp.int32(2 ** 30)
    neg = -jnp.inf

    def first_max(x):
        m = jnp.max(x, axis=0, keepdims=True)
        return m, jnp.min(jnp.where(x == m, sub, big), axis=0, keepdims=True)

    gl = jnp.where(sub < N_EXPERT_GROUPS, lt[N_EXPERTS:N_EXPERTS + EXPERTS_PER_GROUP, :], neg)
    gmax, g_idx = first_max(gl)
    g_prob = 1.0 / jnp.sum(jnp.exp(gl - gmax), axis=0, keepdims=True)
    el = lt[0:EXPERTS_PER_GROUP, :]
    for g in range(1, N_EXPERT_GROUPS):
        el = jnp.where(g_idx == g, lt[g * EXPERTS_PER_GROUP:(g + 1) * EXPERTS_PER_GROUP, :], el)
    m1, i1 = first_max(el)
    m2, i2 = first_max(jnp.where(sub == i1, neg, el))
    e2 = jnp.exp(m2 - m1)
    den = 1.0 + e2
    base = g_idx * EXPERTS_PER_GROUP
    zeros = jnp.zeros((LANE - EXPERTS_PER_GROUP, rows), F32)
    eid_t = jnp.where(sub == 0, base + i1, jnp.where(sub == 1, base + i2, 0)).astype(F32)
    gate_t = jnp.where(sub == 0, (1.0 / den) * g_prob, jnp.where(sub == 1, (e2 / den) * g_prob, 0.0))
    eid = jnp.concatenate([eid_t, zeros], axis=0).T.astype(jnp.int32)
    gate = jnp.concatenate([gate_t, zeros], axis=0).T
    eid_ref[...] = eid.reshape(n_seq, tile, LANE)
    gate_ref[...] = gate.reshape(n_seq, tile, LANE)


def _attn(h1, mem_k, mem_v, w, n_seq, tile):
    nb, seq, _ = h1.shape
    heads_axis = mem_k.ndim == 5
    row_spec = pl.BlockSpec((n_seq, tile, D_MODEL), lambda b, i: (b, i, 0))
    if heads_axis:
        kv_spec = pl.BlockSpec((1, n_seq, MEM_LEN, X_HEADS, X_HEAD_DIM), lambda b, i: (0, b, 0, 0, 0))
    else:
        kv_spec = pl.BlockSpec((n_seq, MEM_LEN, D_MODEL), lambda b, i: (b, 0, 0))
    lane_spec = pl.BlockSpec((n_seq, tile, LANE), lambda b, i: (b, i, 0))
    weights = [w["gx"], w["wq"], w["wo"], w["gffn"], w["wr"], w["br"]]
    return pl.pallas_call(
        functools.partial(_attn_kernel, n_seq, tile, heads_axis),
        grid=(nb // n_seq, seq // tile),
        in_specs=[row_spec, kv_spec, kv_spec] + [_const_spec(a.shape) for a in weights],
        out_specs=[row_spec, pl.BlockSpec((n_seq, tile * ROW_TILES, LANE), lambda b, i: (b, i, 0)),
                   lane_spec, lane_spec],
        out_shape=[jax.ShapeDtypeStruct(h1.shape, F32),
                   jax.ShapeDtypeStruct((nb, seq * ROW_TILES, LANE), F32),
                   jax.ShapeDtypeStruct((nb, seq, LANE), jnp.int32),
                   jax.ShapeDtypeStruct((nb, seq, LANE), F32)],
        scratch_shapes=[pltpu.VMEM((n_seq * tile, D_MODEL), BF16),
                        pltpu.VMEM((2, MEM_LEN, X_HEAD_DIM), F32)],
        compiler_params=pltpu.CompilerParams(dimension_semantics=("arbitrary", "arbitrary"),
                                             vmem_limit_bytes=VMEM_LIMIT),
        name="attn_router",
    )(h1, mem_k, mem_v, *weights)


def _route_plan(eid):
    n_tok = eid.shape[0]
    n_asg = 2 * n_tok
    assert n_asg % MOE_TILE == 0 and n_asg < (1 << KEY_SHIFT)
    n_tiles = n_asg // MOE_TILE
    n_items = n_tiles + N_EXPERTS
    i32 = jnp.int32
    e_flat = eid.reshape(n_asg)
    a_idx = jnp.arange(n_asg, dtype=i32)
    keys = lax.sort(e_flat * (1 << KEY_SHIFT) + a_idx)
    order = keys & ((1 << KEY_SHIFT) - 1)
    row_token = (order // 2).reshape(n_tiles, 1, MOE_TILE)
    pos = lax.sort((order, a_idx), num_keys=1)[1]
    sorted_e = keys >> KEY_SHIFT
    seg_end = jnp.sum((sorted_e[None, :] <= jnp.arange(N_EXPERTS, dtype=i32)[:, None]).astype(i32), axis=1)
    counts = seg_end - jnp.concatenate([jnp.zeros((1,), i32), seg_end[:-1]])
    seg_start = seg_end - counts
    first_tile = seg_start // MOE_TILE
    last_tile = (seg_end - 1) // MOE_TILE
    items_e = jnp.where(counts > 0, last_tile - first_tile + 1, 0)
    it_end = jnp.cumsum(items_e)
    it_start = it_end - items_e
    w = jnp.arange(n_items, dtype=i32)
    wc = jnp.minimum(w, it_end[-1] - 1)
    it_expert = jnp.sum((wc[:, None] >= it_end[None, :]).astype(i32), axis=1)
    it_onehot = (it_expert[:, None] == jnp.arange(N_EXPERTS, dtype=i32)[None, :]).astype(i32)

    def of_item(table):
        return jnp.sum(it_onehot * table[None, :], axis=1)

    it_tile = of_item(first_tile) + (wc - of_item(it_start))
    it_lo = jnp.clip(of_item(seg_start) - it_tile * MOE_TILE, 0, MOE_TILE)
    it_hi = jnp.clip(of_item(seg_end) - it_tile * MOE_TILE, 0, MOE_TILE)
    it_valid = (w < it_end[-1]).astype(i32)
    items = tuple(a.astype(i32) for a in (it_tile, it_expert, it_lo, it_hi, it_valid))
    return pos.reshape(n_tok, 2), row_token, items


def _token_copy(src_hbm, src_row, dst, dst_row, sem):
    return pltpu.make_async_copy(src_hbm.at[pl.ds(pl.multiple_of(src_row * ROW_TILES, ROW_TILES), ROW_TILES), :],
                                 dst.at[pl.ds(pl.multiple_of(dst_row * ROW_TILES, ROW_TILES), ROW_TILES), :], sem)


def _experts_kernel(n_tiles, n_items, tile_ref, exp_ref, lo_ref, hi_ref, valid_ref, rt_cur_ref, rt_nxt_ref, tn_hbm,
                    wg_ref, wu_ref, wd_ref, y_ref, xbuf, xb_ref, wgu_bf, wd_bf, sem):
    w = pl.program_id(0)
    j = tile_ref[w]
    lo = lo_ref[w]
    hi = hi_ref[w]
    valid = valid_ref[w] == 1
    first = jnp.logical_and(valid, lo == 0)
    slot = j % 2

    @pl.when(jnp.logical_or(w == 0, exp_ref[w] != exp_ref[jnp.maximum(w - 1, 0)]))
    def _():
        wgu_bf[:, :EXPERT_FF] = wg_ref[0, 0].astype(BF16)
        wgu_bf[:, EXPERT_FF:] = wu_ref[0, 0].astype(BF16)
        wd_bf[...] = wd_ref[0, 0].astype(BF16)

    def wait_gather(s):
        def body(r, carry):
            _token_copy(tn_hbm, 0, xbuf.at[s], r, sem.at[s]).wait()
            return carry
        lax.fori_loop(0, MOE_TILE, body, 0, unroll=8)

    def expert_rows():
        gu = _dot(xb_ref[...], wgu_bf[...])
        gate = gu[:, :EXPERT_FF]
        hid = (gate * jax.nn.sigmoid(gate)) * gu[:, EXPERT_FF:]
        return _dot(hid.astype(BF16), wd_bf[...])

    def start_gather(rt_ref, s):
        def body(r2, carry):
            for p in range(2):
                r = 2 * r2 + p
                _token_copy(tn_hbm, rt_ref[0, 0, r], xbuf.at[s], r, sem.at[s]).start(priority=p)
            return carry
        lax.fori_loop(0, MOE_TILE // 2, body, 0, unroll=4)

    @pl.when(w == 0)
    def _():
        start_gather(rt_cur_ref, 0)

    @pl.when(first)
    def _():
        wait_gather(slot)
        for sb in range(ROW_TILES):
            xb_ref[:, sb * LANE:(sb + 1) * LANE] = xbuf[slot, pl.ds(sb, MOE_TILE, stride=ROW_TILES), :].astype(BF16)
        for r in range(MOE_TILE):
            _token_copy(tn_hbm, rt_nxt_ref[0, 0, r], xbuf.at[1 - slot], r, sem.at[1 - slot]).start(priority=r % 2)
        y = expert_rows()
        for sb in range(ROW_TILES):
            y_ref[pl.ds(sb, MOE_TILE, stride=ROW_TILES), :] = y[:, sb * LANE:(sb + 1) * LANE]

    @pl.when(jnp.logical_and(valid, lo != 0))
    def _():
        y = expert_rows()
        row = lax.broadcasted_iota(jnp.int32, (MOE_TILE, LANE), 0)
        keep = jnp.logical_and(row >= lo, row < hi)
        for sb in range(ROW_TILES):
            rows = pl.ds(sb, MOE_TILE, stride=ROW_TILES)
            y_ref[rows, :] = jnp.where(keep, y[:, sb * LANE:(sb + 1) * LANE], y_ref[rows, :])

    @pl.when(w == n_items - 1)
    def _():
        wait_gather(n_tiles % 2)


def _experts(tn, row_token, items, w):
    n_tiles = row_token.shape[0]
    n_items = items[0].shape[0]

    def smem_tile(imap):
        return pl.BlockSpec((1, 1, MOE_TILE), imap, memory_space=pltpu.SMEM)

    grid_spec = pltpu.PrefetchScalarGridSpec(
        num_scalar_prefetch=5,
        grid=(n_items,),
        in_specs=[smem_tile(lambda i, tl, ex, lo, hi, va: (tl[i], 0, 0)),
                  smem_tile(lambda i, tl, ex, lo, hi, va: (jnp.minimum(tl[i] + 1, n_tiles - 1), 0, 0)),
                  pl.BlockSpec(memory_space=pl.ANY),
                  pl.BlockSpec((1, 1, D_MODEL, EXPERT_FF), lambda i, tl, ex, lo, hi, va: (0, ex[i], 0, 0)),
                  pl.BlockSpec((1, 1, D_MODEL, EXPERT_FF), lambda i, tl, ex, lo, hi, va: (0, ex[i], 0, 0)),
                  pl.BlockSpec((1, 1, EXPERT_FF, D_MODEL), lambda i, tl, ex, lo, hi, va: (0, ex[i], 0, 0))],
        out_specs=pl.BlockSpec((MOE_TILE * ROW_TILES, LANE), lambda i, tl, ex, lo, hi, va: (tl[i], 0)),
        scratch_shapes=[pltpu.VMEM((2, MOE_TILE * ROW_TILES, LANE), F32),
                        pltpu.VMEM((MOE_TILE, D_MODEL), BF16),
                        pltpu.VMEM((D_MODEL, 2 * EXPERT_FF), BF16),
                        pltpu.VMEM((EXPERT_FF, D_MODEL), BF16),
                        pltpu.SemaphoreType.DMA((2,))],
    )
    return pl.pallas_call(
        functools.partial(_experts_kernel, n_tiles, n_items),
        grid_spec=grid_spec,
        out_shape=jax.ShapeDtypeStruct((n_tiles * MOE_TILE * ROW_TILES, LANE), F32),
        compiler_params=pltpu.CompilerParams(dimension_semantics=("arbitrary",), vmem_limit_bytes=VMEM_LIMIT),
        name="experts",
    )(*items, row_token, row_token, tn, w["wgate"], w["wup"], w["wdown"])


def _combine_kernel(n_steps, pos_cur_ref, pos_nxt_ref, ys_hbm, gate_ref, h2_ref, gfin_ref, out_ref, ybuf, sem):
    i = pl.program_id(0)
    slot = i % 2

    def wait_gather(s):
        def body(r, carry):
            for k in range(2):
                _token_copy(ys_hbm, 0, ybuf.at[s, k], r, sem.at[s]).wait()
            return carry
        lax.fori_loop(0, COMBINE_TILE, body, 0, unroll=8)

    def start_gather(pos_ref, s):
        def body(r, carry):
            for k in range(2):
                _token_copy(ys_hbm, pos_ref[0, 0, k * COMBINE_TILE + r], ybuf.at[s, k], r,
                            sem.at[s]).start(priority=k)
            return carry
        lax.fori_loop(0, COMBINE_TILE, body, 0, unroll=8)

    @pl.when(i == 0)
    def _():
        start_gather(pos_cur_ref, 0)

    @pl.when(i + 1 < n_steps)
    def _():
        start_gather(pos_nxt_ref, 1 - slot)

    wait_gather(slot)

    gates = gate_ref[...]
    g0 = gates[:, 0:1]
    g1 = gates[:, 1:2]
    ssq = jnp.zeros((COMBINE_TILE, 1), F32)
    for sb in range(ROW_TILES):
        rows = pl.ds(sb, COMBINE_TILE, stride=ROW_TILES)
        cols = pl.ds(sb * LANE, LANE)
        v = h2_ref[:, cols] + (ybuf[slot, 0, rows, :] * g0 + ybuf[slot, 1, rows, :] * g1)
        ssq = ssq + jnp.sum(v * v, axis=-1, keepdims=True)
        out_ref[:, cols] = v
    out_ref[...] = out_ref[...] * lax.rsqrt(ssq * (1.0 / D_MODEL) + EPS) * gfin_ref[...]


def _combine(ys, pos, gates, h2, gfin):
    n_tok = h2.shape[0]
    n_tiles = n_tok // COMBINE_TILE
    pos_tiles = pos.reshape(n_tiles, COMBINE_TILE, 2).transpose(0, 2, 1).reshape(n_tiles, 1, 2 * COMBINE_TILE)

    def smem_tile(imap):
        return pl.BlockSpec((1, 1, 2 * COMBINE_TILE), imap, memory_space=pltpu.SMEM)

    row_spec = pl.BlockSpec((COMBINE_TILE, D_MODEL), lambda i: (i, 0))
    return pl.pallas_call(
        functools.partial(_combine_kernel, n_tiles),
        grid=(n_tiles,),
        in_specs=[smem_tile(lambda i: (i, 0, 0)),
                  smem_tile(lambda i: (jnp.minimum(i + 1, n_tiles - 1), 0, 0)),
                  pl.BlockSpec(memory_space=pl.ANY),
                  pl.BlockSpec((COMBINE_TILE, LANE), lambda i: (i, 0)),
                  row_spec, _const_spec((1, D_MODEL))],
        out_specs=row_spec,
        out_shape=jax.ShapeDtypeStruct((n_tok, D_MODEL), F32),
        scratch_shapes=[pltpu.VMEM((2, 2, COMBINE_TILE * ROW_TILES, LANE), F32), pltpu.SemaphoreType.DMA((2,))],
        compiler_params=pltpu.CompilerParams(dimension_semantics=("arbitrary",), vmem_limit_bytes=VMEM_LIMIT),
        name="combine",
    )(pos_tiles, pos_tiles, ys, gates, h2, gfin)


def _row(v, width=None):
    v = v.astype(F32).reshape(1, -1)
    if width is not None and v.shape[1] < width:
        v = jnp.pad(v, ((0, 0), (0, width - v.shape[1])))
    return v


def _s5_params(a_re, a_im, log_dt, b_re, b_im, c_re, c_im):
    dt = jnp.exp(log_dt)[:, None]
    mag = jnp.exp(a_re * dt)
    lam_re, lam_im = mag * jnp.cos(a_im * dt), mag * jnp.sin(a_im * dt)
    den = a_re * a_re + a_im * a_im
    f_re = ((lam_re - 1.0) * a_re + lam_im * a_im) / den
    f_im = (lam_im * a_re - (lam_re - 1.0) * a_im) / den
    bb_re = f_re[..., None] * b_re - f_im[..., None] * b_im
    bb_im = f_re[..., None] * b_im + f_im[..., None] * b_re
    half = S5_GROUPS // 2

    def in_proj(bb):
        bb = bb.reshape(2, half, S5_STATE, S5_GROUP_CH)
        eye = jnp.eye(half, dtype=F32)
        m = jnp.einsum("hgnc,gk->hgckn", bb, eye)
        return m.reshape(2, half * S5_GROUP_CH, half * S5_STATE)

    wb = jnp.concatenate([in_proj(bb_re), in_proj(bb_im)], axis=2).astype(BF16)

    def out_proj(cc):
        q = S5_GROUPS // 4
        cc = cc.reshape(4, q, S5_GROUP_CH, S5_STATE)
        eye = jnp.eye(q, dtype=F32)
        m = jnp.einsum("qgcn,gk->qgnkc", cc, eye)
        return m.reshape(4, q * S5_STATE, q * S5_GROUP_CH)

    wc = jnp.stack([out_proj(c_re), -out_proj(c_im)], axis=1).astype(BF16)
    return lam_re.reshape(1, S5_LANES), lam_im.reshape(1, S5_LANES), wb, wc


def kernel(x_prompt, x_sample, cache_conv, state_ssd, state_s5_re, state_s5_im, cache_mem_k, cache_mem_v,
           mem_prompt, norm_mix, w_in, conv_w, conv_b, dt_bias, a_log, d_skip, ssd_norm, s5_a_re, s5_a_im,
           s5_log_dt, s5_b_re, s5_b_im, s5_c_re, s5_c_im, s5_d, s5_w_glu, w_out, norm_x, norm_mem, w_q,
           w_k, w_v, w_o, norm_ffn, w_router_group, b_router_group, w_router_expert, b_router_expert,
           w_gate, w_up, w_down, norm_final):
    depth = norm_mix.shape[0]
    assert depth == 1
    l = 0
    bp, seq_p, _ = x_prompt.shape
    bs, seq_s, _ = x_sample.shape

    o1 = SSD_W
    o2 = o1 + CONV_CH
    o3 = o2 + SSD_HEADS
    lam_re, lam_im, wb, wc = _s5_params(s5_a_re[l], s5_a_im[l], s5_log_dt[l], s5_b_re[l], s5_b_im[l],
                                        s5_c_re[l], s5_c_im[l])
    wm = {
        "gmix": _row(norm_mix[l]),
        "wz": w_in[l][:, :o1].astype(BF16),
        "wxbc": w_in[l][:, o1:o2].astype(BF16),
        "wdt": jnp.pad(w_in[l][:, o2:o3], ((0, 0), (0, LANE - SSD_HEADS))).astype(BF16),
        "wu": w_in[l][:, o3:].astype(BF16),
        "convw": conv_w[l].astype(F32), "convb": _row(conv_b[l]),
        "dtb": _row(dt_bias[l], LANE), "aneg": _row(-jnp.exp(a_log[l].astype(F32)), LANE),
        "eexp": (jnp.arange(LANE, dtype=jnp.int32)[:, None]
                 == jnp.arange(SSD_W, dtype=jnp.int32)[None, :] // SSD_HEAD_DIM).astype(BF16),
        "dskip": _row(jnp.repeat(d_skip[l].astype(F32), SSD_HEAD_DIM)), "ssdn": _row(ssd_norm[l]),
        "lamre": lam_re, "lamim": lam_im, "wb": wb, "wc": wc,
        "s5d": _row(s5_d[l]), "wglu": s5_w_glu[l].astype(BF16), "wout": w_out[l].astype(BF16),
    }
    wr = jnp.concatenate([w_router_expert[l].reshape(D_MODEL, N_EXPERTS), w_router_group[l]], axis=1)
    br = jnp.concatenate([b_router_expert[l].reshape(N_EXPERTS), b_router_group[l]])
    wa = {
        "gx": _row(norm_x[l]), "wq": w_q[l].astype(BF16), "wo": w_o[l].astype(BF16),
        "gffn": _row(norm_ffn[l]),
        "wr": jnp.pad(wr, ((0, 0), (0, LANE - wr.shape[1]))).astype(BF16), "br": _row(br, LANE),
    }
    we = {"wgate": w_gate, "wup": w_up, "wdown": w_down}
    gfin = _row(norm_final)

    mk_p, mv_p, mk_state, mv_state = _memkv(mem_prompt, _row(norm_mem[l]), w_k[l].astype(BF16), w_v[l].astype(BF16))

    def group(x, conv0, ssd0, s5re0, s5im0, mem_k, mem_v, attn_seqs, attn_tile):
        nb, seq, _ = x.shape
        n_tok = nb * seq
        h1, conv, ssd, s5re, s5im = _mixer(x, conv0, ssd0, s5re0.reshape(nb, S5_LANES),
                                           s5im0.reshape(nb, S5_LANES), wm)
        h2, tn, eid, gates = _attn(h1, mem_k, mem_v, wa, attn_seqs, attn_tile)
        pos, row_token, items = _route_plan(eid.reshape(n_tok, LANE)[:, :2])
        ys = _experts(tn.reshape(n_tok * ROW_TILES, LANE), row_token, items, we)
        y = _combine(ys, pos, gates.reshape(n_tok, LANE), h2.reshape(n_tok, D_MODEL), gfin)
        return (y.reshape(nb, seq, D_MODEL), conv[None], ssd[None],
                s5re.reshape(1, nb, S5_GROUPS, S5_STATE), s5im.reshape(1, nb, S5_GROUPS, S5_STATE))

    zeros = lambda *s: jnp.zeros(s, F32)
    y_p, conv_p, ssd_p, s5re_p, s5im_p = group(
        x_prompt, zeros(bp, CONV_K - 1, CONV_CH), zeros(bp, SSD_HEADS, SSD_HEAD_DIM, SSD_STATE),
        zeros(bp, S5_GROUPS, S5_STATE), zeros(bp, S5_GROUPS, S5_STATE), mk_p, mv_p,
        1, min(512, seq_p))
    y_s, conv_s, ssd_s, s5re_s, s5im_s = group(
        x_sample, cache_conv.reshape(cache_conv.shape[1:]), state_ssd.reshape(state_ssd.shape[1:]),
        state_s5_re.reshape(state_s5_re.shape[1:]), state_s5_im.reshape(state_s5_im.shape[1:]),
        cache_mem_k, cache_mem_v, SAMPLE_ATTN_SEQS, seq_s)

    return (y_p, y_s, conv_p, ssd_p, s5re_p, s5im_p, mk_state, mv_state, conv_s, ssd_s, s5re_s, s5im_s)
```

```python
import functools
import math

import jax
import jax.numpy as jnp
from jax import lax
from jax.experimental import pallas as pl
from jax.experimental.pallas import tpu as pltpu

F32 = jnp.float32
BF16 = jnp.bfloat16
EPS = 1e-6

D_MODEL = 1024
CHUNK = 64
SEQ_TILE = 8
SSD_W = 512
SSD_HEAD_DIM = 64
SSD_HEADS = 8
SSD_GROUPS = 2
SSD_STATE = 128
CONV_K = 4
CONV_CH = 1024
S5_W = 512
S5_GROUPS = 32
S5_GROUP_CH = 16
S5_STATE = 64
S5_LANES = S5_GROUPS * S5_STATE
MEM_LEN = 256
X_HEADS = 4
X_HEAD_DIM = 256
N_EXPERT_GROUPS = 4
EXPERTS_PER_GROUP = 8
N_EXPERTS = 32
EXPERT_FF = 256
LANE = 128
PAD_ROWS = 8
S5_PITCH = CHUNK + PAD_ROWS
S5_SCAN_TILES = 4
PROMPT_EXPERT_ROWS = 256
SAMPLE_EXPERT_ROWS = 256
COMBINE_TILE = 512
SAMPLE_ATTN_SEQS = 4
ROW_TILES = D_MODEL // LANE
KEY_SHIFT = 20
VMEM_LIMIT = 56 * 1024 * 1024


def _rms(x, g):
    return x * lax.rsqrt(jnp.mean(x * x, axis=-1, keepdims=True) + EPS) * g


def _dot(a, b):
    return jnp.dot(a, b, preferred_element_type=F32)


def _dot_nt(a, b):
    return lax.dot_general(a, b, (((1,), (1,)), ((), ())), preferred_element_type=F32)


def _dot_tn(a, b):
    return lax.dot_general(a, b, (((0,), (0,)), ((), ())), preferred_element_type=F32)


def _const_spec(shape):
    nd = len(shape)
    return pl.BlockSpec(shape, lambda *_: (0,) * nd)


def _memkv_kernel(m_ref, g_ref, wk_ref, wv_ref, k_ref, v_ref, k5_ref, v5_ref):
    mn = _rms(m_ref[0], g_ref[...]).astype(BF16)
    k = _dot(mn, wk_ref[...])
    v = _dot(mn, wv_ref[...])
    k_ref[0] = k
    v_ref[0] = v
    for hd in range(X_HEADS):
        k5_ref[0, 0, :, hd, :] = k[:, hd * X_HEAD_DIM:(hd + 1) * X_HEAD_DIM]
        v5_ref[0, 0, :, hd, :] = v[:, hd * X_HEAD_DIM:(hd + 1) * X_HEAD_DIM]


def _memkv(mem, g, wk, wv):
    nb = mem.shape[0]
    flat_spec = pl.BlockSpec((1, MEM_LEN, D_MODEL), lambda i: (i, 0, 0))
    head_spec = pl.BlockSpec((1, 1, MEM_LEN, X_HEADS, X_HEAD_DIM), lambda i: (0, i, 0, 0, 0))
    flat_shape = jax.ShapeDtypeStruct((nb, MEM_LEN, D_MODEL), F32)
    head_shape = jax.ShapeDtypeStruct((1, nb, MEM_LEN, X_HEADS, X_HEAD_DIM), F32)
    return pl.pallas_call(
        _memkv_kernel,
        grid=(nb,),
        in_specs=[flat_spec, _const_spec((1, D_MODEL)), _const_spec((D_MODEL, D_MODEL)),
                  _const_spec((D_MODEL, D_MODEL))],
        out_specs=[flat_spec, flat_spec, head_spec, head_spec],
        out_shape=[flat_shape, flat_shape, head_shape, head_shape],
        compiler_params=pltpu.CompilerParams(dimension_semantics=("arbitrary",), vmem_limit_bytes=VMEM_LIMIT),
        name="memkv",
    )(mem, g, wk, wv)


def _softplus(x):
    return jnp.maximum(x, 0.0) + jnp.log1p(jnp.exp(-jnp.abs(x)))


def _mixer_kernel(n_chunks, x_ref, conv0_ref, ssd0_ref, s5re0_ref, s5im0_ref,
                  gmix_ref, wz_ref, wxbc_ref, wdt_ref, wu_ref, convw_ref, convb_ref,
                  dtb_ref, aneg_ref, eexp_ref, dskip_ref, ssdn_ref,
                  lamre_ref, lamim_ref, wb_ref, wc_ref, s5d_ref, wglu_ref, wout_ref,
                  h_ref, conv_ref, ssd_ref, s5re_ref, s5im_ref,
                  xn_ref, xpad_ref, xc_ref, dte_ref, cse_ref, crow_ref, st_ref, y_ref, z_ref, u_ref,
                  bure_ref, buim_ref, mix_ref):
    c = pl.program_id(1)
    rows = SEQ_TILE * CHUNK

    @pl.when(c == 0)
    def _():
        xpad_ref[:, PAD_ROWS - (CONV_K - 1):PAD_ROWS, :] = conv0_ref[...]
        s5re_ref[...] = s5re0_ref[...]
        s5im_ref[...] = s5im0_ref[...]

    x = x_ref[...].reshape(rows, D_MODEL)
    xn_ref[...] = _rms(x, gmix_ref[...]).astype(BF16)

    half_ch = S5_W // 2
    half_st = S5_LANES // 2
    half_tiles = half_st // LANE

    cw = CONV_CH // 4
    grp = S5_SCAN_TILES * LANE

    def m_xbc(cb):
        ls = pl.ds(cb * cw, cw)
        xpad_ref[:, PAD_ROWS:, ls] = _dot(xn_ref[...], wxbc_ref[:, ls]).reshape(SEQ_TILE, CHUNK, cw)

    def m_proj(dst_ref, w_ref, hf):
        ls = pl.ds(hf * half_ch, half_ch)
        dst_ref[:, ls] = _dot(xn_ref[...], w_ref[:, ls])

    def m_s5_input(g, im):
        hf, q = divmod(g, 2)
        dst_ref = buim_ref if im else bure_ref
        bu = _dot(u_ref[:, hf * half_ch:(hf + 1) * half_ch].astype(BF16),
                  wb_ref[hf, :, im * half_st + q * grp:im * half_st + (q + 1) * grp])
        for k in range(S5_SCAN_TILES):
            for b in range(SEQ_TILE):
                dst_ref[g * S5_SCAN_TILES + k, pl.ds(b * S5_PITCH, CHUNK), :] = bu[b * CHUNK:(b + 1) * CHUNK,
                                                                                  k * LANE:(k + 1) * LANE]

    def v_conv(cb):
        ls = pl.ds(cb * cw, cw)
        acc = convb_ref[:, ls].reshape(1, 1, cw)
        for k in range(CONV_K):
            lo = PAD_ROWS - (CONV_K - 1) + k
            acc = acc + convw_ref[k:k + 1, ls].reshape(1, 1, cw) * xpad_ref[:, lo:lo + CHUNK, ls]
        xc_ref[:, ls] = (acc * jax.nn.sigmoid(acc)).reshape(rows, cw)

    def v_scan(g):
        tiles = [g * S5_SCAN_TILES + k for k in range(S5_SCAN_TILES)]
        lr = [jnp.broadcast_to(lamre_ref[:, pl.ds(k * LANE, LANE)], (SEQ_TILE, LANE)) for k in tiles]
        li = [jnp.broadcast_to(lamim_ref[:, pl.ds(k * LANE, LANE)], (SEQ_TILE, LANE)) for k in tiles]
        sr = [s5re_ref[:, pl.ds(k * LANE, LANE)] for k in tiles]
        si = [s5im_ref[:, pl.ds(k * LANE, LANE)] for k in tiles]
        for t in range(CHUNK):
            ts = pl.ds(t, SEQ_TILE, stride=S5_PITCH)
            for q, k in enumerate(tiles):
                nr = lr[q] * sr[q] - li[q] * si[q] + bure_ref[k, ts, :]
                ni = lr[q] * si[q] + li[q] * sr[q] + buim_ref[k, ts, :]
                sr[q], si[q] = nr, ni
                bure_ref[k, ts, :] = nr
                buim_ref[k, ts, :] = ni
        for q, k in enumerate(tiles):
            s5re_ref[:, pl.ds(k * LANE, LANE)] = sr[q]
            s5im_ref[:, pl.ds(k * LANE, LANE)] = si[q]

    m_xbc(0); m_proj(u_ref, wu_ref, 0); m_proj(u_ref, wu_ref, 1); v_conv(0)
    m_xbc(1); m_s5_input(0, 0); m_s5_input(0, 1); m_s5_input(1, 0); m_s5_input(1, 1); v_conv(1); v_scan(0)
    m_xbc(2); m_s5_input(2, 0); m_s5_input(2, 1); m_s5_input(3, 0); m_s5_input(3, 1); v_conv(2); v_scan(1)
    m_xbc(3); m_proj(z_ref, wz_ref, 0); m_proj(z_ref, wz_ref, 1); v_conv(3); v_scan(2)
    v_scan(3)
    hist = xpad_ref[:, PAD_ROWS + CHUNK - (CONV_K - 1):, :]
    conv_ref[...] = hist
    xpad_ref[:, PAD_ROWS - (CONV_K - 1):PAD_ROWS, :] = hist

    dt = _softplus(_dot(xn_ref[...], wdt_ref[...]) + dtb_ref[...])
    a = dt * aneg_ref[...]
    tpos = lax.broadcasted_iota(jnp.int32, (rows, LANE), 0) % CHUNK
    sh = 1
    while sh < CHUNK:
        a = a + jnp.where(tpos >= sh, pltpu.roll(a, sh, axis=0), 0.0)
        sh *= 2

    def expand_heads(v):
        hi = v.astype(BF16)
        r1 = v - hi.astype(F32)
        mid = r1.astype(BF16)
        lo = (r1 - mid.astype(F32)).astype(BF16)
        e = eexp_ref[...]
        return _dot(hi, e) + _dot(mid, e) + _dot(lo, e)

    dte_ref[...] = expand_heads(dt)
    cse_ref[...] = expand_heads(a)
    for b in range(SEQ_TILE):
        at = a[b * CHUNK:(b + 1) * CHUNK, :].T
        crow_ref[pl.ds(b, 1), :] = jnp.concatenate([at[h:h + 1, :] for h in range(SSD_HEADS)], axis=1)

    @pl.when(c == 0)
    def _():
        for b in range(SEQ_TILE):
            st_ref[b] = ssd0_ref[b].reshape(SSD_W, SSD_STATE).T

    gw = SSD_W // SSD_GROUPS
    heads_per_group = SSD_HEADS // SSD_GROUPS
    tri = (lax.broadcasted_iota(jnp.int32, (CHUNK, gw), 0)
           >= lax.broadcasted_iota(jnp.int32, (CHUNK, gw), 1) % CHUNK)
    same_head = ((lax.broadcasted_iota(jnp.int32, (gw, gw), 0) // SSD_HEAD_DIM)
                 == (lax.broadcasted_iota(jnp.int32, (gw, gw), 1) // SSD_HEAD_DIM))

    def seq_body(b, carry):
        r0 = pl.multiple_of(b * CHUNK, CHUNK)
        rs = pl.ds(r0, CHUNK)
        for g in range(SSD_GROUPS):
            ls = pl.ds(g * gw, gw)
            cse = cse_ref[rs, ls]
            cs_last = cse_ref[pl.ds(r0 + CHUNK - 1, 1), ls]
            decay = jnp.exp(jnp.where(tri, cse - crow_ref[pl.ds(b, 1), ls], -jnp.inf))
            xs = xc_ref[rs, ls]
            xdt = xs * dte_ref[rs, ls]
            b_bf = xc_ref[rs, pl.ds(SSD_W + g * SSD_STATE, SSD_STATE)].astype(BF16)
            c_bf = xc_ref[rs, pl.ds(SSD_W + SSD_GROUPS * SSD_STATE + g * SSD_STATE, SSD_STATE)].astype(BF16)
            gram = _dot_nt(c_bf, jnp.concatenate([b_bf] * heads_per_group, axis=0))
            xbd = jnp.where(same_head, jnp.concatenate([xdt] * heads_per_group, axis=0), 0.0).astype(BF16)
            y = _dot((gram * decay).astype(BF16), xbd)
            st = st_ref[b, :, ls]
            y = y + _dot(c_bf, st.astype(BF16)) * jnp.exp(cse)
            y = y + dskip_ref[:, ls] * xs
            y_ref[rs, ls] = y
            upd = _dot_tn(b_bf, (xdt * jnp.exp(cs_last - cse)).astype(BF16))
            st_ref[b, :, ls] = st * jnp.exp(cs_last) + upd
        return carry

    lax.fori_loop(0, SEQ_TILE, seq_body, 0)

    @pl.when(c == n_chunks - 1)
    def _():
        for b in range(SEQ_TILE):
            ssd_ref[b] = st_ref[b].T.reshape(SSD_HEADS, SSD_HEAD_DIM, SSD_STATE)

    def seq_rows(ref, k):
        return jnp.concatenate([ref[k, pl.ds(b * S5_PITCH, CHUNK), :] for b in range(SEQ_TILE)], axis=0)

    n_blk = S5_W // LANE
    blk_rows = rows // n_blk
    for j in range(n_blk):
        s_re = jnp.concatenate([seq_rows(bure_ref, 4 * j + k) for k in range(4)], axis=1)
        s_im = jnp.concatenate([seq_rows(buim_ref, 4 * j + k) for k in range(4)], axis=1)
        yj = _dot(s_re.astype(BF16), wc_ref[j, 0]) + _dot(s_im.astype(BF16), wc_ref[j, 1])
        rs = pl.ds(j * blk_rows, blk_rows)
        z = z_ref[rs, :]
        y = y_ref[rs, :] * (z * jax.nn.sigmoid(z))
        mix_ref[rs, :] = _rms(y, ssdn_ref[...]).astype(BF16)
        ls = pl.ds(j * LANE, LANE)
        u_ref[:, ls] = jax.nn.gelu(yj + s5d_ref[:, ls] * u_ref[:, ls])
    h_ssd = _dot(mix_ref[...], wout_ref[0:SSD_W, :])
    y5 = u_ref[...]
    y5 = y5 * jax.nn.sigmoid(_dot(y5.astype(BF16), wglu_ref[...]))
    h = x_ref[...].reshape(rows, D_MODEL) + (h_ssd + _dot(y5.astype(BF16), wout_ref[SSD_W:, :]))
    h_ref[...] = h.reshape(SEQ_TILE, CHUNK, D_MODEL)


def _mixer(x, conv0, ssd0, s5re0, s5im0, w):
    nb, seq, _ = x.shape
    grid = (nb // SEQ_TILE, seq // CHUNK)
    rows = SEQ_TILE * CHUNK
    weights = [w["gmix"], w["wz"], w["wxbc"], w["wdt"], w["wu"], w["convw"], w["convb"],
               w["dtb"], w["aneg"], w["eexp"], w["dskip"], w["ssdn"],
               w["lamre"], w["lamim"], w["wb"], w["wc"], w["s5d"], w["wglu"], w["wout"]]
    state_specs = [pl.BlockSpec((SEQ_TILE, CONV_K - 1, CONV_CH), lambda i, c: (i, 0, 0)),
                   pl.BlockSpec((SEQ_TILE, SSD_HEADS, SSD_HEAD_DIM, SSD_STATE), lambda i, c: (i, 0, 0, 0)),
                   pl.BlockSpec((SEQ_TILE, S5_LANES), lambda i, c: (i, 0)),
                   pl.BlockSpec((SEQ_TILE, S5_LANES), lambda i, c: (i, 0))]
    x_spec = pl.BlockSpec((SEQ_TILE, CHUNK, D_MODEL), lambda i, c: (i, c, 0))
    return pl.pallas_call(
        functools.partial(_mixer_kernel, grid[1]),
        grid=grid,
        in_specs=[x_spec] + state_specs + [_const_spec(a.shape) for a in weights],
        out_specs=[x_spec] + state_specs,
        out_shape=[jax.ShapeDtypeStruct(x.shape, F32),
                   jax.ShapeDtypeStruct(conv0.shape, F32), jax.ShapeDtypeStruct(ssd0.shape, F32),
                   jax.ShapeDtypeStruct(s5re0.shape, F32), jax.ShapeDtypeStruct(s5im0.shape, F32)],
        scratch_shapes=[
            pltpu.VMEM((rows, D_MODEL), BF16),
            pltpu.VMEM((SEQ_TILE, PAD_ROWS + CHUNK, CONV_CH), F32),
            pltpu.VMEM((rows, CONV_CH), F32),
            pltpu.VMEM((rows, SSD_W), F32),
            pltpu.VMEM((rows, SSD_W), F32),
            pltpu.VMEM((SEQ_TILE, SSD_W), F32),
            pltpu.VMEM((SEQ_TILE, SSD_STATE, SSD_W), F32),
            pltpu.VMEM((rows, SSD_W), F32),
            pltpu.VMEM((rows, SSD_W), F32),
            pltpu.VMEM((rows, S5_W), F32),
            pltpu.VMEM((S5_LANES // LANE, SEQ_TILE * S5_PITCH, LANE), F32),
            pltpu.VMEM((S5_LANES // LANE, SEQ_TILE * S5_PITCH, LANE), F32),
            pltpu.VMEM((rows, SSD_W), BF16),
        ],
        compiler_params=pltpu.CompilerParams(dimension_semantics=("arbitrary", "arbitrary"),
                                             vmem_limit_bytes=VMEM_LIMIT),
        name="mixer",
    )(x, conv0, ssd0, s5re0, s5im0, *weights)


def _attn_kernel(n_seq, tile, heads_axis, h_ref, k_ref, v_ref, gx_ref, wq_ref, wo_ref, gffn_ref, wr_ref, br_ref,
                 h2_ref, tn_ref, eid_ref, gate_ref, o_ref, kv_ref):
    rows = n_seq * tile
    h1 = h_ref[...].reshape(rows, D_MODEL)
    xn = _rms(h1, gx_ref[...]).astype(BF16)
    q = _dot(xn, wq_ref[...])
    scale = X_HEAD_DIM ** -0.5
    for sq in range(n_seq):
        rs = slice(sq * tile, (sq + 1) * tile)
        for hd in range(X_HEADS):
            ls = slice(hd * X_HEAD_DIM, (hd + 1) * X_HEAD_DIM)
            if heads_axis:
                kv_ref[0] = k_ref[0, sq, :, hd, :]
                kv_ref[1] = v_ref[0, sq, :, hd, :]
                kh = kv_ref[0].astype(BF16)
                vh = kv_ref[1].astype(BF16)
            else:
                kh = k_ref[sq, :, ls].astype(BF16)
                vh = v_ref[sq, :, ls].astype(BF16)
            s = _dot_nt(q[rs, ls].astype(BF16), kh) * scale
            s = s - jnp.max(s, axis=-1, keepdims=True)
            p = jnp.exp(s)
            p = p / jnp.sum(p, axis=-1, keepdims=True)
            o_ref[rs, ls] = _dot(p.astype(BF16), vh).astype(BF16)
    h2 = h1 + _dot(o_ref[...], wo_ref[...])
    h2_ref[...] = h2.reshape(n_seq, tile, D_MODEL)

    tn = _rms(h2, gffn_ref[...]).astype(BF16)
    tn32 = tn.astype(F32)
    for sq in range(n_seq):
        for sb in range(ROW_TILES):
            tn_ref[sq, pl.ds(sb, tile, stride=ROW_TILES), :] = tn32[sq * tile:(sq + 1) * tile,
                                                                  sb * LANE:(sb + 1) * LANE]
    lt = (_dot(tn, wr_ref[...]) + br_ref[...]).T
    sub = lax.broadcasted_iota(jnp.int32, (EXPERTS_PER_GROUP, rows), 0)
    big = jnp.int32(2 ** 30)
    neg = -jnp.inf

    def first_max(x):
        m = jnp.max(x, axis=0, keepdims=True)
        return m, jnp.min(jnp.where(x == m, sub, big), axis=0, keepdims=True)

    gl = jnp.where(sub < N_EXPERT_GROUPS, lt[N_EXPERTS:N_EXPERTS + EXPERTS_PER_GROUP, :], neg)
    gmax, g_idx = first_max(gl)
    g_prob = 1.0 / jnp.sum(jnp.exp(gl - gmax), axis=0, keepdims=True)
    el = lt[0:EXPERTS_PER_GROUP, :]
    for g in range(1, N_EXPERT_GROUPS):
        el = jnp.where(g_idx == g, lt[g * EXPERTS_PER_GROUP:(g + 1) * EXPERTS_PER_GROUP, :], el)
    m1, i1 = first_max(el)
    m2, i2 = first_max(jnp.where(sub == i1, neg, el))
    e2 = jnp.exp(m2 - m1)
    den = 1.0 + e2
    base = g_idx * EXPERTS_PER_GROUP
    zeros = jnp.zeros((LANE - EXPERTS_PER_GROUP, rows), F32)
    eid_t = jnp.where(sub == 0, base + i1, jnp.where(sub == 1, base + i2, 0)).astype(F32)
    gate_t = jnp.where(sub == 0, (1.0 / den) * g_prob, jnp.where(sub == 1, (e2 / den) * g_prob, 0.0))
    eid = jnp.concatenate([eid_t, zeros], axis=0).T.astype(jnp.int32)
    gate = jnp.concatenate([gate_t, zeros], axis=0).T
    eid_ref[...] = eid.reshape(n_seq, tile, LANE)
    gate_ref[...] = gate.reshape(n_seq, tile, LANE)


def _attn(h1, mem_k, mem_v, w, n_seq, tile):
    nb, seq, _ = h1.shape
    heads_axis = mem_k.ndim == 5
    row_spec = pl.BlockSpec((n_seq, tile, D_MODEL), lambda b, i: (b, i, 0))
    if heads_axis:
        kv_spec = pl.BlockSpec((1, n_seq, MEM_LEN, X_HEADS, X_HEAD_DIM), lambda b, i: (0, b, 0, 0, 0))
    else:
        kv_spec = pl.BlockSpec((n_seq, MEM_LEN, D_MODEL), lambda b, i: (b, 0, 0))
    lane_spec = pl.BlockSpec((n_seq, tile, LANE), lambda b, i: (b, i, 0))
    weights = [w["gx"], w["wq"], w["wo"], w["gffn"], w["wr"], w["br"]]
    return pl.pallas_call(
        functools.partial(_attn_kernel, n_seq, tile, heads_axis),
        grid=(nb // n_seq, seq // tile),
        in_specs=[row_spec, kv_spec, kv_spec] + [_const_spec(a.shape) for a in weights],
        out_specs=[row_spec, pl.BlockSpec((n_seq, tile * ROW_TILES, LANE), lambda b, i: (b, i, 0)),
                   lane_spec, lane_spec],
        out_shape=[jax.ShapeDtypeStruct(h1.shape, F32),
                   jax.ShapeDtypeStruct((nb, seq * ROW_TILES, LANE), F32),
                   jax.ShapeDtypeStruct((nb, seq, LANE), jnp.int32),
                   jax.ShapeDtypeStruct((nb, seq, LANE), F32)],
        scratch_shapes=[pltpu.VMEM((n_seq * tile, D_MODEL), BF16),
                        pltpu.VMEM((2, MEM_LEN, X_HEAD_DIM), F32)],
        compiler_params=pltpu.CompilerParams(dimension_semantics=("arbitrary", "arbitrary"),
                                             vmem_limit_bytes=VMEM_LIMIT),
        name="attn_router",
    )(h1, mem_k, mem_v, *weights)


def _route_plan(eid, tile):
    n_tok = eid.shape[0]
    n_asg = 2 * n_tok
    assert n_asg % tile == 0 and n_asg < (1 << KEY_SHIFT)
    n_tiles = n_asg // tile
    n_items = n_tiles + N_EXPERTS
    i32 = jnp.int32
    e_flat = eid.reshape(n_asg)
    a_idx = jnp.arange(n_asg, dtype=i32)
    keys = lax.sort(e_flat * (1 << KEY_SHIFT) + a_idx)
    order = keys & ((1 << KEY_SHIFT) - 1)
    row_token = (order // 2).reshape(n_tiles, 1, tile)
    pos = lax.sort((order, a_idx), num_keys=1)[1]
    sorted_e = keys >> KEY_SHIFT
    seg_end = jnp.sum((sorted_e[None, :] <= jnp.arange(N_EXPERTS, dtype=i32)[:, None]).astype(i32), axis=1)
    counts = seg_end - jnp.concatenate([jnp.zeros((1,), i32), seg_end[:-1]])
    seg_start = seg_end - counts
    first_tile = seg_start // tile
    last_tile = (seg_end - 1) // tile
    items_e = jnp.where(counts > 0, last_tile - first_tile + 1, 0)
    it_end = jnp.cumsum(items_e)
    it_start = it_end - items_e
    w = jnp.arange(n_items, dtype=i32)
    wc = jnp.minimum(w, it_end[-1] - 1)
    it_expert = jnp.sum((wc[:, None] >= it_end[None, :]).astype(i32), axis=1)
    it_onehot = (it_expert[:, None] == jnp.arange(N_EXPERTS, dtype=i32)[None, :]).astype(i32)

    def of_item(table):
        return jnp.sum(it_onehot * table[None, :], axis=1)

    it_tile = of_item(first_tile) + (wc - of_item(it_start))
    it_lo = jnp.clip(of_item(seg_start) - it_tile * tile, 0, tile)
    it_hi = jnp.clip(of_item(seg_end) - it_tile * tile, 0, tile)
    it_valid = (w < it_end[-1]).astype(i32)
    items = tuple(a.astype(i32) for a in (it_tile, it_expert, it_lo, it_hi, it_valid))
    return pos.reshape(n_tok, 2), row_token, items


def _token_copy(src_hbm, src_row, dst, dst_row, sem):
    return pltpu.make_async_copy(src_hbm.at[pl.ds(pl.multiple_of(src_row * ROW_TILES, ROW_TILES), ROW_TILES), :],
                                 dst.at[pl.ds(pl.multiple_of(dst_row * ROW_TILES, ROW_TILES), ROW_TILES), :], sem)


def _experts_kernel(n_tiles, n_items, tile, tile_ref, exp_ref, lo_ref, hi_ref, valid_ref, rt_cur_ref, rt_nxt_ref, tn_hbm,
                    wg_ref, wu_ref, wd_ref, y_ref, xbuf, xb_ref, wgu_bf, wd_bf, sem):
    w = pl.program_id(0)
    j = tile_ref[w]
    lo = lo_ref[w]
    hi = hi_ref[w]
    valid = valid_ref[w] == 1
    first = jnp.logical_and(valid, lo == 0)
    slot = j % 2

    @pl.when(jnp.logical_or(w == 0, exp_ref[w] != exp_ref[jnp.maximum(w - 1, 0)]))
    def _():
        wgu_bf[:, :EXPERT_FF] = wg_ref[0, 0].astype(BF16)
        wgu_bf[:, EXPERT_FF:] = wu_ref[0, 0].astype(BF16)
        wd_bf[...] = wd_ref[0, 0].astype(BF16)

    def wait_gather(s):
        def body(r, carry):
            _token_copy(tn_hbm, 0, xbuf.at[s], r, sem.at[s]).wait()
            return carry
        lax.fori_loop(0, tile, body, 0, unroll=8)

    def expert_rows():
        gu = _dot(xb_ref[...], wgu_bf[...])
        gate = gu[:, :EXPERT_FF]
        hid = (gate * jax.nn.sigmoid(gate)) * gu[:, EXPERT_FF:]
        return _dot(hid.astype(BF16), wd_bf[...])

    def start_gather(rt_ref, s):
        def body(r2, carry):
            for p in range(2):
                r = 2 * r2 + p
                _token_copy(tn_hbm, rt_ref[0, 0, r], xbuf.at[s], r, sem.at[s]).start(priority=p)
            return carry
        lax.fori_loop(0, tile // 2, body, 0, unroll=4)

    @pl.when(w == 0)
    def _():
        start_gather(rt_cur_ref, 0)

    @pl.when(first)
    def _():
        wait_gather(slot)
        for sb in range(ROW_TILES):
            xb_ref[:, sb * LANE:(sb + 1) * LANE] = xbuf[slot, pl.ds(sb, tile, stride=ROW_TILES), :].astype(BF16)
        for r in range(tile):
            _token_copy(tn_hbm, rt_nxt_ref[0, 0, r], xbuf.at[1 - slot], r, sem.at[1 - slot]).start(priority=r % 2)
        y = expert_rows()
        for sb in range(ROW_TILES):
            y_ref[pl.ds(sb, tile, stride=ROW_TILES), :] = y[:, sb * LANE:(sb + 1) * LANE]

    @pl.when(jnp.logical_and(valid, lo != 0))
    def _():
        y = expert_rows()
        row = lax.broadcasted_iota(jnp.int32, (tile, LANE), 0)
        keep = jnp.logical_and(row >= lo, row < hi)
        for sb in range(ROW_TILES):
            rows = pl.ds(sb, tile, stride=ROW_TILES)
            y_ref[rows, :] = jnp.where(keep, y[:, sb * LANE:(sb + 1) * LANE], y_ref[rows, :])

    @pl.when(w == n_items - 1)
    def _():
        wait_gather(n_tiles % 2)


def _experts(tn, row_token, items, w):
    n_tiles, _, tile = row_token.shape
    n_items = items[0].shape[0]

    def smem_tile(imap):
        return pl.BlockSpec((1, 1, tile), imap, memory_space=pltpu.SMEM)

    grid_spec = pltpu.PrefetchScalarGridSpec(
        num_scalar_prefetch=5,
        grid=(n_items,),
        in_specs=[smem_tile(lambda i, tl, ex, lo, hi, va: (tl[i], 0, 0)),
                  smem_tile(lambda i, tl, ex, lo, hi, va: (jnp.minimum(tl[i] + 1, n_tiles - 1), 0, 0)),
                  pl.BlockSpec(memory_space=pl.ANY),
                  pl.BlockSpec((1, 1, D_MODEL, EXPERT_FF), lambda i, tl, ex, lo, hi, va: (0, ex[i], 0, 0)),
                  pl.BlockSpec((1, 1, D_MODEL, EXPERT_FF), lambda i, tl, ex, lo, hi, va: (0, ex[i], 0, 0)),
                  pl.BlockSpec((1, 1, EXPERT_FF, D_MODEL), lambda i, tl, ex, lo, hi, va: (0, ex[i], 0, 0))],
        out_specs=pl.BlockSpec((tile * ROW_TILES, LANE), lambda i, tl, ex, lo, hi, va: (tl[i], 0)),
        scratch_shapes=[pltpu.VMEM((2, tile * ROW_TILES, LANE), F32),
                        pltpu.VMEM((tile, D_MODEL), BF16),
                        pltpu.VMEM((D_MODEL, 2 * EXPERT_FF), BF16),
                        pltpu.VMEM((EXPERT_FF, D_MODEL), BF16),
                        pltpu.SemaphoreType.DMA((2,))],
    )
    return pl.pallas_call(
        functools.partial(_experts_kernel, n_tiles, n_items, tile),
        grid_spec=grid_spec,
        out_shape=jax.ShapeDtypeStruct((n_tiles * tile * ROW_TILES, LANE), F32),
        compiler_params=pltpu.CompilerParams(dimension_semantics=("arbitrary",), vmem_limit_bytes=VMEM_LIMIT),
        name="experts",
    )(*items, row_token, row_token, tn, w["wgate"], w["wup"], w["wdown"])


def _combine_kernel(n_steps, pos_cur_ref, pos_nxt_ref, ys_hbm, gate_ref, h2_ref, gfin_ref, out_ref, ybuf, sem):
    i = pl.program_id(0)
    slot = i % 2

    def wait_gather(s):
        def body(r, carry):
            for k in range(2):
                _token_copy(ys_hbm, 0, ybuf.at[s, k], r, sem.at[s]).wait()
            return carry
        lax.fori_loop(0, COMBINE_TILE, body, 0, unroll=8)

    def start_gather(pos_ref, s):
        def body(r, carry):
            for k in range(2):
                _token_copy(ys_hbm, pos_ref[0, 0, 2 * r + k], ybuf.at[s, k], r,
                            sem.at[s]).start(priority=k)
            return carry
        lax.fori_loop(0, COMBINE_TILE, body, 0, unroll=8)

    @pl.when(i == 0)
    def _():
        start_gather(pos_cur_ref, 0)

    @pl.when(i + 1 < n_steps)
    def _():
        start_gather(pos_nxt_ref, 1 - slot)

    wait_gather(slot)

    gates = gate_ref[...]
    g0 = gates[:, 0:1]
    g1 = gates[:, 1:2]
    ssq = jnp.zeros((COMBINE_TILE, 1), F32)
    for sb in range(ROW_TILES):
        rows = pl.ds(sb, COMBINE_TILE, stride=ROW_TILES)
        cols = pl.ds(sb * LANE, LANE)
        v = h2_ref[:, cols] + (ybuf[slot, 0, rows, :] * g0 + ybuf[slot, 1, rows, :] * g1)
        ssq = ssq + jnp.sum(v * v, axis=-1, keepdims=True)
        out_ref[:, cols] = v
    out_ref[...] = out_ref[...] * lax.rsqrt(ssq * (1.0 / D_MODEL) + EPS) * gfin_ref[...]


def _combine(ys, pos, gates, h2, gfin):
    n_tok = h2.shape[0]
    n_tiles = n_tok // COMBINE_TILE
    pos_tiles = pos.reshape(n_tiles, 1, 2 * COMBINE_TILE)

    def smem_tile(imap):
        return pl.BlockSpec((1, 1, 2 * COMBINE_TILE), imap, memory_space=pltpu.SMEM)

    row_spec = pl.BlockSpec((COMBINE_TILE, D_MODEL), lambda i: (i, 0))
    return pl.pallas_call(
        functools.partial(_combine_kernel, n_tiles),
        grid=(n_tiles,),
        in_specs=[smem_tile(lambda i: (i, 0, 0)),
                  smem_tile(lambda i: (jnp.minimum(i + 1, n_tiles - 1), 0, 0)),
                  pl.BlockSpec(memory_space=pl.ANY),
                  pl.BlockSpec((COMBINE_TILE, LANE), lambda i: (i, 0)),
                  row_spec, _const_spec((1, D_MODEL))],
        out_specs=row_spec,
        out_shape=jax.ShapeDtypeStruct((n_tok, D_MODEL), F32),
        scratch_shapes=[pltpu.VMEM((2, 2, COMBINE_TILE * ROW_TILES, LANE), F32), pltpu.SemaphoreType.DMA((2,))],
        compiler_params=pltpu.CompilerParams(dimension_semantics=("arbitrary",), vmem_limit_bytes=VMEM_LIMIT),
        name="combine",
    )(pos_tiles, pos_tiles, ys, gates, h2, gfin)


def _row(v, width=None):
    v = v.astype(F32).reshape(1, -1)
    if width is not None and v.shape[1] < width:
        v = jnp.pad(v, ((0, 0), (0, width - v.shape[1])))
    return v


def _s5_params(a_re, a_im, log_dt, b_re, b_im, c_re, c_im):
    dt = jnp.exp(log_dt)[:, None]
    mag = jnp.exp(a_re * dt)
    lam_re, lam_im = mag * jnp.cos(a_im * dt), mag * jnp.sin(a_im * dt)
    den = a_re * a_re + a_im * a_im
    f_re = ((lam_re - 1.0) * a_re + lam_im * a_im) / den
    f_im = (lam_im * a_re - (lam_re - 1.0) * a_im) / den
    bb_re = f_re[..., None] * b_re - f_im[..., None] * b_im
    bb_im = f_re[..., None] * b_im + f_im[..., None] * b_re
    half = S5_GROUPS // 2

    def in_proj(bb):
        bb = bb.reshape(2, half, S5_STATE, S5_GROUP_CH)
        eye = jnp.eye(half, dtype=F32)
        m = jnp.einsum("hgnc,gk->hgckn", bb, eye)
        return m.reshape(2, half * S5_GROUP_CH, half * S5_STATE)

    wb = jnp.concatenate([in_proj(bb_re), in_proj(bb_im)], axis=2).astype(BF16)

    def out_proj(cc):
        q = S5_GROUPS // 4
        cc = cc.reshape(4, q, S5_GROUP_CH, S5_STATE)
        eye = jnp.eye(q, dtype=F32)
        m = jnp.einsum("qgcn,gk->qgnkc", cc, eye)
        return m.reshape(4, q * S5_STATE, q * S5_GROUP_CH)

    wc = jnp.stack([out_proj(c_re), -out_proj(c_im)], axis=1).astype(BF16)
    return lam_re.reshape(1, S5_LANES), lam_im.reshape(1, S5_LANES), wb, wc


def kernel(x_prompt, x_sample, cache_conv, state_ssd, state_s5_re, state_s5_im, cache_mem_k, cache_mem_v,
           mem_prompt, norm_mix, w_in, conv_w, conv_b, dt_bias, a_log, d_skip, ssd_norm, s5_a_re, s5_a_im,
           s5_log_dt, s5_b_re, s5_b_im, s5_c_re, s5_c_im, s5_d, s5_w_glu, w_out, norm_x, norm_mem, w_q,
           w_k, w_v, w_o, norm_ffn, w_router_group, b_router_group, w_router_expert, b_router_expert,
           w_gate, w_up, w_down, norm_final):
    depth = norm_mix.shape[0]
    assert depth == 1
    l = 0
    bp, seq_p, _ = x_prompt.shape
    bs, seq_s, _ = x_sample.shape

    o1 = SSD_W
    o2 = o1 + CONV_CH
    o3 = o2 + SSD_HEADS
    lam_re, lam_im, wb, wc = _s5_params(s5_a_re[l], s5_a_im[l], s5_log_dt[l], s5_b_re[l], s5_b_im[l],
                                        s5_c_re[l], s5_c_im[l])
    wm = {
        "gmix": _row(norm_mix[l]),
        "wz": w_in[l][:, :o1].astype(BF16),
        "wxbc": w_in[l][:, o1:o2].astype(BF16),
        "wdt": jnp.pad(w_in[l][:, o2:o3], ((0, 0), (0, LANE - SSD_HEADS))).astype(BF16),
        "wu": w_in[l][:, o3:].astype(BF16),
        "convw": conv_w[l].astype(F32), "convb": _row(conv_b[l]),
        "dtb": _row(dt_bias[l], LANE), "aneg": _row(-jnp.exp(a_log[l].astype(F32)), LANE),
        "eexp": (jnp.arange(LANE, dtype=jnp.int32)[:, None]
                 == jnp.arange(SSD_W, dtype=jnp.int32)[None, :] // SSD_HEAD_DIM).astype(BF16),
        "dskip": _row(jnp.repeat(d_skip[l].astype(F32), SSD_HEAD_DIM)), "ssdn": _row(ssd_norm[l]),
        "lamre": lam_re, "lamim": lam_im, "wb": wb, "wc": wc,
        "s5d": _row(s5_d[l]), "wglu": s5_w_glu[l].astype(BF16), "wout": w_out[l].astype(BF16),
    }
    wr = jnp.concatenate([w_router_expert[l].reshape(D_MODEL, N_EXPERTS), w_router_group[l]], axis=1)
    br = jnp.concatenate([b_router_expert[l].reshape(N_EXPERTS), b_router_group[l]])
    wa = {
        "gx": _row(norm_x[l]), "wq": w_q[l].astype(BF16), "wo": w_o[l].astype(BF16),
        "gffn": _row(norm_ffn[l]),
        "wr": jnp.pad(wr, ((0, 0), (0, LANE - wr.shape[1]))).astype(BF16), "br": _row(br, LANE),
    }
    we = {"wgate": w_gate, "wup": w_up, "wdown": w_down}
    gfin = _row(norm_final)

    mk_p, mv_p, mk_state, mv_state = _memkv(mem_prompt, _row(norm_mem[l]), w_k[l].astype(BF16), w_v[l].astype(BF16))

    def group(x, conv0, ssd0, s5re0, s5im0, mem_k, mem_v, attn_seqs, attn_tile, expert_rows):
        nb, seq, _ = x.shape
        n_tok = nb * seq
        h1, conv, ssd, s5re, s5im = _mixer(x, conv0, ssd0, s5re0.reshape(nb, S5_LANES),
                                           s5im0.reshape(nb, S5_LANES), wm)
        h2, tn, eid, gates = _attn(h1, mem_k, mem_v, wa, attn_seqs, attn_tile)
        pos, row_token, items = _route_plan(eid.reshape(n_tok, LANE)[:, :2], expert_rows)
        ys = _experts(tn.reshape(n_tok * ROW_TILES, LANE), row_token, items, we)
        y = _combine(ys, pos, gates.reshape(n_tok, LANE), h2.reshape(n_tok, D_MODEL), gfin)
        return (y.reshape(nb, seq, D_MODEL), conv[None], ssd[None],
                s5re.reshape(1, nb, S5_GROUPS, S5_STATE), s5im.reshape(1, nb, S5_GROUPS, S5_STATE))

    zeros = lambda *s: jnp.zeros(s, F32)
    y_p, conv_p, ssd_p, s5re_p, s5im_p = group(
        x_prompt, zeros(bp, CONV_K - 1, CONV_CH), zeros(bp, SSD_HEADS, SSD_HEAD_DIM, SSD_STATE),
        zeros(bp, S5_GROUPS, S5_STATE), zeros(bp, S5_GROUPS, S5_STATE), mk_p, mv_p,
        1, min(512, seq_p), PROMPT_EXPERT_ROWS)
    y_s, conv_s, ssd_s, s5re_s, s5im_s = group(
        x_sample, cache_conv.reshape(cache_conv.shape[1:]), state_ssd.reshape(state_ssd.shape[1:]),
        state_s5_re.reshape(state_s5_re.shape[1:]), state_s5_im.reshape(state_s5_im.shape[1:]),
        cache_mem_k, cache_mem_v, SAMPLE_ATTN_SEQS, seq_s, SAMPLE_EXPERT_ROWS)

    return (y_p, y_s, conv_p, ssd_p, s5re_p, s5im_p, mk_state, mv_state, conv_s, ssd_s, s5re_s, s5im_s)
```

```python
import functools
import math

import jax
import jax.numpy as jnp
from jax import lax
from jax.experimental import pallas as pl
from jax.experimental.pallas import tpu as pltpu

F32 = jnp.float32
BF16 = jnp.bfloat16
EPS = 1e-6

D_MODEL = 1024
CHUNK = 64
SEQ_TILE = 8
SSD_W = 512
SSD_HEAD_DIM = 64
SSD_HEADS = 8
SSD_GROUPS = 2
SSD_STATE = 128
CONV_K = 4
CONV_CH = 1024
S5_W = 512
S5_GROUPS = 32
S5_GROUP_CH = 16
S5_STATE = 64
S5_LANES = S5_GROUPS * S5_STATE
MEM_LEN = 256
X_HEADS = 4
X_HEAD_DIM = 256
N_EXPERT_GROUPS = 4
EXPERTS_PER_GROUP = 8
N_EXPERTS = 32
EXPERT_FF = 256
LANE = 128
PAD_ROWS = 8
S5_PITCH = CHUNK + PAD_ROWS
S5_SCAN_TILES = 4
PROMPT_EXPERT_ROWS = 256
SAMPLE_EXPERT_ROWS = 256
GATHER_SLOTS = 3
COMBINE_TILE = 512
PROMPT_ATTN_ROWS = 1024
SAMPLE_ATTN_SEQS = 4
ROW_TILES = D_MODEL // LANE
KEY_SHIFT = 20
VMEM_LIMIT = 56 * 1024 * 1024


def _rms(x, g):
    return x * lax.rsqrt(jnp.mean(x * x, axis=-1, keepdims=True) + EPS) * g


def _dot(a, b):
    return jnp.dot(a, b, preferred_element_type=F32)


def _dot_nt(a, b):
    return lax.dot_general(a, b, (((1,), (1,)), ((), ())), preferred_element_type=F32)


def _dot_tn(a, b):
    return lax.dot_general(a, b, (((0,), (0,)), ((), ())), preferred_element_type=F32)


def _const_spec(shape):
    nd = len(shape)
    return pl.BlockSpec(shape, lambda *_: (0,) * nd)


def _memkv_kernel(m_ref, g_ref, wk_ref, wv_ref, k_ref, v_ref, k5_ref, v5_ref):
    mn = _rms(m_ref[0], g_ref[...]).astype(BF16)
    k = _dot(mn, wk_ref[...])
    v = _dot(mn, wv_ref[...])
    k_ref[0] = k
    v_ref[0] = v
    for hd in range(X_HEADS):
        k5_ref[0, 0, :, hd, :] = k[:, hd * X_HEAD_DIM:(hd + 1) * X_HEAD_DIM]
        v5_ref[0, 0, :, hd, :] = v[:, hd * X_HEAD_DIM:(hd + 1) * X_HEAD_DIM]


def _memkv(mem, g, wk, wv):
    nb = mem.shape[0]
    flat_spec = pl.BlockSpec((1, MEM_LEN, D_MODEL), lambda i: (i, 0, 0))
    head_spec = pl.BlockSpec((1, 1, MEM_LEN, X_HEADS, X_HEAD_DIM), lambda i: (0, i, 0, 0, 0))
    flat_shape = jax.ShapeDtypeStruct((nb, MEM_LEN, D_MODEL), F32)
    head_shape = jax.ShapeDtypeStruct((1, nb, MEM_LEN, X_HEADS, X_HEAD_DIM), F32)
    return pl.pallas_call(
        _memkv_kernel,
        grid=(nb,),
        in_specs=[flat_spec, _const_spec((1, D_MODEL)), _const_spec((D_MODEL, D_MODEL)),
                  _const_spec((D_MODEL, D_MODEL))],
        out_specs=[flat_spec, flat_spec, head_spec, head_spec],
        out_shape=[flat_shape, flat_shape, head_shape, head_shape],
        compiler_params=pltpu.CompilerParams(dimension_semantics=("arbitrary",), vmem_limit_bytes=VMEM_LIMIT),
        name="memkv",
    )(mem, g, wk, wv)


def _softplus(x):
    return jnp.maximum(x, 0.0) + jnp.log1p(jnp.exp(-jnp.abs(x)))


def _mixer_kernel(n_chunks, x_ref, conv0_ref, ssd0_ref, s5re0_ref, s5im0_ref,
                  gmix_ref, wz_ref, wxbc_ref, wdt_ref, wu_ref, convw_ref, convb_ref,
                  dtb_ref, aneg_ref, eexp_ref, dskip_ref, ssdn_ref,
                  lamre_ref, lamim_ref, wb_ref, wc_ref, s5d_ref, wglu_ref, wout_ref,
                  h_ref, conv_ref, ssd_ref, s5re_ref, s5im_ref,
                  xn_ref, xpad_ref, xc_ref, dte_ref, cse_ref, crow_ref, st_ref, y_ref, z_ref, u_ref,
                  bure_ref, buim_ref, mix_ref):
    c = pl.program_id(1)
    rows = SEQ_TILE * CHUNK

    @pl.when(c == 0)
    def _():
        xpad_ref[:, PAD_ROWS - (CONV_K - 1):PAD_ROWS, :] = conv0_ref[...]
        s5re_ref[...] = s5re0_ref[...]
        s5im_ref[...] = s5im0_ref[...]

    x = x_ref[...].reshape(rows, D_MODEL)
    xn_ref[...] = _rms(x, gmix_ref[...]).astype(BF16)

    half_ch = S5_W // 2
    half_st = S5_LANES // 2
    half_tiles = half_st // LANE

    cw = CONV_CH // 4
    grp = S5_SCAN_TILES * LANE

    def m_xbc(cb):
        ls = pl.ds(cb * cw, cw)
        xpad_ref[:, PAD_ROWS:, ls] = _dot(xn_ref[...], wxbc_ref[:, ls]).reshape(SEQ_TILE, CHUNK, cw)

    def m_proj(dst_ref, w_ref, hf):
        ls = pl.ds(hf * half_ch, half_ch)
        dst_ref[:, ls] = _dot(xn_ref[...], w_ref[:, ls])

    def m_s5_input(g, im):
        hf, q = divmod(g, 2)
        dst_ref = buim_ref if im else bure_ref
        bu = _dot(u_ref[:, hf * half_ch:(hf + 1) * half_ch].astype(BF16),
                  wb_ref[hf, :, im * half_st + q * grp:im * half_st + (q + 1) * grp])
        for k in range(S5_SCAN_TILES):
            for b in range(SEQ_TILE):
                dst_ref[g * S5_SCAN_TILES + k, pl.ds(b * S5_PITCH, CHUNK), :] = bu[b * CHUNK:(b + 1) * CHUNK,
                                                                                  k * LANE:(k + 1) * LANE]

    def v_conv(cb):
        ls = pl.ds(cb * cw, cw)
        acc = convb_ref[:, ls].reshape(1, 1, cw)
        for k in range(CONV_K):
            lo = PAD_ROWS - (CONV_K - 1) + k
            acc = acc + convw_ref[k:k + 1, ls].reshape(1, 1, cw) * xpad_ref[:, lo:lo + CHUNK, ls]
        xc_ref[:, ls] = (acc * jax.nn.sigmoid(acc)).reshape(rows, cw)

    def v_scan(g):
        tiles = [g * S5_SCAN_TILES + k for k in range(S5_SCAN_TILES)]
        lr = [jnp.broadcast_to(lamre_ref[:, pl.ds(k * LANE, LANE)], (SEQ_TILE, LANE)) for k in tiles]
        li = [jnp.broadcast_to(lamim_ref[:, pl.ds(k * LANE, LANE)], (SEQ_TILE, LANE)) for k in tiles]
        sr = [s5re_ref[:, pl.ds(k * LANE, LANE)] for k in tiles]
        si = [s5im_ref[:, pl.ds(k * LANE, LANE)] for k in tiles]
        for t in range(CHUNK):
            ts = pl.ds(t, SEQ_TILE, stride=S5_PITCH)
            for q, k in enumerate(tiles):
                nr = lr[q] * sr[q] - li[q] * si[q] + bure_ref[k, ts, :]
                ni = lr[q] * si[q] + li[q] * sr[q] + buim_ref[k, ts, :]
                sr[q], si[q] = nr, ni
                bure_ref[k, ts, :] = nr
                buim_ref[k, ts, :] = ni
        for q, k in enumerate(tiles):
            s5re_ref[:, pl.ds(k * LANE, LANE)] = sr[q]
            s5im_ref[:, pl.ds(k * LANE, LANE)] = si[q]

    m_xbc(0); m_proj(u_ref, wu_ref, 0); m_proj(u_ref, wu_ref, 1); v_conv(0)
    m_xbc(1); m_s5_input(0, 0); m_s5_input(0, 1); m_s5_input(1, 0); m_s5_input(1, 1); v_conv(1); v_scan(0)
    m_xbc(2); m_s5_input(2, 0); m_s5_input(2, 1); m_s5_input(3, 0); m_s5_input(3, 1); v_conv(2); v_scan(1)
    m_xbc(3); m_proj(z_ref, wz_ref, 0); m_proj(z_ref, wz_ref, 1); v_conv(3); v_scan(2)
    v_scan(3)
    hist = xpad_ref[:, PAD_ROWS + CHUNK - (CONV_K - 1):, :]
    conv_ref[...] = hist
    xpad_ref[:, PAD_ROWS - (CONV_K - 1):PAD_ROWS, :] = hist

    dt = _softplus(_dot(xn_ref[...], wdt_ref[...]) + dtb_ref[...])
    a = dt * aneg_ref[...]
    tpos = lax.broadcasted_iota(jnp.int32, (rows, LANE), 0) % CHUNK
    sh = 1
    while sh < CHUNK:
        a = a + jnp.where(tpos >= sh, pltpu.roll(a, sh, axis=0), 0.0)
        sh *= 2

    def expand_heads(v):
        hi = v.astype(BF16)
        r1 = v - hi.astype(F32)
        mid = r1.astype(BF16)
        lo = (r1 - mid.astype(F32)).astype(BF16)
        e = eexp_ref[...]
        return _dot(hi, e) + _dot(mid, e) + _dot(lo, e)

    dte_ref[...] = expand_heads(dt)
    cse_ref[...] = expand_heads(a)
    for b in range(SEQ_TILE):
        at = a[b * CHUNK:(b + 1) * CHUNK, :].T
        crow_ref[pl.ds(b, 1), :] = jnp.concatenate([at[h:h + 1, :] for h in range(SSD_HEADS)], axis=1)

    @pl.when(c == 0)
    def _():
        for b in range(SEQ_TILE):
            st_ref[b] = ssd0_ref[b].reshape(SSD_W, SSD_STATE).T

    gw = SSD_W // SSD_GROUPS
    heads_per_group = SSD_HEADS // SSD_GROUPS
    tri = (lax.broadcasted_iota(jnp.int32, (CHUNK, gw), 0)
           >= lax.broadcasted_iota(jnp.int32, (CHUNK, gw), 1) % CHUNK)
    same_head = ((lax.broadcasted_iota(jnp.int32, (gw, gw), 0) // SSD_HEAD_DIM)
                 == (lax.broadcasted_iota(jnp.int32, (gw, gw), 1) // SSD_HEAD_DIM))

    def seq_body(b, carry):
        r0 = pl.multiple_of(b * CHUNK, CHUNK)
        rs = pl.ds(r0, CHUNK)
        for g in range(SSD_GROUPS):
            ls = pl.ds(g * gw, gw)
            cse = cse_ref[rs, ls]
            cs_last = cse_ref[pl.ds(r0 + CHUNK - 1, 1), ls]
            decay = jnp.exp(jnp.where(tri, cse - crow_ref[pl.ds(b, 1), ls], -jnp.inf))
            xs = xc_ref[rs, ls]
            xdt = xs * dte_ref[rs, ls]
            b_bf = xc_ref[rs, pl.ds(SSD_W + g * SSD_STATE, SSD_STATE)].astype(BF16)
            c_bf = xc_ref[rs, pl.ds(SSD_W + SSD_GROUPS * SSD_STATE + g * SSD_STATE, SSD_STATE)].astype(BF16)
            gram = _dot_nt(c_bf, jnp.concatenate([b_bf] * heads_per_group, axis=0))
            xbd = jnp.where(same_head, jnp.concatenate([xdt] * heads_per_group, axis=0), 0.0).astype(BF16)
            y = _dot((gram * decay).astype(BF16), xbd)
            st = st_ref[b, :, ls]
            y = y + _dot(c_bf, st.astype(BF16)) * jnp.exp(cse)
            y = y + dskip_ref[:, ls] * xs
            y_ref[rs, ls] = y
            upd = _dot_tn(b_bf, (xdt * jnp.exp(cs_last - cse)).astype(BF16))
            st_ref[b, :, ls] = st * jnp.exp(cs_last) + upd
        return carry

    lax.fori_loop(0, SEQ_TILE, seq_body, 0)

    @pl.when(c == n_chunks - 1)
    def _():
        for b in range(SEQ_TILE):
            ssd_ref[b] = st_ref[b].T.reshape(SSD_HEADS, SSD_HEAD_DIM, SSD_STATE)

    def seq_rows(ref, k):
        return jnp.concatenate([ref[k, pl.ds(b * S5_PITCH, CHUNK), :] for b in range(SEQ_TILE)], axis=0)

    n_blk = S5_W // LANE
    blk_rows = rows // n_blk
    for j in range(n_blk):
        s_re = jnp.concatenate([seq_rows(bure_ref, 4 * j + k) for k in range(4)], axis=1)
        s_im = jnp.concatenate([seq_rows(buim_ref, 4 * j + k) for k in range(4)], axis=1)
        yj = _dot(s_re.astype(BF16), wc_ref[j, 0]) + _dot(s_im.astype(BF16), wc_ref[j, 1])
        rs = pl.ds(j * blk_rows, blk_rows)
        z = z_ref[rs, :]
        y = y_ref[rs, :] * (z * jax.nn.sigmoid(z))
        mix_ref[rs, :] = _rms(y, ssdn_ref[...]).astype(BF16)
        ls = pl.ds(j * LANE, LANE)
        u_ref[:, ls] = jax.nn.gelu(yj + s5d_ref[:, ls] * u_ref[:, ls])
    h_ssd = _dot(mix_ref[...], wout_ref[0:SSD_W, :])
    y5 = u_ref[...]
    y5 = y5 * jax.nn.sigmoid(_dot(y5.astype(BF16), wglu_ref[...]))
    h = x_ref[...].reshape(rows, D_MODEL) + (h_ssd + _dot(y5.astype(BF16), wout_ref[SSD_W:, :]))
    h_ref[...] = h.reshape(SEQ_TILE, CHUNK, D_MODEL)


def _mixer(x, conv0, ssd0, s5re0, s5im0, w):
    nb, seq, _ = x.shape
    grid = (nb // SEQ_TILE, seq // CHUNK)
    rows = SEQ_TILE * CHUNK
    weights = [w["gmix"], w["wz"], w["wxbc"], w["wdt"], w["wu"], w["convw"], w["convb"],
               w["dtb"], w["aneg"], w["eexp"], w["dskip"], w["ssdn"],
               w["lamre"], w["lamim"], w["wb"], w["wc"], w["s5d"], w["wglu"], w["wout"]]
    state_specs = [pl.BlockSpec((SEQ_TILE, CONV_K - 1, CONV_CH), lambda i, c: (i, 0, 0)),
                   pl.BlockSpec((SEQ_TILE, SSD_HEADS, SSD_HEAD_DIM, SSD_STATE), lambda i, c: (i, 0, 0, 0)),
                   pl.BlockSpec((SEQ_TILE, S5_LANES), lambda i, c: (i, 0)),
                   pl.BlockSpec((SEQ_TILE, S5_LANES), lambda i, c: (i, 0))]
    x_spec = pl.BlockSpec((SEQ_TILE, CHUNK, D_MODEL), lambda i, c: (i, c, 0))
    return pl.pallas_call(
        functools.partial(_mixer_kernel, grid[1]),
        grid=grid,
        in_specs=[x_spec] + state_specs + [_const_spec(a.shape) for a in weights],
        out_specs=[x_spec] + state_specs,
        out_shape=[jax.ShapeDtypeStruct(x.shape, F32),
                   jax.ShapeDtypeStruct(conv0.shape, F32), jax.ShapeDtypeStruct(ssd0.shape, F32),
                   jax.ShapeDtypeStruct(s5re0.shape, F32), jax.ShapeDtypeStruct(s5im0.shape, F32)],
        scratch_shapes=[
            pltpu.VMEM((rows, D_MODEL), BF16),
            pltpu.VMEM((SEQ_TILE, PAD_ROWS + CHUNK, CONV_CH), F32),
            pltpu.VMEM((rows, CONV_CH), F32),
            pltpu.VMEM((rows, SSD_W), F32),
            pltpu.VMEM((rows, SSD_W), F32),
            pltpu.VMEM((SEQ_TILE, SSD_W), F32),
            pltpu.VMEM((SEQ_TILE, SSD_STATE, SSD_W), F32),
            pltpu.VMEM((rows, SSD_W), F32),
            pltpu.VMEM((rows, SSD_W), F32),
            pltpu.VMEM((rows, S5_W), F32),
            pltpu.VMEM((S5_LANES // LANE, SEQ_TILE * S5_PITCH, LANE), F32),
            pltpu.VMEM((S5_LANES // LANE, SEQ_TILE * S5_PITCH, LANE), F32),
            pltpu.VMEM((rows, SSD_W), BF16),
        ],
        compiler_params=pltpu.CompilerParams(dimension_semantics=("arbitrary", "arbitrary"),
                                             vmem_limit_bytes=VMEM_LIMIT),
        name="mixer",
    )(x, conv0, ssd0, s5re0, s5im0, *weights)


def _attn_kernel(n_seq, tile, heads_axis, h_ref, k_ref, v_ref, gx_ref, wq_ref, wo_ref, gffn_ref, wr_ref, br_ref,
                 h2_ref, tn_ref, eid_ref, gate_ref, o_ref, kv_ref):
    rows = n_seq * tile
    h1 = h_ref[...].reshape(rows, D_MODEL)
    xn = _rms(h1, gx_ref[...]).astype(BF16)
    q = _dot(xn, wq_ref[...])
    scale = X_HEAD_DIM ** -0.5
    for sq in range(n_seq):
        rs = slice(sq * tile, (sq + 1) * tile)
        for hd in range(X_HEADS):
            ls = slice(hd * X_HEAD_DIM, (hd + 1) * X_HEAD_DIM)
            if heads_axis:
                kv_ref[0] = k_ref[0, sq, :, hd, :]
                kv_ref[1] = v_ref[0, sq, :, hd, :]
                kh = kv_ref[0].astype(BF16)
                vh = kv_ref[1].astype(BF16)
            else:
                kh = k_ref[sq, :, ls].astype(BF16)
                vh = v_ref[sq, :, ls].astype(BF16)
            s = _dot_nt(q[rs, ls].astype(BF16), kh) * scale
            s = s - jnp.max(s, axis=-1, keepdims=True)
            p = jnp.exp(s)
            p = p / jnp.sum(p, axis=-1, keepdims=True)
            o_ref[rs, ls] = _dot(p.astype(BF16), vh).astype(BF16)
    h2 = h1 + _dot(o_ref[...], wo_ref[...])
    h2_ref[...] = h2.reshape(n_seq, tile, D_MODEL)

    tn = _rms(h2, gffn_ref[...]).astype(BF16)
    tn32 = tn.astype(F32)
    for sq in range(n_seq):
        for sb in range(ROW_TILES):
            tn_ref[sq, pl.ds(sb, tile, stride=ROW_TILES), :] = tn32[sq * tile:(sq + 1) * tile,
                                                                  sb * LANE:(sb + 1) * LANE]
    lt = (_dot(tn, wr_ref[...]) + br_ref[...]).T
    sub = lax.broadcasted_iota(jnp.int32, (EXPERTS_PER_GROUP, rows), 0)
    big = jnp.int32(2 ** 30)
    neg = -jnp.inf

    def first_max(x):
        m = jnp.max(x, axis=0, keepdims=True)
        return m, jnp.min(jnp.where(x == m, sub, big), axis=0, keepdims=True)

    gl = jnp.where(sub < N_EXPERT_GROUPS, lt[N_EXPERTS:N_EXPERTS + EXPERTS_PER_GROUP, :], neg)
    gmax, g_idx = first_max(gl)
    g_prob = 1.0 / jnp.sum(jnp.exp(gl - gmax), axis=0, keepdims=True)
    el = lt[0:EXPERTS_PER_GROUP, :]
    for g in range(1, N_EXPERT_GROUPS):
        el = jnp.where(g_idx == g, lt[g * EXPERTS_PER_GROUP:(g + 1) * EXPERTS_PER_GROUP, :], el)
    m1, i1 = first_max(el)
    m2, i2 = first_max(jnp.where(sub == i1, neg, el))
    e2 = jnp.exp(m2 - m1)
    den = 1.0 + e2
    base = g_idx * EXPERTS_PER_GROUP
    zeros = jnp.zeros((LANE - EXPERTS_PER_GROUP, rows), F32)
    eid_t = jnp.where(sub == 0, base + i1, jnp.where(sub == 1, base + i2, 0)).astype(F32)
    gate_t = jnp.where(sub == 0, (1.0 / den) * g_prob, jnp.where(sub == 1, (e2 / den) * g_prob, 0.0))
    eid = jnp.concatenate([eid_t, zeros], axis=0).T.astype(jnp.int32)
    gate = jnp.concatenate([gate_t, zeros], axis=0).T
    eid_ref[...] = eid.reshape(n_seq, tile, LANE)
    gate_ref[...] = gate.reshape(n_seq, tile, LANE)


def _attn(h1, mem_k, mem_v, w, n_seq, tile):
    nb, seq, _ = h1.shape
    heads_axis = mem_k.ndim == 5
    row_spec = pl.BlockSpec((n_seq, tile, D_MODEL), lambda b, i: (b, i, 0))
    if heads_axis:
        kv_spec = pl.BlockSpec((1, n_seq, MEM_LEN, X_HEADS, X_HEAD_DIM), lambda b, i: (0, b, 0, 0, 0))
    else:
        kv_spec = pl.BlockSpec((n_seq, MEM_LEN, D_MODEL), lambda b, i: (b, 0, 0))
    lane_spec = pl.BlockSpec((n_seq, tile, LANE), lambda b, i: (b, i, 0))
    weights = [w["gx"], w["wq"], w["wo"], w["gffn"], w["wr"], w["br"]]
    return pl.pallas_call(
        functools.partial(_attn_kernel, n_seq, tile, heads_axis),
        grid=(nb // n_seq, seq // tile),
        in_specs=[row_spec, kv_spec, kv_spec] + [_const_spec(a.shape) for a in weights],
        out_specs=[row_spec, pl.BlockSpec((n_seq, tile * ROW_TILES, LANE), lambda b, i: (b, i, 0)),
                   lane_spec, lane_spec],
        out_shape=[jax.ShapeDtypeStruct(h1.shape, F32),
                   jax.ShapeDtypeStruct((nb, seq * ROW_TILES, LANE), F32),
                   jax.ShapeDtypeStruct((nb, seq, LANE), jnp.int32),
                   jax.ShapeDtypeStruct((nb, seq, LANE), F32)],
        scratch_shapes=[pltpu.VMEM((n_seq * tile, D_MODEL), BF16),
                        pltpu.VMEM((2, MEM_LEN, X_HEAD_DIM), F32)],
        compiler_params=pltpu.CompilerParams(dimension_semantics=("arbitrary", "arbitrary"),
                                             vmem_limit_bytes=VMEM_LIMIT),
        name="attn_router",
    )(h1, mem_k, mem_v, *weights)


def _route_plan(eid, tile):
    n_tok = eid.shape[0]
    n_asg = 2 * n_tok
    assert n_asg % tile == 0 and n_asg < (1 << KEY_SHIFT)
    n_tiles = n_asg // tile
    n_items = n_tiles + N_EXPERTS
    i32 = jnp.int32
    e_flat = eid.reshape(n_asg)
    a_idx = jnp.arange(n_asg, dtype=i32)
    keys = lax.sort(e_flat * (1 << KEY_SHIFT) + a_idx)
    order = keys & ((1 << KEY_SHIFT) - 1)
    row_token = (order // 2).reshape(n_tiles, 1, tile)
    pos = lax.sort((order, a_idx), num_keys=1)[1]
    sorted_e = keys >> KEY_SHIFT
    seg_end = jnp.sum((sorted_e[None, :] <= jnp.arange(N_EXPERTS, dtype=i32)[:, None]).astype(i32), axis=1)
    counts = seg_end - jnp.concatenate([jnp.zeros((1,), i32), seg_end[:-1]])
    seg_start = seg_end - counts
    first_tile = seg_start // tile
    last_tile = (seg_end - 1) // tile
    items_e = jnp.where(counts > 0, last_tile - first_tile + 1, 0)
    it_end = jnp.cumsum(items_e)
    it_start = it_end - items_e
    w = jnp.arange(n_items, dtype=i32)
    wc = jnp.minimum(w, it_end[-1] - 1)
    it_expert = jnp.sum((wc[:, None] >= it_end[None, :]).astype(i32), axis=1)
    it_onehot = (it_expert[:, None] == jnp.arange(N_EXPERTS, dtype=i32)[None, :]).astype(i32)

    def of_item(table):
        return jnp.sum(it_onehot * table[None, :], axis=1)

    it_tile = of_item(first_tile) + (wc - of_item(it_start))
    it_lo = jnp.clip(of_item(seg_start) - it_tile * tile, 0, tile)
    it_hi = jnp.clip(of_item(seg_end) - it_tile * tile, 0, tile)
    it_valid = (w < it_end[-1]).astype(i32)
    items = tuple(a.astype(i32) for a in (it_tile, it_expert, it_lo, it_hi, it_valid))
    return pos.reshape(n_tok, 2), row_token, items


def _token_copy(src_hbm, src_row, dst, dst_row, sem):
    return pltpu.make_async_copy(src_hbm.at[pl.ds(pl.multiple_of(src_row * ROW_TILES, ROW_TILES), ROW_TILES), :],
                                 dst.at[pl.ds(pl.multiple_of(dst_row * ROW_TILES, ROW_TILES), ROW_TILES), :], sem)


def _experts_kernel(n_tiles, n_items, tile, tile_ref, exp_ref, lo_ref, hi_ref, valid_ref, rt_cur_ref, rt_nxt_ref,
                    rt_nxt2_ref, tn_hbm, wg_ref, wu_ref, wd_ref, y_ref, xbuf, xb_ref, wgu_bf, wd_bf, sem):
    w = pl.program_id(0)
    j = tile_ref[w]
    lo = lo_ref[w]
    hi = hi_ref[w]
    valid = valid_ref[w] == 1
    first = jnp.logical_and(valid, lo == 0)
    slot = j % GATHER_SLOTS
    slot_ahead = (j + 2) % GATHER_SLOTS

    @pl.when(jnp.logical_or(w == 0, exp_ref[w] != exp_ref[jnp.maximum(w - 1, 0)]))
    def _():
        wgu_bf[:, :EXPERT_FF] = wg_ref[0, 0].astype(BF16)
        wgu_bf[:, EXPERT_FF:] = wu_ref[0, 0].astype(BF16)
        wd_bf[...] = wd_ref[0, 0].astype(BF16)

    def wait_gather(s):
        def body(r, carry):
            _token_copy(tn_hbm, 0, xbuf.at[s], r, sem.at[s]).wait()
            return carry
        lax.fori_loop(0, tile, body, 0, unroll=8)

    def expert_rows():
        gu = _dot(xb_ref[...], wgu_bf[...])
        gate = gu[:, :EXPERT_FF]
        hid = (gate * jax.nn.sigmoid(gate)) * gu[:, EXPERT_FF:]
        return _dot(hid.astype(BF16), wd_bf[...])

    def start_gather(rt_ref, s):
        def body(r2, carry):
            for p in range(2):
                r = 2 * r2 + p
                _token_copy(tn_hbm, rt_ref[0, 0, r], xbuf.at[s], r, sem.at[s]).start(priority=p)
            return carry
        lax.fori_loop(0, tile // 2, body, 0, unroll=4)

    @pl.when(w == 0)
    def _():
        start_gather(rt_cur_ref, 0)
        start_gather(rt_nxt_ref, 1)

    @pl.when(first)
    def _():
        wait_gather(slot)
        for sb in range(ROW_TILES):
            xb_ref[:, sb * LANE:(sb + 1) * LANE] = xbuf[slot, pl.ds(sb, tile, stride=ROW_TILES), :].astype(BF16)
        for r in range(tile):
            _token_copy(tn_hbm, rt_nxt2_ref[0, 0, r], xbuf.at[slot_ahead], r,
                        sem.at[slot_ahead]).start(priority=r % 2)
        y = expert_rows()
        for sb in range(ROW_TILES):
            y_ref[pl.ds(sb, tile, stride=ROW_TILES), :] = y[:, sb * LANE:(sb + 1) * LANE]

    @pl.when(jnp.logical_and(valid, lo != 0))
    def _():
        y = expert_rows()
        row = lax.broadcasted_iota(jnp.int32, (tile, LANE), 0)
        keep = jnp.logical_and(row >= lo, row < hi)
        for sb in range(ROW_TILES):
            rows = pl.ds(sb, tile, stride=ROW_TILES)
            y_ref[rows, :] = jnp.where(keep, y[:, sb * LANE:(sb + 1) * LANE], y_ref[rows, :])

    @pl.when(w == n_items - 1)
    def _():
        wait_gather(n_tiles % GATHER_SLOTS)
        wait_gather((n_tiles + 1) % GATHER_SLOTS)


def _experts(tn, row_token, items, w):
    n_tiles, _, tile = row_token.shape
    n_items = items[0].shape[0]

    def smem_tile(imap):
        return pl.BlockSpec((1, 1, tile), imap, memory_space=pltpu.SMEM)

    grid_spec = pltpu.PrefetchScalarGridSpec(
        num_scalar_prefetch=5,
        grid=(n_items,),
        in_specs=[smem_tile(lambda i, tl, ex, lo, hi, va: (tl[i], 0, 0)),
                  smem_tile(lambda i, tl, ex, lo, hi, va: (jnp.minimum(tl[i] + 1, n_tiles - 1), 0, 0)),
                  smem_tile(lambda i, tl, ex, lo, hi, va: (jnp.minimum(tl[i] + 2, n_tiles - 1), 0, 0)),
                  pl.BlockSpec(memory_space=pl.ANY),
                  pl.BlockSpec((1, 1, D_MODEL, EXPERT_FF), lambda i, tl, ex, lo, hi, va: (0, ex[i], 0, 0)),
                  pl.BlockSpec((1, 1, D_MODEL, EXPERT_FF), lambda i, tl, ex, lo, hi, va: (0, ex[i], 0, 0)),
                  pl.BlockSpec((1, 1, EXPERT_FF, D_MODEL), lambda i, tl, ex, lo, hi, va: (0, ex[i], 0, 0))],
        out_specs=pl.BlockSpec((tile * ROW_TILES, LANE), lambda i, tl, ex, lo, hi, va: (tl[i], 0)),
        scratch_shapes=[pltpu.VMEM((GATHER_SLOTS, tile * ROW_TILES, LANE), F32),
                        pltpu.VMEM((tile, D_MODEL), BF16),
                        pltpu.VMEM((D_MODEL, 2 * EXPERT_FF), BF16),
                        pltpu.VMEM((EXPERT_FF, D_MODEL), BF16),
                        pltpu.SemaphoreType.DMA((GATHER_SLOTS,))],
    )
    return pl.pallas_call(
        functools.partial(_experts_kernel, n_tiles, n_items, tile),
        grid_spec=grid_spec,
        out_shape=jax.ShapeDtypeStruct((n_tiles * tile * ROW_TILES, LANE), F32),
        compiler_params=pltpu.CompilerParams(dimension_semantics=("arbitrary",), vmem_limit_bytes=VMEM_LIMIT),
        name="experts",
    )(*items, row_token, row_token, row_token, tn, w["wgate"], w["wup"], w["wdown"])


def _combine_kernel(n_steps, pos_cur_ref, pos_nxt_ref, ys_hbm, gate_ref, h2_ref, gfin_ref, out_ref, ybuf, sem):
    i = pl.program_id(0)
    slot = i % 2

    def wait_gather(s):
        def body(r, carry):
            for k in range(2):
                _token_copy(ys_hbm, 0, ybuf.at[s, k], r, sem.at[s]).wait()
            return carry
        lax.fori_loop(0, COMBINE_TILE, body, 0, unroll=8)

    def start_gather(pos_ref, s):
        def body(r, carry):
            for k in range(2):
                _token_copy(ys_hbm, pos_ref[0, 0, 2 * r + k], ybuf.at[s, k], r,
                            sem.at[s]).start(priority=k)
            return carry
        lax.fori_loop(0, COMBINE_TILE, body, 0, unroll=8)

    @pl.when(i == 0)
    def _():
        start_gather(pos_cur_ref, 0)

    @pl.when(i + 1 < n_steps)
    def _():
        start_gather(pos_nxt_ref, 1 - slot)

    wait_gather(slot)

    gates = gate_ref[...]
    g0 = gates[:, 0:1]
    g1 = gates[:, 1:2]
    ssq = jnp.zeros((COMBINE_TILE, 1), F32)
    for sb in range(ROW_TILES):
        rows = pl.ds(sb, COMBINE_TILE, stride=ROW_TILES)
        cols = pl.ds(sb * LANE, LANE)
        v = h2_ref[:, cols] + (ybuf[slot, 0, rows, :] * g0 + ybuf[slot, 1, rows, :] * g1)
        ssq = ssq + jnp.sum(v * v, axis=-1, keepdims=True)
        out_ref[:, cols] = v
    out_ref[...] = out_ref[...] * lax.rsqrt(ssq * (1.0 / D_MODEL) + EPS) * gfin_ref[...]


def _combine(ys, pos, gates, h2, gfin):
    n_tok = h2.shape[0]
    n_tiles = n_tok // COMBINE_TILE
    pos_tiles = pos.reshape(n_tiles, 1, 2 * COMBINE_TILE)

    def smem_tile(imap):
        return pl.BlockSpec((1, 1, 2 * COMBINE_TILE), imap, memory_space=pltpu.SMEM)

    row_spec = pl.BlockSpec((COMBINE_TILE, D_MODEL), lambda i: (i, 0))
    return pl.pallas_call(
        functools.partial(_combine_kernel, n_tiles),
        grid=(n_tiles,),
        in_specs=[smem_tile(lambda i: (i, 0, 0)),
                  smem_tile(lambda i: (jnp.minimum(i + 1, n_tiles - 1), 0, 0)),
                  pl.BlockSpec(memory_space=pl.ANY),
                  pl.BlockSpec((COMBINE_TILE, LANE), lambda i: (i, 0)),
                  row_spec, _const_spec((1, D_MODEL))],
        out_specs=row_spec,
        out_shape=jax.ShapeDtypeStruct((n_tok, D_MODEL), F32),
        scratch_shapes=[pltpu.VMEM((2, 2, COMBINE_TILE * ROW_TILES, LANE), F32), pltpu.SemaphoreType.DMA((2,))],
        compiler_params=pltpu.CompilerParams(dimension_semantics=("arbitrary",), vmem_limit_bytes=VMEM_LIMIT),
        name="combine",
    )(pos_tiles, pos_tiles, ys, gates, h2, gfin)


def _row(v, width=None):
    v = v.astype(F32).reshape(1, -1)
    if width is not None and v.shape[1] < width:
        v = jnp.pad(v, ((0, 0), (0, width - v.shape[1])))
    return v


def _s5_params(a_re, a_im, log_dt, b_re, b_im, c_re, c_im):
    dt = jnp.exp(log_dt)[:, None]
    mag = jnp.exp(a_re * dt)
    lam_re, lam_im = mag * jnp.cos(a_im * dt), mag * jnp.sin(a_im * dt)
    den = a_re * a_re + a_im * a_im
    f_re = ((lam_re - 1.0) * a_re + lam_im * a_im) / den
    f_im = (lam_im * a_re - (lam_re - 1.0) * a_im) / den
    bb_re = f_re[..., None] * b_re - f_im[..., None] * b_im
    bb_im = f_re[..., None] * b_im + f_im[..., None] * b_re
    half = S5_GROUPS // 2

    def in_proj(bb):
        bb = bb.reshape(2, half, S5_STATE, S5_GROUP_CH)
        eye = jnp.eye(half, dtype=F32)
        m = jnp.einsum("hgnc,gk->hgckn", bb, eye)
        return m.reshape(2, half * S5_GROUP_CH, half * S5_STATE)

    wb = jnp.concatenate([in_proj(bb_re), in_proj(bb_im)], axis=2).astype(BF16)

    def out_proj(cc):
        q = S5_GROUPS // 4
        cc = cc.reshape(4, q, S5_GROUP_CH, S5_STATE)
        eye = jnp.eye(q, dtype=F32)
        m = jnp.einsum("qgcn,gk->qgnkc", cc, eye)
        return m.reshape(4, q * S5_STATE, q * S5_GROUP_CH)

    wc = jnp.stack([out_proj(c_re), -out_proj(c_im)], axis=1).astype(BF16)
    return lam_re.reshape(1, S5_LANES), lam_im.reshape(1, S5_LANES), wb, wc


def kernel(x_prompt, x_sample, cache_conv, state_ssd, state_s5_re, state_s5_im, cache_mem_k, cache_mem_v,
           mem_prompt, norm_mix, w_in, conv_w, conv_b, dt_bias, a_log, d_skip, ssd_norm, s5_a_re, s5_a_im,
           s5_log_dt, s5_b_re, s5_b_im, s5_c_re, s5_c_im, s5_d, s5_w_glu, w_out, norm_x, norm_mem, w_q,
           w_k, w_v, w_o, norm_ffn, w_router_group, b_router_group, w_router_expert, b_router_expert,
           w_gate, w_up, w_down, norm_final):
    depth = norm_mix.shape[0]
    assert depth == 1
    l = 0
    bp, seq_p, _ = x_prompt.shape
    bs, seq_s, _ = x_sample.shape

    o1 = SSD_W
    o2 = o1 + CONV_CH
    o3 = o2 + SSD_HEADS
    lam_re, lam_im, wb, wc = _s5_params(s5_a_re[l], s5_a_im[l], s5_log_dt[l], s5_b_re[l], s5_b_im[l],
                                        s5_c_re[l], s5_c_im[l])
    wm = {
        "gmix": _row(norm_mix[l]),
        "wz": w_in[l][:, :o1].astype(BF16),
        "wxbc": w_in[l][:, o1:o2].astype(BF16),
        "wdt": jnp.pad(w_in[l][:, o2:o3], ((0, 0), (0, LANE - SSD_HEADS))).astype(BF16),
        "wu": w_in[l][:, o3:].astype(BF16),
        "convw": conv_w[l].astype(F32), "convb": _row(conv_b[l]),
        "dtb": _row(dt_bias[l], LANE), "aneg": _row(-jnp.exp(a_log[l].astype(F32)), LANE),
        "eexp": (jnp.arange(LANE, dtype=jnp.int32)[:, None]
                 == jnp.arange(SSD_W, dtype=jnp.int32)[None, :] // SSD_HEAD_DIM).astype(BF16),
        "dskip": _row(jnp.repeat(d_skip[l].astype(F32), SSD_HEAD_DIM)), "ssdn": _row(ssd_norm[l]),
        "lamre": lam_re, "lamim": lam_im, "wb": wb, "wc": wc,
        "s5d": _row(s5_d[l]), "wglu": s5_w_glu[l].astype(BF16), "wout": w_out[l].astype(BF16),
    }
    wr = jnp.concatenate([w_router_expert[l].reshape(D_MODEL, N_EXPERTS), w_router_group[l]], axis=1)
    br = jnp.concatenate([b_router_expert[l].reshape(N_EXPERTS), b_router_group[l]])
    wa = {
        "gx": _row(norm_x[l]), "wq": w_q[l].astype(BF16), "wo": w_o[l].astype(BF16),
        "gffn": _row(norm_ffn[l]),
        "wr": jnp.pad(wr, ((0, 0), (0, LANE - wr.shape[1]))).astype(BF16), "br": _row(br, LANE),
    }
    we = {"wgate": w_gate, "wup": w_up, "wdown": w_down}
    gfin = _row(norm_final)

    mk_p, mv_p, mk_state, mv_state = _memkv(mem_prompt, _row(norm_mem[l]), w_k[l].astype(BF16), w_v[l].astype(BF16))

    def group(x, conv0, ssd0, s5re0, s5im0, mem_k, mem_v, attn_seqs, attn_tile, expert_rows):
        nb, seq, _ = x.shape
        n_tok = nb * seq
        h1, conv, ssd, s5re, s5im = _mixer(x, conv0, ssd0, s5re0.reshape(nb, S5_LANES),
                                           s5im0.reshape(nb, S5_LANES), wm)
        h2, tn, eid, gates = _attn(h1, mem_k, mem_v, wa, attn_seqs, attn_tile)
        pos, row_token, items = _route_plan(eid.reshape(n_tok, LANE)[:, :2], expert_rows)
        ys = _experts(tn.reshape(n_tok * ROW_TILES, LANE), row_token, items, we)
        y = _combine(ys, pos, gates.reshape(n_tok, LANE), h2.reshape(n_tok, D_MODEL), gfin)
        return (y.reshape(nb, seq, D_MODEL), conv[None], ssd[None],
                s5re.reshape(1, nb, S5_GROUPS, S5_STATE), s5im.reshape(1, nb, S5_GROUPS, S5_STATE))

    zeros = lambda *s: jnp.zeros(s, F32)
    y_p, conv_p, ssd_p, s5re_p, s5im_p = group(
        x_prompt, zeros(bp, CONV_K - 1, CONV_CH), zeros(bp, SSD_HEADS, SSD_HEAD_DIM, SSD_STATE),
        zeros(bp, S5_GROUPS, S5_STATE), zeros(bp, S5_GROUPS, S5_STATE), mk_p, mv_p,
        1, min(PROMPT_ATTN_ROWS, seq_p), PROMPT_EXPERT_ROWS)
    y_s, conv_s, ssd_s, s5re_s, s5im_s = group(
        x_sample, cache_conv.reshape(cache_conv.shape[1:]), state_ssd.reshape(state_ssd.shape[1:]),
        state_s5_re.reshape(state_s5_re.shape[1:]), state_s5_im.reshape(state_s5_im.shape[1:]),
        cache_mem_k, cache_mem_v, SAMPLE_ATTN_SEQS, seq_s, SAMPLE_EXPERT_ROWS)

    return (y_p, y_s, conv_p, ssd_p, s5re_p, s5im_p, mk_state, mv_state, conv_s, ssd_s, s5re_s, s5im_s)
```

```python
import functools

import jax
import jax.numpy as jnp
from jax import lax
from jax.experimental import pallas as pl
from jax.experimental.pallas import tpu as pltpu

F32 = jnp.float32
BF16 = jnp.bfloat16
EPS = 1e-6

D_MODEL = 1024
CHUNK = 64
SEQ_TILE = 8
SSD_W = 512
SSD_HEAD_DIM = 64
SSD_HEADS = 8
SSD_GROUPS = 2
SSD_STATE = 128
CONV_K = 4
CONV_CH = 1024
S5_W = 512
S5_GROUPS = 32
S5_GROUP_CH = 16
S5_STATE = 64
S5_LANES = S5_GROUPS * S5_STATE
MEM_LEN = 256
X_HEADS = 4
X_HEAD_DIM = 256
N_EXPERT_GROUPS = 4
EXPERTS_PER_GROUP = 8
N_EXPERTS = 32
EXPERT_FF = 256
LANE = 128
PAD_ROWS = 8
S5_PITCH = CHUNK + PAD_ROWS
S5_SCAN_TILES = 4
PROMPT_EXPERT_ROWS = 256
SAMPLE_EXPERT_ROWS = 256
GATHER_SLOTS = 3
COMBINE_TILE = 512
PROMPT_ATTN_ROWS = 1024
SAMPLE_ATTN_SEQS = 4
ROW_TILES = D_MODEL // LANE
KEY_SHIFT = 20
VMEM_LIMIT = 56 * 1024 * 1024


def _rms(x, g):
    return x * lax.rsqrt(jnp.mean(x * x, axis=-1, keepdims=True) + EPS) * g


def _dot(a, b):
    return jnp.dot(a, b, preferred_element_type=F32)


def _dot_nt(a, b):
    return lax.dot_general(a, b, (((1,), (1,)), ((), ())), preferred_element_type=F32)


def _dot_tn(a, b):
    return lax.dot_general(a, b, (((0,), (0,)), ((), ())), preferred_element_type=F32)


def _const_spec(shape):
    nd = len(shape)
    return pl.BlockSpec(shape, lambda *_: (0,) * nd)


def _memkv_kernel(m_ref, g_ref, wk_ref, wv_ref, k_ref, v_ref, k5_ref, v5_ref):
    mn = _rms(m_ref[0], g_ref[...]).astype(BF16)
    k = _dot(mn, wk_ref[...])
    v = _dot(mn, wv_ref[...])
    k_ref[0] = k
    v_ref[0] = v
    for hd in range(X_HEADS):
        k5_ref[0, 0, :, hd, :] = k[:, hd * X_HEAD_DIM:(hd + 1) * X_HEAD_DIM]
        v5_ref[0, 0, :, hd, :] = v[:, hd * X_HEAD_DIM:(hd + 1) * X_HEAD_DIM]


def _memkv(mem, g, wk, wv):
    nb = mem.shape[0]
    flat_spec = pl.BlockSpec((1, MEM_LEN, D_MODEL), lambda i: (i, 0, 0))
    head_spec = pl.BlockSpec((1, 1, MEM_LEN, X_HEADS, X_HEAD_DIM), lambda i: (0, i, 0, 0, 0))
    flat_shape = jax.ShapeDtypeStruct((nb, MEM_LEN, D_MODEL), F32)
    head_shape = jax.ShapeDtypeStruct((1, nb, MEM_LEN, X_HEADS, X_HEAD_DIM), F32)
    return pl.pallas_call(
        _memkv_kernel,
        grid=(nb,),
        in_specs=[flat_spec, _const_spec((1, D_MODEL)), _const_spec((D_MODEL, D_MODEL)),
                  _const_spec((D_MODEL, D_MODEL))],
        out_specs=[flat_spec, flat_spec, head_spec, head_spec],
        out_shape=[flat_shape, flat_shape, head_shape, head_shape],
        compiler_params=pltpu.CompilerParams(dimension_semantics=("arbitrary",), vmem_limit_bytes=VMEM_LIMIT),
        name="memkv",
    )(mem, g, wk, wv)


def _softplus(x):
    return jnp.maximum(x, 0.0) + jnp.log1p(jnp.exp(-jnp.abs(x)))


def _mixer_kernel(n_chunks, x_ref, conv0_ref, ssd0_ref, s5re0_ref, s5im0_ref,
                  gmix_ref, wz_ref, wxbc_ref, wdt_ref, wu_ref, convw_ref, convb_ref,
                  dtb_ref, aneg_ref, eexp_ref, dskip_ref, ssdn_ref,
                  lamre_ref, lamim_ref, wb_ref, wc_ref, s5d_ref, wglu_ref, wout_ref,
                  h_ref, conv_ref, ssd_ref, s5re_ref, s5im_ref,
                  xn_ref, xpad_ref, xc_ref, dte_ref, cse_ref, crow_ref, st_ref, y_ref, z_ref, u_ref,
                  bure_ref, buim_ref, mix_ref):
    c = pl.program_id(1)
    rows = SEQ_TILE * CHUNK

    @pl.when(c == 0)
    def _():
        xpad_ref[:, PAD_ROWS - (CONV_K - 1):PAD_ROWS, :] = conv0_ref[...]
        s5re_ref[...] = s5re0_ref[...]
        s5im_ref[...] = s5im0_ref[...]

    x = x_ref[...].reshape(rows, D_MODEL)
    xn_ref[...] = _rms(x, gmix_ref[...]).astype(BF16)

    half_ch = S5_W // 2
    half_st = S5_LANES // 2

    cw = CONV_CH // 4
    grp = S5_SCAN_TILES * LANE

    def m_xbc(cb):
        ls = pl.ds(cb * cw, cw)
        xpad_ref[:, PAD_ROWS:, ls] = _dot(xn_ref[...], wxbc_ref[:, ls]).reshape(SEQ_TILE, CHUNK, cw)

    def m_proj(dst_ref, w_ref, hf):
        ls = pl.ds(hf * half_ch, half_ch)
        dst_ref[:, ls] = _dot(xn_ref[...], w_ref[:, ls])

    def m_s5_input(g, im):
        hf, q = divmod(g, 2)
        dst_ref = buim_ref if im else bure_ref
        bu = _dot(u_ref[:, hf * half_ch:(hf + 1) * half_ch].astype(BF16),
                  wb_ref[hf, :, im * half_st + q * grp:im * half_st + (q + 1) * grp])
        for k in range(S5_SCAN_TILES):
            for b in range(SEQ_TILE):
                dst_ref[g * S5_SCAN_TILES + k, pl.ds(b * S5_PITCH, CHUNK), :] = bu[b * CHUNK:(b + 1) * CHUNK,
                                                                                  k * LANE:(k + 1) * LANE]

    def v_conv(cb):
        ls = pl.ds(cb * cw, cw)
        acc = convb_ref[:, ls].reshape(1, 1, cw)
        for k in range(CONV_K):
            lo = PAD_ROWS - (CONV_K - 1) + k
            acc = acc + convw_ref[k:k + 1, ls].reshape(1, 1, cw) * xpad_ref[:, lo:lo + CHUNK, ls]
        xc_ref[:, ls] = (acc * jax.nn.sigmoid(acc)).reshape(rows, cw)

    def v_scan(g):
        tiles = [g * S5_SCAN_TILES + k for k in range(S5_SCAN_TILES)]
        lr = [jnp.broadcast_to(lamre_ref[:, pl.ds(k * LANE, LANE)], (SEQ_TILE, LANE)) for k in tiles]
        li = [jnp.broadcast_to(lamim_ref[:, pl.ds(k * LANE, LANE)], (SEQ_TILE, LANE)) for k in tiles]
        sr = [s5re_ref[:, pl.ds(k * LANE, LANE)] for k in tiles]
        si = [s5im_ref[:, pl.ds(k * LANE, LANE)] for k in tiles]
        for t in range(CHUNK):
            ts = pl.ds(t, SEQ_TILE, stride=S5_PITCH)
            for q, k in enumerate(tiles):
                nr = lr[q] * sr[q] - li[q] * si[q] + bure_ref[k, ts, :]
                ni = lr[q] * si[q] + li[q] * sr[q] + buim_ref[k, ts, :]
                sr[q], si[q] = nr, ni
                bure_ref[k, ts, :] = nr
                buim_ref[k, ts, :] = ni
        for q, k in enumerate(tiles):
            s5re_ref[:, pl.ds(k * LANE, LANE)] = sr[q]
            s5im_ref[:, pl.ds(k * LANE, LANE)] = si[q]

    m_xbc(0); m_proj(u_ref, wu_ref, 0); m_proj(u_ref, wu_ref, 1); v_conv(0)
    m_xbc(1); m_s5_input(0, 0); m_s5_input(0, 1); m_s5_input(1, 0); m_s5_input(1, 1); v_conv(1); v_scan(0)
    m_xbc(2); m_s5_input(2, 0); m_s5_input(2, 1); m_s5_input(3, 0); m_s5_input(3, 1); v_conv(2); v_scan(1)
    m_xbc(3); m_proj(z_ref, wz_ref, 0); m_proj(z_ref, wz_ref, 1); v_conv(3); v_scan(2)
    v_scan(3)
    hist = xpad_ref[:, PAD_ROWS + CHUNK - (CONV_K - 1):, :]
    conv_ref[...] = hist
    xpad_ref[:, PAD_ROWS - (CONV_K - 1):PAD_ROWS, :] = hist

    dt = _softplus(_dot(xn_ref[...], wdt_ref[...]) + dtb_ref[...])
    a = dt * aneg_ref[...]
    tpos = lax.broadcasted_iota(jnp.int32, (rows, LANE), 0) % CHUNK
    sh = 1
    while sh < CHUNK:
        a = a + jnp.where(tpos >= sh, pltpu.roll(a, sh, axis=0), 0.0)
        sh *= 2

    def expand_heads(v):
        hi = v.astype(BF16)
        r1 = v - hi.astype(F32)
        mid = r1.astype(BF16)
        lo = (r1 - mid.astype(F32)).astype(BF16)
        e = eexp_ref[...]
        return _dot(hi, e) + _dot(mid, e) + _dot(lo, e)

    dte_ref[...] = expand_heads(dt)
    cse_ref[...] = expand_heads(a)
    for b in range(SEQ_TILE):
        at = a[b * CHUNK:(b + 1) * CHUNK, :].T
        crow_ref[pl.ds(b, 1), :] = jnp.concatenate([at[h:h + 1, :] for h in range(SSD_HEADS)], axis=1)

    @pl.when(c == 0)
    def _():
        for b in range(SEQ_TILE):
            st_ref[b] = ssd0_ref[b].reshape(SSD_W, SSD_STATE).T

    gw = SSD_W // SSD_GROUPS
    heads_per_group = SSD_HEADS // SSD_GROUPS
    tri = (lax.broadcasted_iota(jnp.int32, (CHUNK, gw), 0)
           >= lax.broadcasted_iota(jnp.int32, (CHUNK, gw), 1) % CHUNK)
    same_head = ((lax.broadcasted_iota(jnp.int32, (gw, gw), 0) // SSD_HEAD_DIM)
                 == (lax.broadcasted_iota(jnp.int32, (gw, gw), 1) // SSD_HEAD_DIM))

    def seq_body(b, carry):
        r0 = pl.multiple_of(b * CHUNK, CHUNK)
        rs = pl.ds(r0, CHUNK)
        for g in range(SSD_GROUPS):
            ls = pl.ds(g * gw, gw)
            cse = cse_ref[rs, ls]
            cs_last = cse_ref[pl.ds(r0 + CHUNK - 1, 1), ls]
            decay = jnp.exp(jnp.where(tri, cse - crow_ref[pl.ds(b, 1), ls], -jnp.inf))
            xs = xc_ref[rs, ls]
            xdt = xs * dte_ref[rs, ls]
            b_bf = xc_ref[rs, pl.ds(SSD_W + g * SSD_STATE, SSD_STATE)].astype(BF16)
            c_bf = xc_ref[rs, pl.ds(SSD_W + SSD_GROUPS * SSD_STATE + g * SSD_STATE, SSD_STATE)].astype(BF16)
            gram = _dot_nt(c_bf, jnp.concatenate([b_bf] * heads_per_group, axis=0))
            xbd = jnp.where(same_head, jnp.concatenate([xdt] * heads_per_group, axis=0), 0.0).astype(BF16)
            y = _dot((gram * decay).astype(BF16), xbd)
            st = st_ref[b, :, ls]
            y = y + _dot(c_bf, st.astype(BF16)) * jnp.exp(cse)
            y = y + dskip_ref[:, ls] * xs
            y_ref[rs, ls] = y
            upd = _dot_tn(b_bf, (xdt * jnp.exp(cs_last - cse)).astype(BF16))
            st_ref[b, :, ls] = st * jnp.exp(cs_last) + upd
        return carry

    lax.fori_loop(0, SEQ_TILE, seq_body, 0)

    @pl.when(c == n_chunks - 1)
    def _():
        for b in range(SEQ_TILE):
            ssd_ref[b] = st_ref[b].T.reshape(SSD_HEADS, SSD_HEAD_DIM, SSD_STATE)

    def seq_rows(ref, k):
        return jnp.concatenate([ref[k, pl.ds(b * S5_PITCH, CHUNK), :] for b in range(SEQ_TILE)], axis=0)

    n_blk = S5_W // LANE
    blk_rows = rows // n_blk
    for j in range(n_blk):
        s_re = jnp.concatenate([seq_rows(bure_ref, 4 * j + k) for k in range(4)], axis=1)
        s_im = jnp.concatenate([seq_rows(buim_ref, 4 * j + k) for k in range(4)], axis=1)
        yj = _dot(s_re.astype(BF16), wc_ref[j, 0]) + _dot(s_im.astype(BF16), wc_ref[j, 1])
        rs = pl.ds(j * blk_rows, blk_rows)
        z = z_ref[rs, :]
        y = y_ref[rs, :] * (z * jax.nn.sigmoid(z))
        mix_ref[rs, :] = _rms(y, ssdn_ref[...]).astype(BF16)
        ls = pl.ds(j * LANE, LANE)
        u_ref[:, ls] = jax.nn.gelu(yj + s5d_ref[:, ls] * u_ref[:, ls])
    h_ssd = _dot(mix_ref[...], wout_ref[0:SSD_W, :])
    y5 = u_ref[...]
    y5 = y5 * jax.nn.sigmoid(_dot(y5.astype(BF16), wglu_ref[...]))
    h = x_ref[...].reshape(rows, D_MODEL) + (h_ssd + _dot(y5.astype(BF16), wout_ref[SSD_W:, :]))
    h_ref[...] = h.reshape(SEQ_TILE, CHUNK, D_MODEL)


def _mixer(x, conv0, ssd0, s5re0, s5im0, w):
    nb, seq, _ = x.shape
    grid = (nb // SEQ_TILE, seq // CHUNK)
    rows = SEQ_TILE * CHUNK
    weights = [w["gmix"], w["wz"], w["wxbc"], w["wdt"], w["wu"], w["convw"], w["convb"],
               w["dtb"], w["aneg"], w["eexp"], w["dskip"], w["ssdn"],
               w["lamre"], w["lamim"], w["wb"], w["wc"], w["s5d"], w["wglu"], w["wout"]]
    state_specs = [pl.BlockSpec((SEQ_TILE, CONV_K - 1, CONV_CH), lambda i, c: (i, 0, 0)),
                   pl.BlockSpec((SEQ_TILE, SSD_HEADS, SSD_HEAD_DIM, SSD_STATE), lambda i, c: (i, 0, 0, 0)),
                   pl.BlockSpec((SEQ_TILE, S5_LANES), lambda i, c: (i, 0)),
                   pl.BlockSpec((SEQ_TILE, S5_LANES), lambda i, c: (i, 0))]
    x_spec = pl.BlockSpec((SEQ_TILE, CHUNK, D_MODEL), lambda i, c: (i, c, 0))
    return pl.pallas_call(
        functools.partial(_mixer_kernel, grid[1]),
        grid=grid,
        in_specs=[x_spec] + state_specs + [_const_spec(a.shape) for a in weights],
        out_specs=[x_spec] + state_specs,
        out_shape=[jax.ShapeDtypeStruct(x.shape, F32),
                   jax.ShapeDtypeStruct(conv0.shape, F32), jax.ShapeDtypeStruct(ssd0.shape, F32),
                   jax.ShapeDtypeStruct(s5re0.shape, F32), jax.ShapeDtypeStruct(s5im0.shape, F32)],
        scratch_shapes=[
            pltpu.VMEM((rows, D_MODEL), BF16),
            pltpu.VMEM((SEQ_TILE, PAD_ROWS + CHUNK, CONV_CH), F32),
            pltpu.VMEM((rows, CONV_CH), F32),
            pltpu.VMEM((rows, SSD_W), F32),
            pltpu.VMEM((rows, SSD_W), F32),
            pltpu.VMEM((SEQ_TILE, SSD_W), F32),
            pltpu.VMEM((SEQ_TILE, SSD_STATE, SSD_W), F32),
            pltpu.VMEM((rows, SSD_W), F32),
            pltpu.VMEM((rows, SSD_W), F32),
            pltpu.VMEM((rows, S5_W), F32),
            pltpu.VMEM((S5_LANES // LANE, SEQ_TILE * S5_PITCH, LANE), F32),
            pltpu.VMEM((S5_LANES // LANE, SEQ_TILE * S5_PITCH, LANE), F32),
            pltpu.VMEM((rows, SSD_W), BF16),
        ],
        compiler_params=pltpu.CompilerParams(dimension_semantics=("arbitrary", "arbitrary"),
                                             vmem_limit_bytes=VMEM_LIMIT),
        name="mixer",
    )(x, conv0, ssd0, s5re0, s5im0, *weights)


def _attn_kernel(n_seq, tile, heads_axis, h_ref, k_ref, v_ref, gx_ref, wq_ref, wo_ref, gffn_ref, wr_ref, br_ref,
                 h2_ref, tn_ref, eid_ref, gate_ref, o_ref, kv_ref):
    rows = n_seq * tile
    h1 = h_ref[...].reshape(rows, D_MODEL)
    xn = _rms(h1, gx_ref[...]).astype(BF16)
    q = _dot(xn, wq_ref[...])
    scale = X_HEAD_DIM ** -0.5
    for sq in range(n_seq):
        rs = slice(sq * tile, (sq + 1) * tile)
        for hd in range(X_HEADS):
            ls = slice(hd * X_HEAD_DIM, (hd + 1) * X_HEAD_DIM)
            if heads_axis:
                kv_ref[0] = k_ref[0, sq, :, hd, :]
                kv_ref[1] = v_ref[0, sq, :, hd, :]
                kh = kv_ref[0].astype(BF16)
                vh = kv_ref[1].astype(BF16)
            else:
                kh = k_ref[sq, :, ls].astype(BF16)
                vh = v_ref[sq, :, ls].astype(BF16)
            s = _dot_nt(q[rs, ls].astype(BF16), kh) * scale
            s = s - jnp.max(s, axis=-1, keepdims=True)
            p = jnp.exp(s)
            p = p / jnp.sum(p, axis=-1, keepdims=True)
            o_ref[rs, ls] = _dot(p.astype(BF16), vh).astype(BF16)
    h2 = h1 + _dot(o_ref[...], wo_ref[...])
    h2_ref[...] = h2.reshape(n_seq, tile, D_MODEL)

    tn = _rms(h2, gffn_ref[...]).astype(BF16)
    tn32 = tn.astype(F32)
    for sq in range(n_seq):
        for sb in range(ROW_TILES):
            tn_ref[sq, pl.ds(sb, tile, stride=ROW_TILES), :] = tn32[sq * tile:(sq + 1) * tile,
                                                                  sb * LANE:(sb + 1) * LANE]
    lt = (_dot(tn, wr_ref[...]) + br_ref[...]).T
    sub = lax.broadcasted_iota(jnp.int32, (EXPERTS_PER_GROUP, rows), 0)
    big = jnp.int32(2 ** 30)
    neg = -jnp.inf

    def first_max(x):
        m = jnp.max(x, axis=0, keepdims=True)
        return m, jnp.min(jnp.where(x == m, sub, big), axis=0, keepdims=True)

    gl = jnp.where(sub < N_EXPERT_GROUPS, lt[N_EXPERTS:N_EXPERTS + EXPERTS_PER_GROUP, :], neg)
    gmax, g_idx = first_max(gl)
    g_prob = 1.0 / jnp.sum(jnp.exp(gl - gmax), axis=0, keepdims=True)
    el = lt[0:EXPERTS_PER_GROUP, :]
    for g in range(1, N_EXPERT_GROUPS):
        el = jnp.where(g_idx == g, lt[g * EXPERTS_PER_GROUP:(g + 1) * EXPERTS_PER_GROUP, :], el)
    m1, i1 = first_max(el)
    m2, i2 = first_max(jnp.where(sub == i1, neg, el))
    e2 = jnp.exp(m2 - m1)
    den = 1.0 + e2
    base = g_idx * EXPERTS_PER_GROUP
    zeros = jnp.zeros((LANE - EXPERTS_PER_GROUP, rows), F32)
    eid_t = jnp.where(sub == 0, base + i1, jnp.where(sub == 1, base + i2, 0)).astype(F32)
    gate_t = jnp.where(sub == 0, (1.0 / den) * g_prob, jnp.where(sub == 1, (e2 / den) * g_prob, 0.0))
    eid = jnp.concatenate([eid_t, zeros], axis=0).T.astype(jnp.int32)
    gate = jnp.concatenate([gate_t, zeros], axis=0).T
    eid_ref[...] = eid.reshape(n_seq, tile, LANE)
    gate_ref[...] = gate.reshape(n_seq, tile, LANE)


def _attn(h1, mem_k, mem_v, w, n_seq, tile):
    nb, seq, _ = h1.shape
    heads_axis = mem_k.ndim == 5
    row_spec = pl.BlockSpec((n_seq, tile, D_MODEL), lambda b, i: (b, i, 0))
    if heads_axis:
        kv_spec = pl.BlockSpec((1, n_seq, MEM_LEN, X_HEADS, X_HEAD_DIM), lambda b, i: (0, b, 0, 0, 0))
    else:
        kv_spec = pl.BlockSpec((n_seq, MEM_LEN, D_MODEL), lambda b, i: (b, 0, 0))
    lane_spec = pl.BlockSpec((n_seq, tile, LANE), lambda b, i: (b, i, 0))
    weights = [w["gx"], w["wq"], w["wo"], w["gffn"], w["wr"], w["br"]]
    return pl.pallas_call(
        functools.partial(_attn_kernel, n_seq, tile, heads_axis),
        grid=(nb // n_seq, seq // tile),
        in_specs=[row_spec, kv_spec, kv_spec] + [_const_spec(a.shape) for a in weights],
        out_specs=[row_spec, pl.BlockSpec((n_seq, tile * ROW_TILES, LANE), lambda b, i: (b, i, 0)),
                   lane_spec, lane_spec],
        out_shape=[jax.ShapeDtypeStruct(h1.shape, F32),
                   jax.ShapeDtypeStruct((nb, seq * ROW_TILES, LANE), F32),
                   jax.ShapeDtypeStruct((nb, seq, LANE), jnp.int32),
                   jax.ShapeDtypeStruct((nb, seq, LANE), F32)],
        scratch_shapes=[pltpu.VMEM((n_seq * tile, D_MODEL), BF16),
                        pltpu.VMEM((2, MEM_LEN, X_HEAD_DIM), F32)],
        compiler_params=pltpu.CompilerParams(dimension_semantics=("arbitrary", "arbitrary"),
                                             vmem_limit_bytes=VMEM_LIMIT),
        name="attn_router",
    )(h1, mem_k, mem_v, *weights)


def _route_plan(eid, tile):
    n_tok = eid.shape[0]
    n_asg = 2 * n_tok
    assert n_asg % tile == 0 and n_asg < (1 << KEY_SHIFT)
    n_tiles = n_asg // tile
    n_items = n_tiles + N_EXPERTS
    i32 = jnp.int32
    e_flat = eid.reshape(n_asg)
    a_idx = jnp.arange(n_asg, dtype=i32)
    keys = lax.sort(e_flat * (1 << KEY_SHIFT) + a_idx)
    order = keys & ((1 << KEY_SHIFT) - 1)
    row_token = (order // 2).reshape(n_tiles, 1, tile)
    pos = lax.sort((order, a_idx), num_keys=1)[1]
    sorted_e = keys >> KEY_SHIFT
    seg_end = jnp.sum((sorted_e[None, :] <= jnp.arange(N_EXPERTS, dtype=i32)[:, None]).astype(i32), axis=1)
    counts = seg_end - jnp.concatenate([jnp.zeros((1,), i32), seg_end[:-1]])
    seg_start = seg_end - counts
    first_tile = seg_start // tile
    last_tile = (seg_end - 1) // tile
    items_e = jnp.where(counts > 0, last_tile - first_tile + 1, 0)
    it_end = jnp.cumsum(items_e)
    it_start = it_end - items_e
    w = jnp.arange(n_items, dtype=i32)
    wc = jnp.minimum(w, it_end[-1] - 1)
    it_expert = jnp.sum((wc[:, None] >= it_end[None, :]).astype(i32), axis=1)
    it_onehot = (it_expert[:, None] == jnp.arange(N_EXPERTS, dtype=i32)[None, :]).astype(i32)

    def of_item(table):
        return jnp.sum(it_onehot * table[None, :], axis=1)

    it_tile = of_item(first_tile) + (wc - of_item(it_start))
    it_lo = jnp.clip(of_item(seg_start) - it_tile * tile, 0, tile)
    it_hi = jnp.clip(of_item(seg_end) - it_tile * tile, 0, tile)
    it_valid = (w < it_end[-1]).astype(i32)
    items = tuple(a.astype(i32) for a in (it_tile, it_expert, it_lo, it_hi, it_valid))
    return pos.reshape(n_tok, 2), row_token, items


def _token_copy(src_hbm, src_row, dst, dst_row, sem):
    return pltpu.make_async_copy(src_hbm.at[pl.ds(pl.multiple_of(src_row * ROW_TILES, ROW_TILES), ROW_TILES), :],
                                 dst.at[pl.ds(pl.multiple_of(dst_row * ROW_TILES, ROW_TILES), ROW_TILES), :], sem)


def _experts_kernel(n_tiles, n_items, tile, tile_ref, exp_ref, lo_ref, hi_ref, valid_ref, rt_cur_ref, rt_nxt_ref,
                    rt_nxt2_ref, tn_hbm, wg_ref, wu_ref, wd_ref, y_ref, xbuf, xb_ref, wgu_bf, wd_bf, sem):
    w = pl.program_id(0)
    j = tile_ref[w]
    lo = lo_ref[w]
    hi = hi_ref[w]
    valid = valid_ref[w] == 1
    first = jnp.logical_and(valid, lo == 0)
    slot = j % GATHER_SLOTS
    slot_ahead = (j + 2) % GATHER_SLOTS

    @pl.when(jnp.logical_or(w == 0, exp_ref[w] != exp_ref[jnp.maximum(w - 1, 0)]))
    def _():
        wgu_bf[:, :EXPERT_FF] = wg_ref[0, 0].astype(BF16)
        wgu_bf[:, EXPERT_FF:] = wu_ref[0, 0].astype(BF16)
        wd_bf[...] = wd_ref[0, 0].astype(BF16)

    def wait_gather(s):
        def body(r, carry):
            _token_copy(tn_hbm, 0, xbuf.at[s], r, sem.at[s]).wait()
            return carry
        lax.fori_loop(0, tile, body, 0, unroll=8)

    def expert_rows():
        gu = _dot(xb_ref[...], wgu_bf[...])
        gate = gu[:, :EXPERT_FF]
        hid = (gate * jax.nn.sigmoid(gate)) * gu[:, EXPERT_FF:]
        return _dot(hid.astype(BF16), wd_bf[...])

    def start_gather(rt_ref, s):
        def body(r2, carry):
            for p in range(2):
                r = 2 * r2 + p
                _token_copy(tn_hbm, rt_ref[0, 0, r], xbuf.at[s], r, sem.at[s]).start(priority=p)
            return carry
        lax.fori_loop(0, tile // 2, body, 0, unroll=4)

    @pl.when(w == 0)
    def _():
        start_gather(rt_cur_ref, 0)
        start_gather(rt_nxt_ref, 1)

    @pl.when(first)
    def _():
        wait_gather(slot)
        for sb in range(ROW_TILES):
            xb_ref[:, sb * LANE:(sb + 1) * LANE] = xbuf[slot, pl.ds(sb, tile, stride=ROW_TILES), :].astype(BF16)
        for r in range(tile):
            _token_copy(tn_hbm, rt_nxt2_ref[0, 0, r], xbuf.at[slot_ahead], r,
                        sem.at[slot_ahead]).start(priority=r % 2)
        y = expert_rows()
        for sb in range(ROW_TILES):
            y_ref[pl.ds(sb, tile, stride=ROW_TILES), :] = y[:, sb * LANE:(sb + 1) * LANE]

    @pl.when(jnp.logical_and(valid, lo != 0))
    def _():
        y = expert_rows()
        row = lax.broadcasted_iota(jnp.int32, (tile, LANE), 0)
        keep = jnp.logical_and(row >= lo, row < hi)
        for sb in range(ROW_TILES):
            rows = pl.ds(sb, tile, stride=ROW_TILES)
            y_ref[rows, :] = jnp.where(keep, y[:, sb * LANE:(sb + 1) * LANE], y_ref[rows, :])

    @pl.when(w == n_items - 1)
    def _():
        wait_gather(n_tiles % GATHER_SLOTS)
        wait_gather((n_tiles + 1) % GATHER_SLOTS)


def _experts(tn, row_token, items, w):
    n_tiles, _, tile = row_token.shape
    n_items = items[0].shape[0]

    def smem_tile(imap):
        return pl.BlockSpec((1, 1, tile), imap, memory_space=pltpu.SMEM)

    grid_spec = pltpu.PrefetchScalarGridSpec(
        num_scalar_prefetch=5,
        grid=(n_items,),
        in_specs=[smem_tile(lambda i, tl, ex, lo, hi, va: (tl[i], 0, 0)),
                  smem_tile(lambda i, tl, ex, lo, hi, va: (jnp.minimum(tl[i] + 1, n_tiles - 1), 0, 0)),
                  smem_tile(lambda i, tl, ex, lo, hi, va: (jnp.minimum(tl[i] + 2, n_tiles - 1), 0, 0)),
                  pl.BlockSpec(memory_space=pl.ANY),
                  pl.BlockSpec((1, 1, D_MODEL, EXPERT_FF), lambda i, tl, ex, lo, hi, va: (0, ex[i], 0, 0)),
                  pl.BlockSpec((1, 1, D_MODEL, EXPERT_FF), lambda i, tl, ex, lo, hi, va: (0, ex[i], 0, 0)),
                  pl.BlockSpec((1, 1, EXPERT_FF, D_MODEL), lambda i, tl, ex, lo, hi, va: (0, ex[i], 0, 0))],
        out_specs=pl.BlockSpec((tile * ROW_TILES, LANE), lambda i, tl, ex, lo, hi, va: (tl[i], 0)),
        scratch_shapes=[pltpu.VMEM((GATHER_SLOTS, tile * ROW_TILES, LANE), F32),
                        pltpu.VMEM((tile, D_MODEL), BF16),
                        pltpu.VMEM((D_MODEL, 2 * EXPERT_FF), BF16),
                        pltpu.VMEM((EXPERT_FF, D_MODEL), BF16),
                        pltpu.SemaphoreType.DMA((GATHER_SLOTS,))],
    )
    return pl.pallas_call(
        functools.partial(_experts_kernel, n_tiles, n_items, tile),
        grid_spec=grid_spec,
        out_shape=jax.ShapeDtypeStruct((n_tiles * tile * ROW_TILES, LANE), F32),
        compiler_params=pltpu.CompilerParams(dimension_semantics=("arbitrary",), vmem_limit_bytes=VMEM_LIMIT),
        name="experts",
    )(*items, row_token, row_token, row_token, tn, w["wgate"], w["wup"], w["wdown"])


def _combine_kernel(n_steps, pos_cur_ref, pos_nxt_ref, ys_hbm, gate_ref, h2_ref, gfin_ref, out_ref, ybuf, sem):
    i = pl.program_id(0)
    slot = i % 2

    def wait_gather(s):
        def body(r, carry):
            for k in range(2):
                _token_copy(ys_hbm, 0, ybuf.at[s, k], r, sem.at[s]).wait()
            return carry
        lax.fori_loop(0, COMBINE_TILE, body, 0, unroll=8)

    def start_gather(pos_ref, s):
        def body(r, carry):
            for k in range(2):
                _token_copy(ys_hbm, pos_ref[0, 0, 2 * r + k], ybuf.at[s, k], r,
                            sem.at[s]).start(priority=k)
            return carry
        lax.fori_loop(0, COMBINE_TILE, body, 0, unroll=8)

    @pl.when(i == 0)
    def _():
        start_gather(pos_cur_ref, 0)

    @pl.when(i + 1 < n_steps)
    def _():
        start_gather(pos_nxt_ref, 1 - slot)

    wait_gather(slot)

    gates = gate_ref[...]
    g0 = gates[:, 0:1]
    g1 = gates[:, 1:2]
    ssq = jnp.zeros((COMBINE_TILE, 1), F32)
    for sb in range(ROW_TILES):
        rows = pl.ds(sb, COMBINE_TILE, stride=ROW_TILES)
        cols = pl.ds(sb * LANE, LANE)
        v = h2_ref[:, cols] + (ybuf[slot, 0, rows, :] * g0 + ybuf[slot, 1, rows, :] * g1)
        ssq = ssq + jnp.sum(v * v, axis=-1, keepdims=True)
        out_ref[:, cols] = v
    out_ref[...] = out_ref[...] * lax.rsqrt(ssq * (1.0 / D_MODEL) + EPS) * gfin_ref[...]


def _combine(ys, pos, gates, h2, gfin):
    n_tok = h2.shape[0]
    n_tiles = n_tok // COMBINE_TILE
    pos_tiles = pos.reshape(n_tiles, 1, 2 * COMBINE_TILE)

    def smem_tile(imap):
        return pl.BlockSpec((1, 1, 2 * COMBINE_TILE), imap, memory_space=pltpu.SMEM)

    row_spec = pl.BlockSpec((COMBINE_TILE, D_MODEL), lambda i: (i, 0))
    return pl.pallas_call(
        functools.partial(_combine_kernel, n_tiles),
        grid=(n_tiles,),
        in_specs=[smem_tile(lambda i: (i, 0, 0)),
                  smem_tile(lambda i: (jnp.minimum(i + 1, n_tiles - 1), 0, 0)),
                  pl.BlockSpec(memory_space=pl.ANY),
                  pl.BlockSpec((COMBINE_TILE, LANE), lambda i: (i, 0)),
                  row_spec, _const_spec((1, D_MODEL))],
        out_specs=row_spec,
        out_shape=jax.ShapeDtypeStruct((n_tok, D_MODEL), F32),
        scratch_shapes=[pltpu.VMEM((2, 2, COMBINE_TILE * ROW_TILES, LANE), F32), pltpu.SemaphoreType.DMA((2,))],
        compiler_params=pltpu.CompilerParams(dimension_semantics=("arbitrary",), vmem_limit_bytes=VMEM_LIMIT),
        name="combine",
    )(pos_tiles, pos_tiles, ys, gates, h2, gfin)


def _row(v, width=None):
    v = v.astype(F32).reshape(1, -1)
    if width is not None and v.shape[1] < width:
        v = jnp.pad(v, ((0, 0), (0, width - v.shape[1])))
    return v


def _s5_params(a_re, a_im, log_dt, b_re, b_im, c_re, c_im):
    dt = jnp.exp(log_dt)[:, None]
    mag = jnp.exp(a_re * dt)
    lam_re, lam_im = mag * jnp.cos(a_im * dt), mag * jnp.sin(a_im * dt)
    den = a_re * a_re + a_im * a_im
    f_re = ((lam_re - 1.0) * a_re + lam_im * a_im) / den
    f_im = (lam_im * a_re - (lam_re - 1.0) * a_im) / den
    bb_re = f_re[..., None] * b_re - f_im[..., None] * b_im
    bb_im = f_re[..., None] * b_im + f_im[..., None] * b_re
    half = S5_GROUPS // 2

    def in_proj(bb):
        bb = bb.reshape(2, half, S5_STATE, S5_GROUP_CH)
        eye = jnp.eye(half, dtype=F32)
        m = jnp.einsum("hgnc,gk->hgckn", bb, eye)
        return m.reshape(2, half * S5_GROUP_CH, half * S5_STATE)

    wb = jnp.concatenate([in_proj(bb_re), in_proj(bb_im)], axis=2).astype(BF16)

    def out_proj(cc):
        q = S5_GROUPS // 4
        cc = cc.reshape(4, q, S5_GROUP_CH, S5_STATE)
        eye = jnp.eye(q, dtype=F32)
        m = jnp.einsum("qgcn,gk->qgnkc", cc, eye)
        return m.reshape(4, q * S5_STATE, q * S5_GROUP_CH)

    wc = jnp.stack([out_proj(c_re), -out_proj(c_im)], axis=1).astype(BF16)
    return lam_re.reshape(1, S5_LANES), lam_im.reshape(1, S5_LANES), wb, wc


def kernel(x_prompt, x_sample, cache_conv, state_ssd, state_s5_re, state_s5_im, cache_mem_k, cache_mem_v,
           mem_prompt, norm_mix, w_in, conv_w, conv_b, dt_bias, a_log, d_skip, ssd_norm, s5_a_re, s5_a_im,
           s5_log_dt, s5_b_re, s5_b_im, s5_c_re, s5_c_im, s5_d, s5_w_glu, w_out, norm_x, norm_mem, w_q,
           w_k, w_v, w_o, norm_ffn, w_router_group, b_router_group, w_router_expert, b_router_expert,
           w_gate, w_up, w_down, norm_final):
    depth = norm_mix.shape[0]
    assert depth == 1
    l = 0
    bp, seq_p, _ = x_prompt.shape
    bs, seq_s, _ = x_sample.shape

    o1 = SSD_W
    o2 = o1 + CONV_CH
    o3 = o2 + SSD_HEADS
    lam_re, lam_im, wb, wc = _s5_params(s5_a_re[l], s5_a_im[l], s5_log_dt[l], s5_b_re[l], s5_b_im[l],
                                        s5_c_re[l], s5_c_im[l])
    wm = {
        "gmix": _row(norm_mix[l]),
        "wz": w_in[l][:, :o1].astype(BF16),
        "wxbc": w_in[l][:, o1:o2].astype(BF16),
        "wdt": jnp.pad(w_in[l][:, o2:o3], ((0, 0), (0, LANE - SSD_HEADS))).astype(BF16),
        "wu": w_in[l][:, o3:].astype(BF16),
        "convw": conv_w[l].astype(F32), "convb": _row(conv_b[l]),
        "dtb": _row(dt_bias[l], LANE), "aneg": _row(-jnp.exp(a_log[l].astype(F32)), LANE),
        "eexp": (jnp.arange(LANE, dtype=jnp.int32)[:, None]
                 == jnp.arange(SSD_W, dtype=jnp.int32)[None, :] // SSD_HEAD_DIM).astype(BF16),
        "dskip": _row(jnp.repeat(d_skip[l].astype(F32), SSD_HEAD_DIM)), "ssdn": _row(ssd_norm[l]),
        "lamre": lam_re, "lamim": lam_im, "wb": wb, "wc": wc,
        "s5d": _row(s5_d[l]), "wglu": s5_w_glu[l].astype(BF16), "wout": w_out[l].astype(BF16),
    }
    wr = jnp.concatenate([w_router_expert[l].reshape(D_MODEL, N_EXPERTS), w_router_group[l]], axis=1)
    br = jnp.concatenate([b_router_expert[l].reshape(N_EXPERTS), b_router_group[l]])
    wa = {
        "gx": _row(norm_x[l]), "wq": w_q[l].astype(BF16), "wo": w_o[l].astype(BF16),
        "gffn": _row(norm_ffn[l]),
        "wr": jnp.pad(wr, ((0, 0), (0, LANE - wr.shape[1]))).astype(BF16), "br": _row(br, LANE),
    }
    we = {"wgate": w_gate, "wup": w_up, "wdown": w_down}
    gfin = _row(norm_final)

    mk_p, mv_p, mk_state, mv_state = _memkv(mem_prompt, _row(norm_mem[l]), w_k[l].astype(BF16), w_v[l].astype(BF16))

    def group(x, conv0, ssd0, s5re0, s5im0, mem_k, mem_v, attn_seqs, attn_tile, expert_rows):
        nb, seq, _ = x.shape
        n_tok = nb * seq
        h1, conv, ssd, s5re, s5im = _mixer(x, conv0, ssd0, s5re0.reshape(nb, S5_LANES),
                                           s5im0.reshape(nb, S5_LANES), wm)
        h2, tn, eid, gates = _attn(h1, mem_k, mem_v, wa, attn_seqs, attn_tile)
        pos, row_token, items = _route_plan(eid.reshape(n_tok, LANE)[:, :2], expert_rows)
        ys = _experts(tn.reshape(n_tok * ROW_TILES, LANE), row_token, items, we)
        y = _combine(ys, pos, gates.reshape(n_tok, LANE), h2.reshape(n_tok, D_MODEL), gfin)
        return (y.reshape(nb, seq, D_MODEL), conv[None], ssd[None],
                s5re.reshape(1, nb, S5_GROUPS, S5_STATE), s5im.reshape(1, nb, S5_GROUPS, S5_STATE))

    zeros = lambda *s: jnp.zeros(s, F32)
    y_p, conv_p, ssd_p, s5re_p, s5im_p = group(
        x_prompt, zeros(bp, CONV_K - 1, CONV_CH), zeros(bp, SSD_HEADS, SSD_HEAD_DIM, SSD_STATE),
        zeros(bp, S5_GROUPS, S5_STATE), zeros(bp, S5_GROUPS, S5_STATE), mk_p, mv_p,
        1, min(PROMPT_ATTN_ROWS, seq_p), PROMPT_EXPERT_ROWS)
    y_s, conv_s, ssd_s, s5re_s, s5im_s = group(
        x_sample, cache_conv.reshape(cache_conv.shape[1:]), state_ssd.reshape(state_ssd.shape[1:]),
        state_s5_re.reshape(state_s5_re.shape[1:]), state_s5_im.reshape(state_s5_im.shape[1:]),
        cache_mem_k, cache_mem_v, SAMPLE_ATTN_SEQS, seq_s, SAMPLE_EXPERT_ROWS)

    return (y_p, y_s, conv_p, ssd_p, s5re_p, s5im_p, mk_state, mv_state, conv_s, ssd_s, s5re_s, s5im_s)
```
